```python
import math
import jax, jax.numpy as jnp
from jax import lax
import numpy as np

D_MODEL = 2048
BATCH = 4
SEQ = 2048
DEPTH = 4
DEC_BATCH = 32
DEC_SEQ = 4
PAST_LEN = 16384
PAGE_SIZE = 128

N_A_LAYERS = DEPTH // 2
N_B_LAYERS = DEPTH - N_A_LAYERS
HD_A = 128
H_A = D_MODEL // HD_A
KV_A = 4
HD_B = 64
H_B = D_MODEL // HD_B
KV_B = 4
WINDOW = 128
Q_BLOCK = 128
N_BUCKETS = 32
MAX_EXACT = N_BUCKETS // 2
MAX_DISTANCE = WINDOW
D_FF = 11 * D_MODEL // 4
CONV_W = 3
EPS = 1e-6
POOL_NUM = 5
POOL_DEN = 4
SB_BIAS_INIT = -6.0

kernel_name = "yoco_stickbreak_swa_sink_convffn_step"


def rmsnorm(x, g):
    xf = x.astype(jnp.float32)
    y = xf * lax.rsqrt(jnp.mean(xf * xf, axis=-1, keepdims=True) + EPS)
    return (y * g.astype(jnp.float32)).astype(x.dtype)


def t5_bucket(dist):
    d = jnp.maximum(dist, 0)
    large = MAX_EXACT + (jnp.log(jnp.maximum(d, 1).astype(jnp.float32) / MAX_EXACT)
                         / math.log(MAX_DISTANCE / MAX_EXACT) * (N_BUCKETS - MAX_EXACT)).astype(jnp.int32)
    large = jnp.minimum(large, N_BUCKETS - 1)
    return jnp.where(d < MAX_EXACT, d, large)


def qkv_a(hn, w):
    b, t = hn.shape[:2]
    qkv = hn @ w
    q = qkv[..., :H_A * HD_A].reshape(b, t, H_A, HD_A)
    k = qkv[..., H_A * HD_A:(H_A + KV_A) * HD_A].reshape(b, t, KV_A, HD_A)
    v = qkv[..., (H_A + KV_A) * HD_A:].reshape(b, t, KV_A, HD_A)
    return q, k, v


def stick_breaking(q, k, v, q_pos, k_pos, sb_bias):
    b, tq = q.shape[:2]
    grp = H_A // KV_A
    qg = q.reshape(b, tq, KV_A, grp, HD_A)
    z = jnp.einsum('bqkgd,bskd->bkgqs', qg, k).astype(jnp.float32) * (HD_A ** -0.5)
    z = z + sb_bias.astype(jnp.float32).reshape(KV_A, grp)[:, :, None, None]
    valid = k_pos[None, :] < q_pos[:, None]
    log_stay = jnp.where(valid, jax.nn.log_sigmoid(-z), 0.0)
    later = lax.cumsum(log_stay, axis=4, reverse=True) - log_stay
    a = jnp.where(valid, jnp.exp(jax.nn.log_sigmoid(z) + later), 0.0)
    o = jnp.einsum('bkgqs,bskd->bqkgd', a.astype(v.dtype), v)
    return o.reshape(b, tq, H_A * HD_A)


def sb_prompt(q, k, v, sb_bias):
    b, s = q.shape[:2]
    nb = s // Q_BLOCK
    k_pos = jnp.arange(s)
    qb = q.reshape(b, nb, Q_BLOCK, H_A, HD_A).transpose(1, 0, 2, 3, 4)

    def blk(args):
        qi, i = args
        q_pos = i * Q_BLOCK + jnp.arange(Q_BLOCK)
        return stick_breaking(qi, k, v, q_pos, k_pos, sb_bias)

    o = lax.map(blk, (qb, jnp.arange(nb)))
    return o.transpose(1, 0, 2, 3).reshape(b, s, H_A * HD_A)


def shared_kv_b(h, g_kv, w_kv):
    b, t = h.shape[:2]
    kv = rmsnorm(h, g_kv) @ w_kv
    k = kv[..., :KV_B * HD_B].reshape(b, t, KV_B, HD_B)
    v = kv[..., KV_B * HD_B:].reshape(b, t, KV_B, HD_B)
    return k, v


def window_attend(q, k, v, q_pos, k_pos, sinks, rel_bias):
    b, n, tq = q.shape[:3]
    tk = k.shape[2]
    grp = H_B // KV_B
    qg = q.reshape(b, n, tq, KV_B, grp, HD_B)
    s = jnp.einsum('bnqkgd,bnskd->bnkgqs', qg, k).astype(jnp.float32) * (HD_B ** -0.5)
    dist = q_pos[:, :, None] - k_pos[:, None, :]
    valid = (dist >= 0) & (dist < WINDOW) & (k_pos[:, None, :] >= 0)
    bias = rel_bias[t5_bucket(dist)].astype(jnp.float32)
    bias = bias.reshape(n, tq, tk, KV_B, grp).transpose(0, 3, 4, 1, 2)
    s = jnp.where(valid[:, None, None], s + bias[None], -jnp.inf)
    sink = sinks.astype(jnp.float32).reshape(KV_B, grp)[:, :, None, None]
    m = jnp.maximum(jnp.max(s, axis=-1, keepdims=True), sink)
    p = jnp.exp(s - m)
    denom = jnp.sum(p, axis=-1, keepdims=True) + jnp.exp(sink - m)
    o = jnp.einsum('bnkgqs,bnskd->bnqkgd', (p / denom).astype(v.dtype), v)
    return o.reshape(b, n, tq, H_B * HD_B)


def window_prompt(q, k, v, sinks, rel_bias):
    b, s = q.shape[:2]
    nb = s // WINDOW

    def pad_blocks(t):
        tp = jnp.pad(t, ((0, 0), (WINDOW, 0), (0, 0), (0, 0)))
        return tp.reshape(b, nb + 1, WINDOW, t.shape[2], t.shape[3])

    kp, vp = pad_blocks(k), pad_blocks(v)
    kb = jnp.concatenate([kp[:, :-1], kp[:, 1:]], axis=2)
    vb = jnp.concatenate([vp[:, :-1], vp[:, 1:]], axis=2)
    qb = q.reshape(b, nb, WINDOW, H_B, HD_B)
    blk = jnp.arange(nb)[:, None]
    q_pos = blk * WINDOW + jnp.arange(WINDOW)[None, :]
    k_pos = (blk - 1) * WINDOW + jnp.arange(2 * WINDOW)[None, :]
    o = window_attend(qb, kb, vb, q_pos, k_pos, sinks, rel_bias)
    return o.reshape(b, s, H_B * HD_B)


def conv_ffn(hn, conv_state, w_gate, w_up, w_down, conv_w, conv_b):
    t = hn.shape[1]
    g = hn @ w_gate
    u = hn @ w_up
    gp = jnp.concatenate([conv_state.astype(g.dtype), g], axis=1)
    gc = conv_b + gp[:, 0:t] * conv_w[0]
    for i in range(1, CONV_W):
        gc = gc + gp[:, i:i + t] * conv_w[i]
    y = (jax.nn.gelu(gc) * u) @ w_down
    return y, gp[:, -(CONV_W - 1):]


def setup_inputs(seed: int = 0) -> dict:
    key = jax.random.key(seed)
    ks = jax.random.split(key, 32)
    f32 = jnp.float32
    n_pages = PAST_LEN // PAGE_SIZE
    n_pool = (DEC_BATCH * n_pages * POOL_NUM) // POOL_DEN

    def nrm(k, shape, scale):
        return jax.random.normal(k, shape, f32) * scale

    perm = jax.random.permutation(ks[0], n_pool)[:DEC_BATCH * n_pages]
    page_table = perm.reshape(DEC_BATCH, n_pages).astype(jnp.int32)
    return {
        "x_prompt": nrm(ks[1], (BATCH, SEQ, D_MODEL), 1.0),
        "x_sample": nrm(ks[2], (DEC_BATCH, DEC_SEQ, D_MODEL), 1.0),
        "cache_k_a": nrm(ks[3], (N_A_LAYERS, n_pool, PAGE_SIZE, KV_A, HD_A), 1.0),
        "cache_v_a": nrm(ks[4], (N_A_LAYERS, n_pool, PAGE_SIZE, KV_A, HD_A), 1.0),
        "page_table": page_table,
        "cache_k_b": nrm(ks[5], (DEC_BATCH, WINDOW, KV_B, HD_B), 1.0),
        "cache_v_b": nrm(ks[6], (DEC_BATCH, WINDOW, KV_B, HD_B), 1.0),
        "state_conv": nrm(ks[7], (DEPTH, DEC_BATCH, CONV_W - 1, D_FF), 1.0),
        "w_qkv_a": nrm(ks[8], (N_A_LAYERS, D_MODEL, (H_A + 2 * KV_A) * HD_A), D_MODEL ** -0.5),
        "w_o_a": nrm(ks[9], (N_A_LAYERS, H_A * HD_A, D_MODEL), (H_A * HD_A) ** -0.5),
        "sb_bias": SB_BIAS_INIT + nrm(ks[25], (N_A_LAYERS, H_A), 0.5),
        "g_kv": 1.0 + nrm(ks[10], (D_MODEL,), 0.05),
        "w_kv_b": nrm(ks[11], (D_MODEL, 2 * KV_B * HD_B), D_MODEL ** -0.5),
        "w_q_b": nrm(ks[12], (N_B_LAYERS, D_MODEL, H_B * HD_B), D_MODEL ** -0.5),
        "w_o_b": nrm(ks[13], (N_B_LAYERS, H_B * HD_B, D_MODEL), (H_B * HD_B) ** -0.5),
        "sinks_b": nrm(ks[14], (N_B_LAYERS, H_B), 1.0),
        "rel_bias": nrm(ks[15], (N_BUCKETS, H_B), 0.5),
        "g_pre_mix": 1.0 + nrm(ks[16], (DEPTH, D_MODEL), 0.05),
        "g_post_mix": 1.0 + nrm(ks[17], (DEPTH, D_MODEL), 0.05),
        "g_pre_ffn": 1.0 + nrm(ks[18], (DEPTH, D_MODEL), 0.05),
        "g_post_ffn": 1.0 + nrm(ks[19], (DEPTH, D_MODEL), 0.05),
        "w_gate": nrm(ks[20], (DEPTH, D_MODEL, D_FF), D_MODEL ** -0.5),
        "w_up": nrm(ks[21], (DEPTH, D_MODEL, D_FF), D_MODEL ** -0.5),
        "w_down": nrm(ks[22], (DEPTH, D_FF, D_MODEL), D_FF ** -0.5),
        "conv_w": nrm(ks[23], (DEPTH, CONV_W, D_FF), CONV_W ** -0.5),
        "conv_b": nrm(ks[24], (DEPTH, D_FF), 0.01),
    }


def reference(x_prompt, x_sample, cache_k_a, cache_v_a, page_table, cache_k_b, cache_v_b, state_conv,
              w_qkv_a, w_o_a, sb_bias, g_kv, w_kv_b, w_q_b, w_o_b, sinks_b, rel_bias,
              g_pre_mix, g_post_mix, g_pre_ffn, g_post_ffn, w_gate, w_up, w_down, conv_w, conv_b):
    n_seq, n_pages = page_table.shape
    past_len = n_pages * PAGE_SIZE
    t_s = x_sample.shape[1]
    sb_q_pos = past_len + jnp.arange(t_s)
    sb_k_pos = jnp.arange(past_len + t_s)
    win_q_pos = (past_len + jnp.arange(t_s))[None, :]
    win_k_pos = (past_len - WINDOW + jnp.arange(WINDOW + t_s))[None, :]

    hp, hs = x_prompt, x_sample
    ka_p, va_p, ka_s, va_s = [], [], [], []
    conv_p, conv_s = [], []
    kb_p = vb_p = kb_s = vb_s = None
    for l in range(DEPTH):
        hnp = rmsnorm(hp, g_pre_mix[l])
        hns = rmsnorm(hs, g_pre_mix[l])
        if l < N_A_LAYERS:
            qp, kp, vp = qkv_a(hnp, w_qkv_a[l])
            qs, kss, vss = qkv_a(hns, w_qkv_a[l])
            op = sb_prompt(qp, kp, vp, sb_bias[l])
            past_k = cache_k_a[l][page_table].reshape(n_seq, past_len, KV_A, HD_A)
            past_v = cache_v_a[l][page_table].reshape(n_seq, past_len, KV_A, HD_A)
            os_ = stick_breaking(qs, jnp.concatenate([past_k.astype(kss.dtype), kss], axis=1),
                                 jnp.concatenate([past_v.astype(vss.dtype), vss], axis=1),
                                 sb_q_pos, sb_k_pos, sb_bias[l])
            mp = op @ w_o_a[l]
            ms = os_ @ w_o_a[l]
            ka_p.append(kp); va_p.append(vp); ka_s.append(kss); va_s.append(vss)
        else:
            j = l - N_A_LAYERS
            if l == N_A_LAYERS:
                kb_p, vb_p = shared_kv_b(hp, g_kv, w_kv_b)
                kbs_new, vbs_new = shared_kv_b(hs, g_kv, w_kv_b)
                kb_s = jnp.concatenate([cache_k_b.astype(kbs_new.dtype), kbs_new], axis=1)
                vb_s = jnp.concatenate([cache_v_b.astype(vbs_new.dtype), vbs_new], axis=1)
            qp = (hnp @ w_q_b[j]).reshape(hp.shape[0], hp.shape[1], H_B, HD_B)
            qs = (hns @ w_q_b[j]).reshape(hs.shape[0], t_s, H_B, HD_B)
            op = window_prompt(qp, kb_p, vb_p, sinks_b[j], rel_bias)
            os_ = window_attend(qs[:, None], kb_s[:, None], vb_s[:, None], win_q_pos, win_k_pos,
                                sinks_b[j], rel_bias)[:, 0]
            mp = op @ w_o_b[j]
            ms = os_ @ w_o_b[j]
        hp = hp + rmsnorm(mp, g_post_mix[l])
        hs = hs + rmsnorm(ms, g_post_mix[l])
        zero_state = jnp.zeros((hp.shape[0], CONV_W - 1, D_FF), hp.dtype)
        fp, cp = conv_ffn(rmsnorm(hp, g_pre_ffn[l]), zero_state, w_gate[l], w_up[l], w_down[l], conv_w[l], conv_b[l])
        fs, cs = conv_ffn(rmsnorm(hs, g_pre_ffn[l]), state_conv[l], w_gate[l], w_up[l], w_down[l], conv_w[l], conv_b[l])
        hp = hp + rmsnorm(fp, g_post_ffn[l])
        hs = hs + rmsnorm(fs, g_post_ffn[l])
        conv_p.append(cp); conv_s.append(cs)

    new_k_a_prompt = jnp.stack(ka_p)
    new_v_a_prompt = jnp.stack(va_p)
    new_k_a_sample = jnp.stack(ka_s)
    new_v_a_sample = jnp.stack(va_s)
    new_k_b_prompt = kb_p[:, -WINDOW:]
    new_v_b_prompt = vb_p[:, -WINDOW:]
    new_k_b_sample = kb_s[:, -WINDOW:]
    new_v_b_sample = vb_s[:, -WINDOW:]
    new_conv_prompt = jnp.stack(conv_p)
    new_conv_sample = jnp.stack(conv_s)
    return (hp, hs, new_k_a_prompt, new_v_a_prompt, new_k_a_sample, new_v_a_sample,
            new_k_b_prompt, new_v_b_prompt, new_k_b_sample, new_v_b_sample,
            new_conv_prompt, new_conv_sample)
```

```python
import functools
import math

import numpy as np
import jax
import jax.numpy as jnp
from jax import lax
from jax.experimental import pallas as pl
from jax.experimental.pallas import tpu as pltpu

F32 = jnp.float32
BF16 = jnp.bfloat16

EPS = 1e-6
HD_A = 128
KV_A = 4
HD_B = 64
KV_B = 4
WINDOW = 128
Q_BLOCK = 128
N_BUCKETS = 32
MAX_EXACT = N_BUCKETS // 2
MAX_DISTANCE = WINDOW
CONV_W = 3
NEG = -1e30
SB_KEY_BLOCK = 256
VMEM_LIMIT = 56 * 1024 * 1024


def _cparams(sem):
    return pltpu.CompilerParams(dimension_semantics=sem, vmem_limit_bytes=VMEM_LIMIT)


def _rmsnorm_rows(x, g):
    ms = jnp.mean(x * x, axis=-1, keepdims=True)
    return x * lax.rsqrt(ms + EPS) * g


def _norm_matmul_kernel(h_ref, g_ref, w_ref, o_ref, hn_ref):
    @pl.when(pl.program_id(1) == 0)
    def _():
        hn_ref[...] = _rmsnorm_rows(h_ref[...], g_ref[...]).astype(BF16)

    o_ref[...] = jnp.dot(hn_ref[...], w_ref[...].astype(BF16), preferred_element_type=F32)


def norm_matmul(h, g, w, tm, tn):
    m, d = h.shape
    n = w.shape[1]
    return pl.pallas_call(
        _norm_matmul_kernel,
        grid=(m // tm, n // tn),
        in_specs=[pl.BlockSpec((tm, d), lambda i, j: (i, 0)),
                  pl.BlockSpec((1, d), lambda i, j: (0, 0)),
                  pl.BlockSpec((d, tn), lambda i, j: (0, j))],
        out_specs=pl.BlockSpec((tm, tn), lambda i, j: (i, j)),
        out_shape=jax.ShapeDtypeStruct((m, n), F32),
        scratch_shapes=[pltpu.VMEM((tm, d), BF16)],
        compiler_params=_cparams(("parallel", "arbitrary")),
        name="norm_matmul",
    )(h, g.reshape(1, d), w)


def _proj_res_kernel(o_ref, w_ref, g_ref, h_ref, out_ref, wb_ref):
    @pl.when(pl.program_id(0) == 0)
    def _():
        wb_ref[...] = w_ref[...].astype(BF16)

    y = jnp.dot(o_ref[...].astype(BF16), wb_ref[...], preferred_element_type=F32)
    out_ref[...] = h_ref[...] + _rmsnorm_rows(y, g_ref[...])


def proj_res(o, w, g, h, tm):
    m, k = o.shape
    d = w.shape[1]
    return pl.pallas_call(
        _proj_res_kernel,
        grid=(m // tm,),
        in_specs=[pl.BlockSpec((tm, k), lambda i: (i, 0)),
                  pl.BlockSpec((k, d), lambda i: (0, 0), pipeline_mode=pl.Buffered(1)),
                  pl.BlockSpec((1, d), lambda i: (0, 0)),
                  pl.BlockSpec((tm, d), lambda i: (i, 0))],
        out_specs=pl.BlockSpec((tm, d), lambda i: (i, 0)),
        out_shape=jax.ShapeDtypeStruct((m, d), F32),
        scratch_shapes=[pltpu.VMEM((k, d), BF16)],
        compiler_params=_cparams(("arbitrary",)),
        name="proj_res",
    )(o, w, g.reshape(1, d), h)


def _gelu_tanh(x):
    c = math.sqrt(2.0 / math.pi)
    return 0.5 * x * (1.0 + jnp.tanh(c * (x + 0.044715 * (x * x * x))))


def _conv_ffn_kernel(*refs, tm, seq_len, has_state, tail):
    if has_state:
        (h_ref, gpre_ref, gpost_ref, wg_ref, wu_ref, wd_ref, cw_ref, cb_ref, s1_ref, s2_ref,
         out_ref, gt_ref, hn_ref, gp_ref, carry_ref) = refs
    else:
        (h_ref, gpre_ref, gpost_ref, wg_ref, wu_ref, wd_ref, cw_ref, cb_ref,
         out_ref, gt_ref, hn_ref, gp_ref, carry_ref) = refs
    i = pl.program_id(0)
    j = pl.program_id(1)

    @pl.when(j == 0)
    def _():
        hn_ref[...] = _rmsnorm_rows(h_ref[...], gpre_ref[...]).astype(BF16)
        out_ref[...] = jnp.zeros_like(out_ref)

    @pl.when(i == 0)
    def _():
        carry_ref[j] = jnp.zeros(carry_ref.shape[1:], F32)

    hn = hn_ref[...]
    g = jnp.dot(hn, wg_ref[...].astype(BF16), preferred_element_type=F32)
    u = jnp.dot(hn, wu_ref[...].astype(BF16), preferred_element_type=F32)
    gt_ref[...] = g[tm - tail:, :]

    gp_ref[0:8, :] = carry_ref[j]
    gp_ref[8:8 + tm, :] = g
    carry_ref[j] = g[tm - 8:, :]
    g1 = gp_ref[7:7 + tm, :]
    g2 = gp_ref[6:6 + tm, :]
    pos = (i * tm + lax.broadcasted_iota(jnp.int32, (tm, 1), 0)) % seq_len
    if has_state:
        g1 = jnp.where(pos >= 1, g1, s1_ref[...])
        g2 = jnp.where(pos >= 2, g2, s2_ref[...])
    else:
        g1 = jnp.where(pos >= 1, g1, 0.0)
        g2 = jnp.where(pos >= 2, g2, 0.0)
    cw = cw_ref[...]
    gc = cb_ref[...] + g2 * cw[0:1, :] + g1 * cw[1:2, :] + g * cw[2:3, :]
    y = (_gelu_tanh(gc) * u).astype(BF16)
    out_ref[...] += jnp.dot(y, wd_ref[...].astype(BF16), preferred_element_type=F32)

    @pl.when(j == pl.num_programs(1) - 1)
    def _():
        out_ref[...] = h_ref[...] + _rmsnorm_rows(out_ref[...], gpost_ref[...])


def conv_ffn(h, g_pre, g_post, wg, wu, wd, cw, cb, *, tm, tf, seq_len, state_rows=None):
    m, d = h.shape
    f = wg.shape[1]
    has_state = state_rows is not None
    tail = tm if has_state else 8
    ni, nj = m // tm, f // tf
    in_specs = [pl.BlockSpec((tm, d), lambda i, j: (i, 0), pipeline_mode=pl.Buffered(1)),
                pl.BlockSpec((1, d), lambda i, j: (0, 0)),
                pl.BlockSpec((1, d), lambda i, j: (0, 0)),
                pl.BlockSpec((d, tf), lambda i, j: (0, j)),
                pl.BlockSpec((d, tf), lambda i, j: (0, j)),
                pl.BlockSpec((tf, d), lambda i, j: (j, 0)),
                pl.BlockSpec((CONV_W, tf), lambda i, j: (0, j)),
                pl.BlockSpec((1, tf), lambda i, j: (0, j))]
    args = [h, g_pre.reshape(1, d), g_post.reshape(1, d), wg, wu, wd, cw, cb.reshape(1, f)]
    if has_state:
        in_specs += [pl.BlockSpec((tm, tf), lambda i, j: (i, j))] * 2
        args += list(state_rows)
    return pl.pallas_call(
        functools.partial(_conv_ffn_kernel, tm=tm, seq_len=seq_len, has_state=has_state, tail=tail),
        grid=(ni, nj),
        in_specs=in_specs,
        out_specs=[pl.BlockSpec((tm, d), lambda i, j: (i, 0)),
                   pl.BlockSpec((None, tail, tf), lambda i, j: (i, 0, j))],
        out_shape=[jax.ShapeDtypeStruct((m, d), F32),
                   jax.ShapeDtypeStruct((ni, tail, f), F32)],
        scratch_shapes=[pltpu.VMEM((tm, d), BF16),
                        pltpu.VMEM((tm + 8, tf), F32),
                        pltpu.VMEM((nj, 8, tf), F32)],
        compiler_params=_cparams(("arbitrary", "arbitrary")),
        name="conv_ffn",
    )(*args)


def _sb_block(z, valid, carry, u2, n_keys):
    sp = jnp.log1p(jnp.exp(-jnp.abs(z)))
    log_stay = -(jnp.maximum(z, 0.0) + sp)
    log_beta = jnp.minimum(z, 0.0) - sp
    if valid is not None:
        log_stay = jnp.where(valid, log_stay, 0.0)
    hi = log_stay.astype(BF16)
    lo = (log_stay - hi.astype(F32)).astype(BF16)
    later = jnp.dot(jnp.concatenate([hi, lo], axis=1), u2, preferred_element_type=F32) + carry
    a = jnp.exp(log_beta + later)
    if valid is not None:
        a = jnp.where(valid, a, 0.0)
    return a, carry + jnp.sum(log_stay, axis=1, keepdims=True)


def _suffix_matrix(n):
    u = (np.arange(n)[:, None] > np.arange(n)[None, :]).astype(np.float32)
    return jnp.asarray(np.concatenate([u, u], axis=0), dtype=BF16)


def _sb_prompt_kernel(bias_ref, q_ref, k_ref, v_ref, u2_ref, o_ref, kb_ref, vb_ref, qs_ref, acc_ref, *, grp):
    kh = pl.program_id(1)
    qi = pl.program_id(2)
    tq = q_ref.shape[0]
    tk = SB_KEY_BLOCK
    rows = grp * tq

    @pl.when(qi == 0)
    def _():
        kb_ref[...] = k_ref[...].astype(BF16)
        vb_ref[...] = v_ref[...].astype(BF16)

    scale = HD_A ** -0.5
    for g in range(grp):
        qs_ref[g * tq:(g + 1) * tq, :] = (q_ref[:, g * HD_A:(g + 1) * HD_A] * scale).astype(BF16)
    qs = qs_ref[...]
    u2 = u2_ref[...]
    row = lax.broadcasted_iota(jnp.int32, (rows, 1), 0)
    bias = jnp.zeros((rows, 1), F32)
    for g in range(grp):
        bias = jnp.where(row // tq == g, bias_ref[kh * grp + g], bias)
    acc_ref[...] = jnp.zeros_like(acc_ref)

    def step(kb, carry, masked):
        start = pl.multiple_of(kb * tk, tk)
        kblk = kb_ref[pl.ds(start, tk), :]
        vblk = vb_ref[pl.ds(start, tk), :]
        z = lax.dot_general(qs, kblk, (((1,), (1,)), ((), ())), preferred_element_type=F32) + bias
        valid = None
        if masked:
            q_pos = qi * tq + row % tq
            k_pos = kb * tk + lax.broadcasted_iota(jnp.int32, (rows, tk), 1)
            valid = k_pos < q_pos
        a, carry = _sb_block(z, valid, carry, u2, tk)
        acc_ref[...] += jnp.dot(a.astype(BF16), vblk, preferred_element_type=F32)
        return carry

    kb_diag = (qi * tq) // tk
    carry = step(kb_diag, jnp.zeros((rows, 1), F32), True)
    lax.fori_loop(0, kb_diag, lambda n, c: step(kb_diag - 1 - n, c, False), carry)
    for g in range(grp):
        o_ref[:, g * HD_A:(g + 1) * HD_A] = acc_ref[g * tq:(g + 1) * tq, :].astype(o_ref.dtype)


def sb_prompt(qkv, sb_bias, batch, seq, n_heads):
    grp = n_heads // KV_A
    nq = seq // Q_BLOCK
    kcol = n_heads
    vcol = n_heads + KV_A
    return pl.pallas_call(
        functools.partial(_sb_prompt_kernel, grp=grp),
        grid=(batch, KV_A, nq),
        in_specs=[pl.BlockSpec(memory_space=pltpu.SMEM),
                  pl.BlockSpec((Q_BLOCK, grp * HD_A), lambda b, k, q: (b * nq + q, k)),
                  pl.BlockSpec((seq, HD_A), lambda b, k, q: (b, kcol + k)),
                  pl.BlockSpec((seq, HD_A), lambda b, k, q: (b, vcol + k)),
                  pl.BlockSpec((2 * SB_KEY_BLOCK, SB_KEY_BLOCK), lambda b, k, q: (0, 0))],
        out_specs=pl.BlockSpec((Q_BLOCK, grp * HD_A), lambda b, k, q: (b * nq + q, k)),
        out_shape=jax.ShapeDtypeStruct((batch * seq, n_heads * HD_A), BF16),
        scratch_shapes=[pltpu.VMEM((seq, HD_A), BF16),
                        pltpu.VMEM((seq, HD_A), BF16),
                        pltpu.VMEM((grp * Q_BLOCK, HD_A), BF16),
                        pltpu.VMEM((grp * Q_BLOCK, HD_A), F32)],
        compiler_params=_cparams(("arbitrary", "arbitrary", "arbitrary")),
        name="sb_prompt",
    )(sb_bias, qkv, qkv, qkv, _suffix_matrix(SB_KEY_BLOCK))


def _sb_decode_kernel(pt_ref, bias_ref, q_ref, kn_ref, vn_ref, *rest, pages, grp, t_s):
    k_refs = rest[:pages]
    v_refs = rest[pages:2 * pages]
    u2_ref, o_ref, qs_ref, acc_ref, carry_ref = rest[2 * pages:]
    c = pl.program_id(1)
    rk = grp * t_s
    rows = KV_A * rk
    page = k_refs[0].shape[0]
    u2 = u2_ref[...]
    row = lax.broadcasted_iota(jnp.int32, (rows, 1), 0)
    bias = jnp.zeros((rows, 1), F32)
    for hh in range(KV_A * grp):
        bias = jnp.where(row // t_s == hh, bias_ref[hh], bias)

    def process(k_ref, v_ref, masked):
        zs = []
        for k in range(KV_A):
            kk = k_ref[:, k * HD_A:(k + 1) * HD_A].astype(BF16)
            zs.append(lax.dot_general(qs_ref[k], kk, (((1,), (1,)), ((), ())), preferred_element_type=F32))
        z = jnp.concatenate(zs, axis=0) + bias
        valid = None
        if masked:
            valid = lax.broadcasted_iota(jnp.int32, (rows, page), 1) < row % t_s
        a, carry = _sb_block(z, valid, carry_ref[...], u2, page)
        carry_ref[...] = carry
        a = a.astype(BF16)
        for k in range(KV_A):
            vv = v_ref[:, k * HD_A:(k + 1) * HD_A].astype(BF16)
            acc_ref[k] += jnp.dot(a[k * rk:(k + 1) * rk, :], vv, preferred_element_type=F32)

    @pl.when(c == 0)
    def _():
        qs_ref[...] = (q_ref[...] * (HD_A ** -0.5)).astype(BF16)
        acc_ref[...] = jnp.zeros_like(acc_ref)
        carry_ref[...] = jnp.zeros_like(carry_ref)
        process(kn_ref, vn_ref, True)

    for j in range(pages):
        process(k_refs[j], v_refs[j], False)

    @pl.when(c == pl.num_programs(1) - 1)
    def _():
        o_ref[...] = acc_ref[...]


def sb_decode(q, k_new, v_new, cache_k, cache_v, layer, page_table, sb_bias, *, pages):
    n_seq, _, rk, _ = q.shape
    n_pages = page_table.shape[1]
    page = cache_k.shape[2]
    grp = sb_bias.shape[0] // KV_A
    t_s = rk // grp
    n_chunks = n_pages // pages

    def page_spec(j):
        return pl.BlockSpec((None, None, page, KV_A * HD_A),
                            lambda b, c, pt: (layer, pt[b, n_pages - 1 - (c * pages + j)], 0, 0))

    blk4 = pl.BlockSpec((None, KV_A, rk, HD_A), lambda b, c, pt: (b, 0, 0, 0))
    new_spec = pl.BlockSpec((None, page, KV_A * HD_A), lambda b, c, pt: (b, 0, 0))
    grid_spec = pltpu.PrefetchScalarGridSpec(
        num_scalar_prefetch=1,
        grid=(n_seq, n_chunks),
        in_specs=([pl.BlockSpec(memory_space=pltpu.SMEM), blk4, new_spec, new_spec]
                  + [page_spec(j) for j in range(pages)] * 2
                  + [pl.BlockSpec((2 * page, page), lambda b, c, pt: (0, 0))]),
        out_specs=blk4,
        scratch_shapes=[pltpu.VMEM((KV_A, rk, HD_A), BF16),
                        pltpu.VMEM((KV_A, rk, HD_A), F32),
                        pltpu.VMEM((KV_A * rk, 1), F32)],
    )
    return pl.pallas_call(
        functools.partial(_sb_decode_kernel, pages=pages, grp=grp, t_s=t_s),
        grid_spec=grid_spec,
        out_shape=jax.ShapeDtypeStruct(q.shape, F32),
        compiler_params=_cparams(("arbitrary", "arbitrary")),
        name="sb_decode",
    )(page_table, sb_bias, q, k_new, v_new, *([cache_k] * pages), *([cache_v] * pages), _suffix_matrix(page))


def _t5_bucket_table(dist):
    d = np.maximum(dist, 0)
    large = MAX_EXACT + (np.log(np.maximum(d, 1).astype(np.float32) / np.float32(MAX_EXACT))
                         / np.float32(math.log(MAX_DISTANCE / MAX_EXACT))
                         * np.float32(N_BUCKETS - MAX_EXACT)).astype(np.int32)
    large = np.minimum(large, N_BUCKETS - 1)
    return np.where(d < MAX_EXACT, d, large).astype(np.int32)


def _build_bias_table(tbl_ref, bucket, valid, rb_ref, n_tables, head_of):
    def body(n, _):
        h = head_of(n)
        acc = jnp.zeros(bucket.shape, F32)
        for bb in range(N_BUCKETS):
            acc = jnp.where(bucket == bb, rb_ref[bb, h], acc)
        tbl_ref[n] = jnp.where(valid, acc, NEG)
        return 0
    lax.fori_loop(0, n_tables, body, 0)


def _win_softmax_pv(s, sink, v):
    m = jnp.maximum(jnp.max(s, axis=-1, keepdims=True), sink)
    p = jnp.exp(s - m)
    denom = jnp.sum(p, axis=-1, keepdims=True) + jnp.exp(sink - m)
    o = jnp.dot(p.astype(BF16), v, preferred_element_type=F32)
    return o / denom


def _win_prompt_kernel(rb_ref, sink_ref, q_ref, kvp_ref, kvc_ref, bkt_ref, o_ref, tbl_ref, *, n_heads):
    b = pl.program_id(0)
    qi = pl.program_id(1)
    tq = q_ref.shape[0]
    grp = n_heads // KV_B
    kvw = KV_B * HD_B
    t_idx = lax.broadcasted_iota(jnp.int32, (tq, 2 * WINDOW), 0)
    s_idx = lax.broadcasted_iota(jnp.int32, (tq, 2 * WINDOW), 1)

    @pl.when((b == 0) & (qi == 0))
    def _():
        dist = t_idx + WINDOW - s_idx
        valid = (dist >= 0) & (dist < WINDOW)
        _build_bias_table(tbl_ref, bkt_ref[...], valid, rb_ref, n_heads, lambda n: n)

    first = jnp.where((qi == 0) & (s_idx < WINDOW), NEG, 0.0)
    kv = jnp.concatenate([kvp_ref[...], kvc_ref[...]], axis=0).astype(BF16)
    scale = HD_B ** -0.5
    for k in range(KV_B):
        kk = kv[:, k * HD_B:(k + 1) * HD_B]
        vv = kv[:, kvw + k * HD_B:kvw + (k + 1) * HD_B]
        for g in range(grp):
            h = k * grp + g
            qh = q_ref[:, h * HD_B:(h + 1) * HD_B].astype(BF16)
            s = lax.dot_general(qh, kk, (((1,), (1,)), ((), ())), preferred_element_type=F32) * scale
            s = s + tbl_ref[h] + first
            o_ref[:, h * HD_B:(h + 1) * HD_B] = _win_softmax_pv(s, sink_ref[h], vv).astype(o_ref.dtype)


def win_prompt(q, kv, sinks, rel_bias, batch, seq):
    n_heads = q.shape[1] // HD_B
    nq = seq // WINDOW
    t = np.arange(WINDOW)[:, None]
    s = np.arange(2 * WINDOW)[None, :]
    bucket = jnp.asarray(_t5_bucket_table(t + WINDOW - s))
    return pl.pallas_call(
        functools.partial(_win_prompt_kernel, n_heads=n_heads),
        grid=(batch, nq),
        in_specs=[pl.BlockSpec(memory_space=pltpu.SMEM),
                  pl.BlockSpec(memory_space=pltpu.SMEM),
                  pl.BlockSpec((WINDOW, q.shape[1]), lambda b, i: (b * nq + i, 0)),
                  pl.BlockSpec((WINDOW, kv.shape[1]), lambda b, i: (jnp.maximum(b * nq + i - 1, 0), 0)),
                  pl.BlockSpec((WINDOW, kv.shape[1]), lambda b, i: (b * nq + i, 0)),
                  pl.BlockSpec((WINDOW, 2 * WINDOW), lambda b, i: (0, 0))],
        out_specs=pl.BlockSpec((WINDOW, q.shape[1]), lambda b, i: (b * nq + i, 0)),
        out_shape=jax.ShapeDtypeStruct(q.shape, BF16),
        scratch_shapes=[pltpu.VMEM((n_heads, WINDOW, 2 * WINDOW), F32)],
        compiler_params=_cparams(("arbitrary", "arbitrary")),
        name="win_prompt",
    )(rel_bias, sinks, q, kv, kv, bucket)


def _win_decode_kernel(rb_ref, sink_ref, q_ref, k_ref, v_ref, bkt_ref, o_ref, tbl_ref, *, grp, t_s):
    b = pl.program_id(0)
    rk, nk = bkt_ref.shape
    t_idx = lax.broadcasted_iota(jnp.int32, (rk, nk), 0) % t_s
    j_idx = lax.broadcasted_iota(jnp.int32, (rk, nk), 1)

    @pl.when(b == 0)
    def _():
        dist = t_idx + WINDOW - j_idx
        valid = (dist >= 0) & (dist < WINDOW) & (j_idx < WINDOW + t_s)
        bucket = bkt_ref[...]
        row_g = lax.broadcasted_iota(jnp.int32, (rk, nk), 0) // t_s
        for k in range(KV_B):
            def body(g, tbl):
                acc = jnp.zeros((rk, nk), F32)
                for bb in range(N_BUCKETS):
                    acc = jnp.where(bucket == bb, rb_ref[bb, k * grp + g], acc)
                return jnp.where(row_g == g, acc, tbl)
            tbl = lax.fori_loop(0, grp, body, jnp.zeros((rk, nk), F32))
            tbl_ref[k] = jnp.where(valid, tbl, NEG)

    row = lax.broadcasted_iota(jnp.int32, (rk, 1), 0)
    scale = HD_B ** -0.5
    for k in range(KV_B):
        sink = jnp.zeros((rk, 1), F32)
        for g in range(grp):
            sink = jnp.where(row // t_s == g, sink_ref[k * grp + g], sink)
        s = lax.dot_general(q_ref[k].astype(BF16), k_ref[k].astype(BF16), (((1,), (1,)), ((), ())),
                            preferred_element_type=F32) * scale + tbl_ref[k]
        o_ref[k] = _win_softmax_pv(s, sink, v_ref[k].astype(BF16))


def win_decode(q, k, v, sinks, rel_bias, t_s):
    n_seq, _, rk, _ = q.shape
    nk = k.shape[2]
    grp = rk // t_s
    t = (np.arange(rk) % t_s)[:, None]
    j = np.arange(nk)[None, :]
    bucket = jnp.asarray(_t5_bucket_table(t + WINDOW - j))
    blk_q = pl.BlockSpec((None, KV_B, rk, HD_B), lambda b: (b, 0, 0, 0))
    blk_k = pl.BlockSpec((None, KV_B, nk, HD_B), lambda b: (b, 0, 0, 0))
    return pl.pallas_call(
        functools.partial(_win_decode_kernel, grp=grp, t_s=t_s),
        grid=(n_seq,),
        in_specs=[pl.BlockSpec(memory_space=pltpu.SMEM),
                  pl.BlockSpec(memory_space=pltpu.SMEM),
                  blk_q, blk_k, blk_k,
                  pl.BlockSpec((rk, nk), lambda b: (0, 0))],
        out_specs=blk_q,
        out_shape=jax.ShapeDtypeStruct(q.shape, F32),
        scratch_shapes=[pltpu.VMEM((KV_B, rk, nk), F32)],
        compiler_params=_cparams(("arbitrary",)),
        name="win_decode",
    )(rel_bias, sinks, q, k, v, bucket)


def _heads_to_rows(x, n_seq, t_s, n_kv, hd):
    grp = x.shape[1] // (n_kv * hd)
    x = x.reshape(n_seq, t_s, n_kv, grp, hd).transpose(0, 2, 3, 1, 4)
    return x.reshape(n_seq, n_kv, grp * t_s, hd)


def _rows_to_heads(x, t_s):
    n_seq, n_kv, rk, hd = x.shape
    grp = rk // t_s
    x = x.reshape(n_seq, n_kv, grp, t_s, hd).transpose(0, 3, 1, 2, 4)
    return x.reshape(n_seq * t_s, n_kv * grp * hd)


def kernel(x_prompt, x_sample, cache_k_a, cache_v_a, page_table, cache_k_b, cache_v_b, state_conv,
           w_qkv_a, w_o_a, sb_bias, g_kv, w_kv_b, w_q_b, w_o_b, sinks_b, rel_bias,
           g_pre_mix, g_post_mix, g_pre_ffn, g_post_ffn, w_gate, w_up, w_down, conv_w, conv_b):
    batch, seq, d = x_prompt.shape
    n_seq, t_s, _ = x_sample.shape
    n_a = w_qkv_a.shape[0]
    depth = w_gate.shape[0]
    d_ff = w_gate.shape[2]
    page = cache_k_a.shape[2]
    h_a = w_o_a.shape[1] // HD_A
    mp, ms = batch * seq, n_seq * t_s
    qw = h_a * HD_A
    kvw_a = KV_A * HD_A
    kvw_b = KV_B * HD_B

    cache_k4 = cache_k_a.reshape(cache_k_a.shape[0], cache_k_a.shape[1], page, kvw_a)
    cache_v4 = cache_v_a.reshape(cache_v_a.shape[0], cache_v_a.shape[1], page, kvw_a)

    hp = x_prompt.reshape(mp, d)
    hs = x_sample.reshape(ms, d)
    ka_p, va_p, ka_s, va_s, conv_p, conv_s = [], [], [], [], [], []
    kv_p = kb_s = vb_s = kq_s = vq_s = None
    tm_p = 1024
    for l in range(depth):
        if l < n_a:
            qkv_p = norm_matmul(hp, g_pre_mix[l], w_qkv_a[l], tm_p, 512)
            qkv_s = norm_matmul(hs, g_pre_mix[l], w_qkv_a[l], ms, 512)
            op = sb_prompt(qkv_p, sb_bias[l], batch, seq, h_a)
            q_s = _heads_to_rows(qkv_s[:, :qw], n_seq, t_s, KV_A, HD_A)
            k_new = qkv_s[:, qw:qw + kvw_a].reshape(n_seq, t_s, kvw_a)
            v_new = qkv_s[:, qw + kvw_a:].reshape(n_seq, t_s, kvw_a)
            pad = ((0, 0), (0, page - t_s), (0, 0))
            o_s = sb_decode(q_s, jnp.pad(k_new, pad), jnp.pad(v_new, pad), cache_k4, cache_v4, l,
                            page_table, sb_bias[l], pages=8)
            o_s = _rows_to_heads(o_s, t_s)
            hp = proj_res(op, w_o_a[l], g_post_mix[l], hp, 256)
            hs = proj_res(o_s, w_o_a[l], g_post_mix[l], hs, ms)
            ka_p.append(qkv_p[:, qw:qw + kvw_a].reshape(batch, seq, KV_A, HD_A))
            va_p.append(qkv_p[:, qw + kvw_a:].reshape(batch, seq, KV_A, HD_A))
            ka_s.append(k_new.reshape(n_seq, t_s, KV_A, HD_A))
            va_s.append(v_new.reshape(n_seq, t_s, KV_A, HD_A))
        else:
            j = l - n_a
            if j == 0:
                kv_p = norm_matmul(hp, g_kv, w_kv_b, tm_p, 2 * kvw_b)
                kv_s = norm_matmul(hs, g_kv, w_kv_b, ms, 2 * kvw_b)
                kb_s = jnp.concatenate([cache_k_b, kv_s[:, :kvw_b].reshape(n_seq, t_s, KV_B, HD_B)], axis=1)
                vb_s = jnp.concatenate([cache_v_b, kv_s[:, kvw_b:].reshape(n_seq, t_s, KV_B, HD_B)], axis=1)
                padk = ((0, 0), (0, 0), (0, 2 * WINDOW - (WINDOW + t_s)), (0, 0))
                kq_s = jnp.pad(kb_s.transpose(0, 2, 1, 3), padk)
                vq_s = jnp.pad(vb_s.transpose(0, 2, 1, 3), padk)
            q_p = norm_matmul(hp, g_pre_mix[l], w_q_b[j], tm_p, 512)
            q_s = norm_matmul(hs, g_pre_mix[l], w_q_b[j], ms, 512)
            op = win_prompt(q_p, kv_p, sinks_b[j], rel_bias, batch, seq)
            o_s = win_decode(_heads_to_rows(q_s, n_seq, t_s, KV_B, HD_B), kq_s, vq_s, sinks_b[j], rel_bias, t_s)
            o_s = _rows_to_heads(o_s, t_s)
            hp = proj_res(op, w_o_b[j], g_post_mix[l], hp, 256)
            hs = proj_res(o_s, w_o_b[j], g_post_mix[l], hs, ms)
        hp, gt_p = conv_ffn(hp, g_pre_ffn[l], g_post_ffn[l], w_gate[l], w_up[l], w_down[l], conv_w[l], conv_b[l],
                            tm=tm_p, tf=256, seq_len=seq)
        st = state_conv[l]
        zeros = jnp.zeros((n_seq, 1, d_ff), F32)
        s1 = jnp.concatenate([st[:, 1:2], zeros, zeros, zeros][:t_s], axis=1).reshape(ms, d_ff)
        s2 = jnp.concatenate([st[:, 0:1], st[:, 1:2], zeros, zeros][:t_s], axis=1).reshape(ms, d_ff)
        hs, gt_s = conv_ffn(hs, g_pre_ffn[l], g_post_ffn[l], w_gate[l], w_up[l], w_down[l], conv_w[l], conv_b[l],
                            tm=ms, tf=512, seq_len=t_s, state_rows=(s1, s2))
        tiles_per_seq = seq // tm_p
        conv_p.append(gt_p.reshape(batch, tiles_per_seq, 8, d_ff)[:, -1, 8 - (CONV_W - 1):, :])
        conv_s.append(gt_s.reshape(n_seq, t_s, d_ff)[:, t_s - (CONV_W - 1):, :])

    kvp4 = kv_p.reshape(batch, seq, 2, KV_B, HD_B)
    return (hp.reshape(batch, seq, d), hs.reshape(n_seq, t_s, d),
            jnp.stack(ka_p), jnp.stack(va_p), jnp.stack(ka_s), jnp.stack(va_s),
            kvp4[:, seq - WINDOW:, 0], kvp4[:, seq - WINDOW:, 1],
            kb_s[:, -WINDOW:], vb_s[:, -WINDOW:],
            jnp.stack(conv_p), jnp.stack(conv_s))
```

```python
import functools
import math

import numpy as np
import jax
import jax.numpy as jnp
from jax import lax
from jax.experimental import pallas as pl
from jax.experimental.pallas import tpu as pltpu

F32 = jnp.float32
BF16 = jnp.bfloat16

EPS = 1e-6
HD_A = 128
KV_A = 4
HD_B = 64
KV_B = 4
WINDOW = 128
Q_BLOCK = 128
N_BUCKETS = 32
MAX_EXACT = N_BUCKETS // 2
MAX_DISTANCE = WINDOW
CONV_W = 3
NEG = -1e30
LOG2E = math.log2(math.e)
SB_KEY_BLOCK = 256
SB_HEADS_PER_STEP = 2
SB_PAGES_PER_STEP = 8
VMEM_LIMIT = 56 * 1024 * 1024
TM_PROMPT = 1024
TF_PROMPT = 256
FFN_ROW_GROUPS = 4
TF_SAMPLE = 512
TN_PROJ = 512
TM_PROJ_RES = 256


def _cparams(sem):
    return pltpu.CompilerParams(dimension_semantics=sem, vmem_limit_bytes=VMEM_LIMIT)


def _rmsnorm_rows(x, g):
    ms = jnp.mean(x * x, axis=-1, keepdims=True)
    return x * lax.rsqrt(ms + EPS) * g


def _norm_matmul_kernel(h_ref, g_ref, w_ref, o_ref, hn_ref):
    @pl.when(pl.program_id(1) == 0)
    def _():
        hn_ref[...] = _rmsnorm_rows(h_ref[...], g_ref[...]).astype(BF16)

    o_ref[...] = jnp.dot(hn_ref[...], w_ref[...].astype(BF16), preferred_element_type=F32)


def norm_matmul(h, g, w, layer, tm, tn):
    m, d = h.shape
    n = w.shape[2]
    return pl.pallas_call(
        _norm_matmul_kernel,
        grid=(m // tm, n // tn),
        in_specs=[pl.BlockSpec((tm, d), lambda i, j: (i, 0)),
                  pl.BlockSpec((1, d), lambda i, j: (0, 0)),
                  pl.BlockSpec((None, d, tn), lambda i, j: (layer, 0, j))],
        out_specs=pl.BlockSpec((tm, tn), lambda i, j: (i, j)),
        out_shape=jax.ShapeDtypeStruct((m, n), F32),
        scratch_shapes=[pltpu.VMEM((tm, d), BF16)],
        compiler_params=_cparams(("arbitrary", "arbitrary")),
        name="norm_matmul",
    )(h, g.reshape(1, d), w)


def _proj_res_kernel(o_ref, w_ref, g_ref, h_ref, out_ref, wb_ref):
    @pl.when(pl.program_id(0) == 0)
    def _():
        wb_ref[...] = w_ref[...].astype(BF16)

    y = jnp.dot(o_ref[...].astype(BF16), wb_ref[...], preferred_element_type=F32)
    out_ref[...] = h_ref[...] + _rmsnorm_rows(y, g_ref[...])


def proj_res(o, w, layer, g, h, tm):
    m, k = o.shape
    d = w.shape[2]
    return pl.pallas_call(
        _proj_res_kernel,
        grid=(m // tm,),
        in_specs=[pl.BlockSpec((tm, k), lambda i: (i, 0)),
                  pl.BlockSpec((None, k, d), lambda i: (layer, 0, 0), pipeline_mode=pl.Buffered(1)),
                  pl.BlockSpec((1, d), lambda i: (0, 0)),
                  pl.BlockSpec((tm, d), lambda i: (i, 0))],
        out_specs=pl.BlockSpec((tm, d), lambda i: (i, 0)),
        out_shape=jax.ShapeDtypeStruct((m, d), F32),
        scratch_shapes=[pltpu.VMEM((k, d), BF16)],
        compiler_params=_cparams(("arbitrary",)),
        name="proj_res",
    )(o, w, g.reshape(1, d), h)


def _gelu_tanh(x):
    c = math.sqrt(2.0 / math.pi)
    return 0.5 * x * (1.0 + jnp.tanh(c * (x + 0.044715 * (x * x * x))))


def _conv_ffn_kernel(*refs, tm, seq_len, has_state, tail, n_split):
    if has_state:
        (h_ref, gpre_ref, gpost_ref, wg_ref, wu_ref, wd_ref, cw_ref, cb_ref, s1_ref, s2_ref,
         out_ref, gt_ref, hn_ref, gp_ref, carry_ref) = refs
    else:
        (h_ref, gpre_ref, gpost_ref, wg_ref, wu_ref, wd_ref, cw_ref, cb_ref,
         out_ref, gt_ref, hn_ref, gp_ref, carry_ref) = refs
    i = pl.program_id(0)
    j = pl.program_id(1)

    @pl.when(j == 0)
    def _():
        hn_ref[...] = _rmsnorm_rows(h_ref[...], gpre_ref[...]).astype(BF16)
        out_ref[...] = jnp.zeros_like(out_ref)

    @pl.when(i == 0)
    def _():
        carry_ref[j] = jnp.zeros(carry_ref.shape[1:], F32)

    wg = wg_ref[...].astype(BF16)
    wu = wu_ref[...].astype(BF16)
    wd = wd_ref[...].astype(BF16)
    cw = cw_ref[...]
    cb = cb_ref[...]
    gp_ref[0:8, :] = carry_ref[j]
    th = tm // n_split
    gs, us = [], []
    for r in range(n_split):
        r0 = r * th
        hn = hn_ref[r0:r0 + th, :]
        g = jnp.dot(hn, wg, preferred_element_type=F32)
        us.append(jnp.dot(hn, wu, preferred_element_type=F32))
        gs.append(g)
        gp_ref[8 + r0:8 + r0 + th, :] = g
    carry_ref[j] = gs[-1][th - 8:, :]
    if tail == tm:
        for r in range(n_split):
            gt_ref[r * th:(r + 1) * th, :] = gs[r]
    else:
        gt_ref[...] = gs[-1][th - tail:, :]
    for r in range(n_split):
        r0 = r * th
        g, u = gs[r], us[r]
        g1 = gp_ref[7 + r0:7 + r0 + th, :]
        g2 = gp_ref[6 + r0:6 + r0 + th, :]
        pos = (i * tm + r0 + lax.broadcasted_iota(jnp.int32, (th, 1), 0)) % seq_len
        if has_state:
            g1 = jnp.where(pos >= 1, g1, s1_ref[r0:r0 + th, :])
            g2 = jnp.where(pos >= 2, g2, s2_ref[r0:r0 + th, :])
        else:
            g1 = jnp.where(pos >= 1, g1, 0.0)
            g2 = jnp.where(pos >= 2, g2, 0.0)
        gc = cb + g2 * cw[0:1, :] + g1 * cw[1:2, :] + g * cw[2:3, :]
        y = (_gelu_tanh(gc) * u).astype(BF16)
        out_ref[r0:r0 + th, :] += jnp.dot(y, wd, preferred_element_type=F32)

    @pl.when(j == pl.num_programs(1) - 1)
    def _():
        out_ref[...] = h_ref[...] + _rmsnorm_rows(out_ref[...], gpost_ref[...])


def conv_ffn(h, g_pre, g_post, wg, wu, wd, cw, cb, layer, *, tm, tf, seq_len, n_split, state_rows=None):
    m, d = h.shape
    f = wg.shape[2]
    has_state = state_rows is not None
    tail = tm if has_state else 8
    ni, nj = m // tm, f // tf
    in_specs = [pl.BlockSpec((tm, d), lambda i, j: (i, 0), pipeline_mode=pl.Buffered(1)),
                pl.BlockSpec((None, 1, d), lambda i, j: (layer, 0, 0)),
                pl.BlockSpec((None, 1, d), lambda i, j: (layer, 0, 0)),
                pl.BlockSpec((None, d, tf), lambda i, j: (layer, 0, j)),
                pl.BlockSpec((None, d, tf), lambda i, j: (layer, 0, j)),
                pl.BlockSpec((None, tf, d), lambda i, j: (layer, j, 0)),
                pl.BlockSpec((None, CONV_W, tf), lambda i, j: (layer, 0, j)),
                pl.BlockSpec((None, 1, tf), lambda i, j: (layer, 0, j))]
    nl = wg.shape[0]
    args = [h, g_pre.reshape(nl, 1, d), g_post.reshape(nl, 1, d), wg, wu, wd, cw, cb.reshape(nl, 1, f)]
    if has_state:
        in_specs += [pl.BlockSpec((tm, tf), lambda i, j: (i, j))] * 2
        args += list(state_rows)
    return pl.pallas_call(
        functools.partial(_conv_ffn_kernel, tm=tm, seq_len=seq_len, has_state=has_state, tail=tail,
                          n_split=n_split),
        grid=(ni, nj),
        in_specs=in_specs,
        out_specs=[pl.BlockSpec((tm, d), lambda i, j: (i, 0)),
                   pl.BlockSpec((None, tail, tf), lambda i, j: (i, 0, j))],
        out_shape=[jax.ShapeDtypeStruct((m, d), F32),
                   jax.ShapeDtypeStruct((ni, tail, f), F32)],
        scratch_shapes=[pltpu.VMEM((tm, d), BF16),
                        pltpu.VMEM((tm + 8, tf), F32),
                        pltpu.VMEM((nj, 8, tf), F32)],
        compiler_params=_cparams(("arbitrary", "arbitrary")),
        name="conv_ffn",
    )(*args)


def _sb_logs(z, valid):
    sp = jnp.log2(1.0 + jnp.exp2(-jnp.abs(z)))
    log_stay = jnp.minimum(-z, 0.0) - sp
    log_beta = z + log_stay
    if valid is not None:
        log_stay = jnp.where(valid, log_stay, 0.0)
    return log_stay, log_beta


def _split_bf16(x):
    hi = x.astype(BF16)
    lo = (x - hi.astype(F32)).astype(BF16)
    return hi, lo


def _suffix_matrix(n, with_total):
    u = (np.arange(n)[:, None] > np.arange(n)[None, :]).astype(np.float32)
    if with_total:
        u = np.concatenate([u, np.ones((n, n), np.float32)], axis=1)
    return jnp.asarray(np.concatenate([u, u], axis=0), dtype=BF16)


def _sb_prompt_kernel(bias_ref, q_ref, k_ref, v_ref, u2_ref, o_ref, kb_ref, vb_ref, qs_ref, acc_ref, *, grp, hps):
    kh0 = pl.program_id(1) * hps
    qi = pl.program_id(2)
    tq = q_ref.shape[0]
    tk = SB_KEY_BLOCK
    rows = grp * tq

    @pl.when(qi == 0)
    def _():
        kb_ref[...] = k_ref[...].astype(BF16)
        vb_ref[...] = v_ref[...].astype(BF16)

    scale = (HD_A ** -0.5) * LOG2E
    row = lax.broadcasted_iota(jnp.int32, (rows, 1), 0)
    biases = []
    for hh in range(hps):
        for g in range(grp):
            c0 = (hh * grp + g) * HD_A
            qs_ref[hh, g * tq:(g + 1) * tq, :] = (q_ref[:, c0:c0 + HD_A] * scale).astype(BF16)
        bias = jnp.zeros((rows, 1), F32)
        for g in range(grp):
            bias = jnp.where(row // tq == g, bias_ref[(kh0 + hh) * grp + g] * LOG2E, bias)
        biases.append(bias)
    u2 = u2_ref[...]
    acc_ref[...] = jnp.zeros_like(acc_ref)

    def step(kb, carries, masked):
        start = pl.multiple_of(kb * tk, tk)
        valid = None
        if masked:
            q_pos = qi * tq + row % tq
            k_pos = kb * tk + lax.broadcasted_iota(jnp.int32, (rows, tk), 1)
            valid = k_pos < q_pos
        heads = range(hps)
        zs = [lax.dot_general(qs_ref[hh], kb_ref[pl.ds(start, tk), hh * HD_A:(hh + 1) * HD_A],
                              (((1,), (1,)), ((), ())), preferred_element_type=F32) + biases[hh] for hh in heads]
        logs = [_sb_logs(zs[hh], valid) for hh in heads]
        splits = [_split_bf16(logs[hh][0]) for hh in heads]
        laters = [jnp.dot(jnp.concatenate(splits[hh], axis=1), u2, preferred_element_type=F32) + carries[hh]
                  for hh in heads]
        out = []
        for hh in heads:
            a = jnp.exp2(logs[hh][1] + laters[hh])
            if masked:
                a = jnp.where(valid, a, 0.0)
            vblk = vb_ref[pl.ds(start, tk), hh * HD_A:(hh + 1) * HD_A]
            acc_ref[hh] += jnp.dot(a.astype(BF16), vblk, preferred_element_type=F32)
            out.append(carries[hh] + jnp.sum(logs[hh][0], axis=1, keepdims=True))
        return tuple(out)

    kb_diag = (qi * tq) // tk
    carries = step(kb_diag, tuple(jnp.zeros((rows, 1), F32) for _ in range(hps)), True)
    lax.fori_loop(0, kb_diag, lambda n, c: step(kb_diag - 1 - n, c, False), carries)
    for hh in range(hps):
        for g in range(grp):
            c0 = (hh * grp + g) * HD_A
            o_ref[:, c0:c0 + HD_A] = acc_ref[hh, g * tq:(g + 1) * tq, :].astype(o_ref.dtype)


def sb_prompt(qkv, sb_bias, batch, seq, n_heads):
    grp = n_heads // KV_A
    hps = SB_HEADS_PER_STEP
    nq = seq // Q_BLOCK
    kcol = n_heads // hps
    vcol = (n_heads + KV_A) // hps
    qw = hps * grp * HD_A
    return pl.pallas_call(
        functools.partial(_sb_prompt_kernel, grp=grp, hps=hps),
        grid=(batch, KV_A // hps, nq),
        in_specs=[pl.BlockSpec(memory_space=pltpu.SMEM),
                  pl.BlockSpec((Q_BLOCK, qw), lambda b, k, q: (b * nq + q, k)),
                  pl.BlockSpec((seq, hps * HD_A), lambda b, k, q: (b, kcol + k)),
                  pl.BlockSpec((seq, hps * HD_A), lambda b, k, q: (b, vcol + k)),
                  pl.BlockSpec((2 * SB_KEY_BLOCK, SB_KEY_BLOCK), lambda b, k, q: (0, 0))],
        out_specs=pl.BlockSpec((Q_BLOCK, qw), lambda b, k, q: (b * nq + q, k)),
        out_shape=jax.ShapeDtypeStruct((batch * seq, n_heads * HD_A), BF16),
        scratch_shapes=[pltpu.VMEM((seq, hps * HD_A), BF16),
                        pltpu.VMEM((seq, hps * HD_A), BF16),
                        pltpu.VMEM((hps, grp * Q_BLOCK, HD_A), BF16),
                        pltpu.VMEM((hps, grp * Q_BLOCK, HD_A), F32)],
        compiler_params=_cparams(("arbitrary", "arbitrary", "arbitrary")),
        name="sb_prompt",
    )(sb_bias, qkv, qkv, qkv, _suffix_matrix(SB_KEY_BLOCK, False))


def _sb_decode_kernel(pt_ref, bias_ref, q_ref, kn_ref, vn_ref, *rest, pages, grp, t_s):
    k_refs = rest[:pages]
    v_refs = rest[pages:2 * pages]
    u2_ref, o_ref, qs_ref, acc_ref, carry_ref = rest[2 * pages:]
    c = pl.program_id(1)
    rk = grp * t_s
    rows = KV_A * rk
    page = u2_ref.shape[1] // 2
    row = lax.broadcasted_iota(jnp.int32, (rows, 1), 0)
    bias = jnp.zeros((rows, 1), F32)
    for hh in range(KV_A * grp):
        bias = jnp.where(row // t_s == hh, bias_ref[hh] * LOG2E, bias)

    def head_rows(ref, k):
        return ref[pl.ds(k, page, stride=KV_A), :].astype(BF16)

    def process(krefs, vrefs, masked):
        n = len(krefs)
        zs = []
        for kr in krefs:
            zk = [lax.dot_general(qs_ref[k], head_rows(kr, k), (((1,), (1,)), ((), ())),
                                  preferred_element_type=F32) for k in range(KV_A)]
            zs.append(jnp.concatenate(zk, axis=0))
        z = jnp.concatenate(zs, axis=1) + bias
        valid = None
        if masked:
            valid = lax.broadcasted_iota(jnp.int32, (rows, n * page), 1) < row % t_s
        log_stay, log_beta = _sb_logs(z, valid)
        hi, lo = _split_bf16(log_stay)
        u2 = u2_ref[...]
        carry = carry_ref[...]
        a_pages = []
        for j in range(n):
            sl = slice(j * page, (j + 1) * page)
            r = jnp.dot(jnp.concatenate([hi[:, sl], lo[:, sl]], axis=1), u2, preferred_element_type=F32)
            a = jnp.exp2(log_beta[:, sl] + r[:, :page] + carry)
            if masked:
                a = jnp.where(valid[:, sl], a, 0.0)
            a_pages.append(a.astype(BF16))
            carry = carry + r[:, page:]
        carry_ref[...] = carry
        for k in range(KV_A):
            acc = acc_ref[k]
            for j in range(n):
                acc = acc + jnp.dot(a_pages[j][k * rk:(k + 1) * rk, :], head_rows(vrefs[j], k),
                                    preferred_element_type=F32)
            acc_ref[k] = acc

    @pl.when(c == 0)
    def _():
        qs_ref[...] = (q_ref[...] * ((HD_A ** -0.5) * LOG2E)).astype(BF16)
        acc_ref[...] = jnp.zeros_like(acc_ref)
        carry_ref[...] = jnp.zeros_like(carry_ref)
        process([kn_ref], [vn_ref], True)

    process(k_refs, v_refs, False)

    @pl.when(c == pl.num_programs(1) - 1)
    def _():
        o_ref[...] = acc_ref[...]


def sb_decode(q, k_new, v_new, cache_k, cache_v, layer, page_table, sb_bias, *, pages):
    n_seq, _, rk, _ = q.shape
    n_pages = page_table.shape[1]
    prow = cache_k.shape[2]
    page = prow // KV_A
    grp = sb_bias.shape[0] // KV_A
    t_s = rk // grp
    n_chunks = n_pages // pages

    def page_spec(j):
        return pl.BlockSpec((None, None, prow, HD_A),
                            lambda b, c, pt: (layer, pt[b, n_pages - 1 - (c * pages + j)], 0, 0))

    blk4 = pl.BlockSpec((None, KV_A, rk, HD_A), lambda b, c, pt: (b, 0, 0, 0))
    new_spec = pl.BlockSpec((None, prow, HD_A), lambda b, c, pt: (b, 0, 0))
    grid_spec = pltpu.PrefetchScalarGridSpec(
        num_scalar_prefetch=1,
        grid=(n_seq, n_chunks),
        in_specs=([pl.BlockSpec(memory_space=pltpu.SMEM), blk4, new_spec, new_spec]
                  + [page_spec(j) for j in range(pages)] * 2
                  + [pl.BlockSpec((2 * page, 2 * page), lambda b, c, pt: (0, 0))]),
        out_specs=blk4,
        scratch_shapes=[pltpu.VMEM((KV_A, rk, HD_A), BF16),
                        pltpu.VMEM((KV_A, rk, HD_A), F32),
                        pltpu.VMEM((KV_A * rk, page), F32)],
    )
    return pl.pallas_call(
        functools.partial(_sb_decode_kernel, pages=pages, grp=grp, t_s=t_s),
        grid_spec=grid_spec,
        out_shape=jax.ShapeDtypeStruct(q.shape, F32),
        compiler_params=_cparams(("arbitrary", "arbitrary")),
        name="sb_decode",
    )(page_table, sb_bias, q, k_new, v_new, *([cache_k] * pages), *([cache_v] * pages),
      _suffix_matrix(page, True))


def _t5_bucket_table(dist):
    d = np.maximum(dist, 0)
    large = MAX_EXACT + (np.log(np.maximum(d, 1).astype(np.float32) / np.float32(MAX_EXACT))
                         / np.float32(math.log(MAX_DISTANCE / MAX_EXACT))
                         * np.float32(N_BUCKETS - MAX_EXACT)).astype(np.int32)
    large = np.minimum(large, N_BUCKETS - 1)
    return np.where(d < MAX_EXACT, d, large).astype(np.int32)


def _build_bias_table(tbl_ref, bucket, valid, rb_ref, n_tables, head_of):
    def body(n, _):
        h = head_of(n)
        acc = jnp.zeros(bucket.shape, F32)
        for bb in range(N_BUCKETS):
            acc = jnp.where(bucket == bb, rb_ref[bb, h], acc)
        tbl_ref[n] = jnp.where(valid, acc, NEG)
        return 0
    lax.fori_loop(0, n_tables, body, 0)


def _win_softmax_pv(s, sink, v):
    m = jnp.maximum(jnp.max(s, axis=-1, keepdims=True), sink)
    p = jnp.exp(s - m)
    denom = jnp.sum(p, axis=-1, keepdims=True) + jnp.exp(sink - m)
    o = jnp.dot(p.astype(BF16), v, preferred_element_type=F32)
    return o / denom


def _win_prompt_kernel(rb_ref, sink_ref, q_ref, kvp_ref, kvc_ref, bkt_ref, o_ref, tbl_ref, *, n_heads):
    b = pl.program_id(0)
    qi = pl.program_id(1)
    tq = q_ref.shape[0]
    grp = n_heads // KV_B
    kvw = KV_B * HD_B
    t_idx = lax.broadcasted_iota(jnp.int32, (tq, 2 * WINDOW), 0)
    s_idx = lax.broadcasted_iota(jnp.int32, (tq, 2 * WINDOW), 1)

    @pl.when((b == 0) & (qi == 0))
    def _():
        dist = t_idx + WINDOW - s_idx
        valid = (dist >= 0) & (dist < WINDOW)
        _build_bias_table(tbl_ref, bkt_ref[...], valid, rb_ref, n_heads, lambda n: n)

    first = jnp.where((qi == 0) & (s_idx < WINDOW), NEG, 0.0)
    kv = jnp.concatenate([kvp_ref[...], kvc_ref[...]], axis=0).astype(BF16)
    scale = HD_B ** -0.5
    for k in range(KV_B):
        kk = kv[:, k * HD_B:(k + 1) * HD_B]
        vv = kv[:, kvw + k * HD_B:kvw + (k + 1) * HD_B]
        for g in range(grp):
            h = k * grp + g
            qh = q_ref[:, h * HD_B:(h + 1) * HD_B].astype(BF16)
            s = lax.dot_general(qh, kk, (((1,), (1,)), ((), ())), preferred_element_type=F32) * scale
            s = s + tbl_ref[h] + first
            o_ref[:, h * HD_B:(h + 1) * HD_B] = _win_softmax_pv(s, sink_ref[h], vv).astype(o_ref.dtype)


def win_prompt(q, kv, sinks, rel_bias, batch, seq):
    n_heads = q.shape[1] // HD_B
    nq = seq // WINDOW
    t = np.arange(WINDOW)[:, None]
    s = np.arange(2 * WINDOW)[None, :]
    bucket = jnp.asarray(_t5_bucket_table(t + WINDOW - s))
    return pl.pallas_call(
        functools.partial(_win_prompt_kernel, n_heads=n_heads),
        grid=(batch, nq),
        in_specs=[pl.BlockSpec(memory_space=pltpu.SMEM),
                  pl.BlockSpec(memory_space=pltpu.SMEM),
                  pl.BlockSpec((WINDOW, q.shape[1]), lambda b, i: (b * nq + i, 0)),
                  pl.BlockSpec((WINDOW, kv.shape[1]), lambda b, i: (jnp.maximum(b * nq + i - 1, 0), 0)),
                  pl.BlockSpec((WINDOW, kv.shape[1]), lambda b, i: (b * nq + i, 0)),
                  pl.BlockSpec((WINDOW, 2 * WINDOW), lambda b, i: (0, 0))],
        out_specs=pl.BlockSpec((WINDOW, q.shape[1]), lambda b, i: (b * nq + i, 0)),
        out_shape=jax.ShapeDtypeStruct(q.shape, BF16),
        scratch_shapes=[pltpu.VMEM((n_heads, WINDOW, 2 * WINDOW), F32)],
        compiler_params=_cparams(("arbitrary", "arbitrary")),
        name="win_prompt",
    )(rel_bias, sinks, q, kv, kv, bucket)


def _win_decode_kernel(rb_ref, sink_ref, q_ref, k_ref, v_ref, bkt_ref, o_ref, tbl_ref, *, grp, t_s):
    b = pl.program_id(0)
    rk, nk = bkt_ref.shape
    t_idx = lax.broadcasted_iota(jnp.int32, (rk, nk), 0) % t_s
    j_idx = lax.broadcasted_iota(jnp.int32, (rk, nk), 1)

    @pl.when(b == 0)
    def _():
        dist = t_idx + WINDOW - j_idx
        valid = (dist >= 0) & (dist < WINDOW)
        bucket = bkt_ref[...]
        row_g = lax.broadcasted_iota(jnp.int32, (rk, nk), 0) // t_s
        for k in range(KV_B):
            def body(g, tbl):
                acc = jnp.zeros((rk, nk), F32)
                for bb in range(N_BUCKETS):
                    acc = jnp.where(bucket == bb, rb_ref[bb, k * grp + g], acc)
                return jnp.where(row_g == g, acc, tbl)
            tbl = lax.fori_loop(0, grp, body, jnp.zeros((rk, nk), F32))
            tbl_ref[k] = jnp.where(valid, tbl, NEG)

    row = lax.broadcasted_iota(jnp.int32, (rk, 1), 0)
    scale = HD_B ** -0.5
    for k in range(KV_B):
        sink = jnp.zeros((rk, 1), F32)
        for g in range(grp):
            sink = jnp.where(row // t_s == g, sink_ref[k * grp + g], sink)
        s = lax.dot_general(q_ref[k].astype(BF16), k_ref[k].astype(BF16), (((1,), (1,)), ((), ())),
                            preferred_element_type=F32) * scale + tbl_ref[k]
        o_ref[k] = _win_softmax_pv(s, sink, v_ref[k].astype(BF16))


def win_decode(q, k, v, sinks, rel_bias, t_s):
    n_seq, _, rk, _ = q.shape
    nk = k.shape[2]
    grp = rk // t_s
    t = (np.arange(rk) % t_s)[:, None]
    j = np.arange(nk)[None, :]
    bucket = jnp.asarray(_t5_bucket_table(t + WINDOW - j))
    blk_q = pl.BlockSpec((None, KV_B, rk, HD_B), lambda b: (b, 0, 0, 0))
    blk_k = pl.BlockSpec((None, KV_B, nk, HD_B), lambda b: (b, 0, 0, 0))
    return pl.pallas_call(
        functools.partial(_win_decode_kernel, grp=grp, t_s=t_s),
        grid=(n_seq,),
        in_specs=[pl.BlockSpec(memory_space=pltpu.SMEM),
                  pl.BlockSpec(memory_space=pltpu.SMEM),
                  blk_q, blk_k, blk_k,
                  pl.BlockSpec((rk, nk), lambda b: (0, 0))],
        out_specs=blk_q,
        out_shape=jax.ShapeDtypeStruct(q.shape, F32),
        scratch_shapes=[pltpu.VMEM((KV_B, rk, nk), F32)],
        compiler_params=_cparams(("arbitrary",)),
        name="win_decode",
    )(rel_bias, sinks, q, k, v, bucket)


def _heads_to_rows(x, n_seq, t_s, n_kv, hd):
    grp = x.shape[1] // (n_kv * hd)
    x = x.reshape(n_seq, t_s, n_kv, grp, hd).transpose(0, 2, 3, 1, 4)
    return x.reshape(n_seq, n_kv, grp * t_s, hd)


def _rows_to_heads(x, t_s):
    n_seq, n_kv, rk, hd = x.shape
    grp = rk // t_s
    x = x.reshape(n_seq, n_kv, grp, t_s, hd).transpose(0, 3, 1, 2, 4)
    return x.reshape(n_seq * t_s, n_kv * grp * hd)


def kernel(x_prompt, x_sample, cache_k_a, cache_v_a, page_table, cache_k_b, cache_v_b, state_conv,
           w_qkv_a, w_o_a, sb_bias, g_kv, w_kv_b, w_q_b, w_o_b, sinks_b, rel_bias,
           g_pre_mix, g_post_mix, g_pre_ffn, g_post_ffn, w_gate, w_up, w_down, conv_w, conv_b):
    batch, seq, d = x_prompt.shape
    n_seq, t_s, _ = x_sample.shape
    n_a = w_qkv_a.shape[0]
    depth = w_gate.shape[0]
    d_ff = w_gate.shape[2]
    page = cache_k_a.shape[2]
    h_a = w_o_a.shape[1] // HD_A
    mp, ms = batch * seq, n_seq * t_s
    qw = h_a * HD_A
    kvw_a = KV_A * HD_A
    kvw_b = KV_B * HD_B
    assert t_s >= CONV_W - 1 and seq % TM_PROMPT == 0

    cache_k4 = cache_k_a.reshape(cache_k_a.shape[0], cache_k_a.shape[1], page * KV_A, HD_A)
    cache_v4 = cache_v_a.reshape(cache_v_a.shape[0], cache_v_a.shape[1], page * KV_A, HD_A)
    w_kv_b3 = w_kv_b.reshape(1, d, 2 * kvw_b)

    hp = x_prompt.reshape(mp, d)
    hs = x_sample.reshape(ms, d)
    ka_p, va_p, ka_s, va_s, conv_p, conv_s = [], [], [], [], [], []
    kv_p = kb_s = vb_s = kq_s = vq_s = None
    for l in range(depth):
        if l < n_a:
            qkv_p = norm_matmul(hp, g_pre_mix[l], w_qkv_a, l, TM_PROMPT, TN_PROJ)
            qkv_s = norm_matmul(hs, g_pre_mix[l], w_qkv_a, l, ms, TN_PROJ)
            op = sb_prompt(qkv_p, sb_bias[l], batch, seq, h_a)
            q_s = _heads_to_rows(qkv_s[:, :qw], n_seq, t_s, KV_A, HD_A)
            k_new = qkv_s[:, qw:qw + kvw_a]
            v_new = qkv_s[:, qw + kvw_a:]
            pad = ((0, 0), (0, (page - t_s) * KV_A), (0, 0))
            o_s = sb_decode(q_s, jnp.pad(k_new.reshape(n_seq, t_s * KV_A, HD_A), pad),
                            jnp.pad(v_new.reshape(n_seq, t_s * KV_A, HD_A), pad),
                            cache_k4, cache_v4, l, page_table, sb_bias[l], pages=SB_PAGES_PER_STEP)
            o_s = _rows_to_heads(o_s, t_s)
            hp = proj_res(op, w_o_a, l, g_post_mix[l], hp, TM_PROJ_RES)
            hs = proj_res(o_s, w_o_a, l, g_post_mix[l], hs, ms)
            ka_p.append(qkv_p[:, qw:qw + kvw_a].reshape(batch, seq, KV_A, HD_A))
            va_p.append(qkv_p[:, qw + kvw_a:].reshape(batch, seq, KV_A, HD_A))
            ka_s.append(k_new.reshape(n_seq, t_s, KV_A, HD_A))
            va_s.append(v_new.reshape(n_seq, t_s, KV_A, HD_A))
        else:
            j = l - n_a
            if j == 0:
                kv_p = norm_matmul(hp, g_kv, w_kv_b3, 0, TM_PROMPT, 2 * kvw_b)
                kv_s = norm_matmul(hs, g_kv, w_kv_b3, 0, ms, 2 * kvw_b)
                kb_s = jnp.concatenate([cache_k_b, kv_s[:, :kvw_b].reshape(n_seq, t_s, KV_B, HD_B)], axis=1)
                vb_s = jnp.concatenate([cache_v_b, kv_s[:, kvw_b:].reshape(n_seq, t_s, KV_B, HD_B)], axis=1)
                padk = ((0, 0), (0, 0), (0, 2 * WINDOW - (WINDOW + t_s)), (0, 0))
                kq_s = jnp.pad(kb_s.transpose(0, 2, 1, 3), padk)
                vq_s = jnp.pad(vb_s.transpose(0, 2, 1, 3), padk)
            q_p = norm_matmul(hp, g_pre_mix[l], w_q_b, j, TM_PROMPT, TN_PROJ)
            q_s = norm_matmul(hs, g_pre_mix[l], w_q_b, j, ms, TN_PROJ)
            op = win_prompt(q_p, kv_p, sinks_b[j], rel_bias, batch, seq)
            o_s = win_decode(_heads_to_rows(q_s, n_seq, t_s, KV_B, HD_B), kq_s, vq_s, sinks_b[j], rel_bias, t_s)
            o_s = _rows_to_heads(o_s, t_s)
            hp = proj_res(op, w_o_b, j, g_post_mix[l], hp, TM_PROJ_RES)
            hs = proj_res(o_s, w_o_b, j, g_post_mix[l], hs, ms)
        hp, gt_p = conv_ffn(hp, g_pre_ffn, g_post_ffn, w_gate, w_up, w_down, conv_w, conv_b, l,
                            tm=TM_PROMPT, tf=TF_PROMPT, seq_len=seq, n_split=FFN_ROW_GROUPS)
        st = state_conv[l]
        zeros = jnp.zeros((n_seq, t_s - 1, d_ff), F32)
        s1 = jnp.concatenate([st[:, 1:2], zeros], axis=1).reshape(ms, d_ff)
        s2 = jnp.concatenate([st, zeros[:, 1:]], axis=1).reshape(ms, d_ff)
        hs, gt_s = conv_ffn(hs, g_pre_ffn, g_post_ffn, w_gate, w_up, w_down, conv_w, conv_b, l,
                            tm=ms, tf=TF_SAMPLE, seq_len=t_s, n_split=1, state_rows=(s1, s2))
        tiles_per_seq = seq // TM_PROMPT
        conv_p.append(gt_p.reshape(batch, tiles_per_seq, 8, d_ff)[:, -1, 8 - (CONV_W - 1):, :])
        conv_s.append(gt_s.reshape(n_seq, t_s, d_ff)[:, t_s - (CONV_W - 1):, :])

    kvp4 = kv_p.reshape(batch, seq, 2, KV_B, HD_B)
    return (hp.reshape(batch, seq, d), hs.reshape(n_seq, t_s, d),
            jnp.stack(ka_p), jnp.stack(va_p), jnp.stack(ka_s), jnp.stack(va_s),
            kvp4[:, seq - WINDOW:, 0], kvp4[:, seq - WINDOW:, 1],
            kb_s[:, -WINDOW:], vb_s[:, -WINDOW:],
            jnp.stack(conv_p), jnp.stack(conv_s))
```

```python
import functools
import math

import numpy as np
import jax
import jax.numpy as jnp
from jax import lax
from jax.experimental import pallas as pl
from jax.experimental.pallas import tpu as pltpu

F32 = jnp.float32
BF16 = jnp.bfloat16

EPS = 1e-6
HD_A = 128
KV_A = 4
HD_B = 64
KV_B = 4
WINDOW = 128
Q_BLOCK = 128
N_BUCKETS = 32
MAX_EXACT = N_BUCKETS // 2
MAX_DISTANCE = WINDOW
CONV_W = 3
NEG = -1e30
LOG2E = math.log2(math.e)
SB_KEY_BLOCK = 256
SB_HEADS_PER_STEP = 2
SB_PAGES_PER_STEP = 16
VMEM_LIMIT = 56 * 1024 * 1024
TM_PROMPT = 1024
TF_PROMPT = 512
FFN_ROW_GROUPS = 4
TF_SAMPLE = 512
TN_PROJ = 512
TM_PROJ_RES = 256


def _cparams(sem):
    return pltpu.CompilerParams(dimension_semantics=sem, vmem_limit_bytes=VMEM_LIMIT)


def _rmsnorm_rows(x, g):
    ms = jnp.mean(x * x, axis=-1, keepdims=True)
    return x * lax.rsqrt(ms + EPS) * g


def _norm_matmul_kernel(h_ref, g_ref, w_ref, o_ref, hn_ref):
    @pl.when(pl.program_id(1) == 0)
    def _():
        hn_ref[...] = _rmsnorm_rows(h_ref[...], g_ref[...]).astype(BF16)

    o_ref[...] = jnp.dot(hn_ref[...], w_ref[...].astype(BF16), preferred_element_type=F32)


def norm_matmul(h, g, w, layer, tm, tn):
    m, d = h.shape
    n = w.shape[2]
    return pl.pallas_call(
        _norm_matmul_kernel,
        grid=(m // tm, n // tn),
        in_specs=[pl.BlockSpec((tm, d), lambda i, j: (i, 0)),
                  pl.BlockSpec((1, d), lambda i, j: (0, 0)),
                  pl.BlockSpec((None, d, tn), lambda i, j: (layer, 0, j))],
        out_specs=pl.BlockSpec((tm, tn), lambda i, j: (i, j)),
        out_shape=jax.ShapeDtypeStruct((m, n), F32),
        scratch_shapes=[pltpu.VMEM((tm, d), BF16)],
        compiler_params=_cparams(("arbitrary", "arbitrary")),
        name="norm_matmul",
    )(h, g.reshape(1, d), w)


def _proj_res_kernel(o_ref, w_ref, g_ref, h_ref, out_ref, wb_ref):
    @pl.when(pl.program_id(0) == 0)
    def _():
        wb_ref[...] = w_ref[...].astype(BF16)

    y = jnp.dot(o_ref[...].astype(BF16), wb_ref[...], preferred_element_type=F32)
    out_ref[...] = h_ref[...] + _rmsnorm_rows(y, g_ref[...])


def proj_res(o, w, layer, g, h, tm):
    m, k = o.shape
    d = w.shape[2]
    return pl.pallas_call(
        _proj_res_kernel,
        grid=(m // tm,),
        in_specs=[pl.BlockSpec((tm, k), lambda i: (i, 0)),
                  pl.BlockSpec((None, k, d), lambda i: (layer, 0, 0), pipeline_mode=pl.Buffered(1)),
                  pl.BlockSpec((1, d), lambda i: (0, 0)),
                  pl.BlockSpec((tm, d), lambda i: (i, 0))],
        out_specs=pl.BlockSpec((tm, d), lambda i: (i, 0)),
        out_shape=jax.ShapeDtypeStruct((m, d), F32),
        scratch_shapes=[pltpu.VMEM((k, d), BF16)],
        compiler_params=_cparams(("arbitrary",)),
        name="proj_res",
    )(o, w, g.reshape(1, d), h)


def _gelu_tanh(x):
    c = math.sqrt(2.0 / math.pi)
    return 0.5 * x * (1.0 + jnp.tanh(c * (x + 0.044715 * (x * x * x))))


def _conv_ffn_kernel(*refs, tm, seq_len, has_state, tail, n_split):
    if has_state:
        (h_ref, gpre_ref, gpost_ref, wg_ref, wu_ref, wd_ref, cw_ref, cb_ref, s1_ref, s2_ref,
         out_ref, gt_ref, hn_ref, gp_ref, carry_ref) = refs
    else:
        (h_ref, gpre_ref, gpost_ref, wg_ref, wu_ref, wd_ref, cw_ref, cb_ref,
         out_ref, gt_ref, hn_ref, gp_ref, carry_ref) = refs
    i = pl.program_id(0)
    j = pl.program_id(1)

    @pl.when(j == 0)
    def _():
        hn_ref[...] = _rmsnorm_rows(h_ref[...], gpre_ref[...]).astype(BF16)
        out_ref[...] = jnp.zeros_like(out_ref)

    @pl.when(i == 0)
    def _():
        carry_ref[j] = jnp.zeros(carry_ref.shape[1:], F32)

    wg = wg_ref[...]
    wu = wu_ref[...]
    wd = wd_ref[...]
    cw = cw_ref[...]
    cb = cb_ref[...]
    gp_ref[0:8, :] = carry_ref[j]
    th = tm // n_split
    gs, us = [], []
    for r in range(n_split):
        r0 = r * th
        hn = hn_ref[r0:r0 + th, :]
        g = jnp.dot(hn, wg, preferred_element_type=F32)
        us.append(jnp.dot(hn, wu, preferred_element_type=F32))
        gs.append(g)
        gp_ref[8 + r0:8 + r0 + th, :] = g
    carry_ref[j] = gs[-1][th - 8:, :]
    if tail == tm:
        for r in range(n_split):
            gt_ref[r * th:(r + 1) * th, :] = gs[r]
    else:
        gt_ref[...] = gs[-1][th - tail:, :]
    for r in range(n_split):
        r0 = r * th
        g, u = gs[r], us[r]
        g1 = gp_ref[7 + r0:7 + r0 + th, :]
        g2 = gp_ref[6 + r0:6 + r0 + th, :]
        pos = (i * tm + r0 + lax.broadcasted_iota(jnp.int32, (th, 1), 0)) % seq_len
        if has_state:
            g1 = jnp.where(pos >= 1, g1, s1_ref[r0:r0 + th, :])
            g2 = jnp.where(pos >= 2, g2, s2_ref[r0:r0 + th, :])
        else:
            g1 = jnp.where(pos >= 1, g1, 0.0)
            g2 = jnp.where(pos >= 2, g2, 0.0)
        gc = cb + g2 * cw[0:1, :] + g1 * cw[1:2, :] + g * cw[2:3, :]
        y = (_gelu_tanh(gc) * u).astype(BF16)
        out_ref[r0:r0 + th, :] += jnp.dot(y, wd, preferred_element_type=F32)

    @pl.when(j == pl.num_programs(1) - 1)
    def _():
        out_ref[...] = h_ref[...] + _rmsnorm_rows(out_ref[...], gpost_ref[...])


def conv_ffn(h, g_pre, g_post, wg, wu, wd, cw, cb, layer, *, tm, tf, seq_len, n_split, state_rows=None):
    m, d = h.shape
    f = wg.shape[2]
    has_state = state_rows is not None
    tail = tm if has_state else 8
    ni, nj = m // tm, f // tf
    in_specs = [pl.BlockSpec((tm, d), lambda i, j: (i, 0), pipeline_mode=pl.Buffered(1)),
                pl.BlockSpec((None, 1, d), lambda i, j: (layer, 0, 0)),
                pl.BlockSpec((None, 1, d), lambda i, j: (layer, 0, 0)),
                pl.BlockSpec((None, d, tf), lambda i, j: (layer, 0, j)),
                pl.BlockSpec((None, d, tf), lambda i, j: (layer, 0, j)),
                pl.BlockSpec((None, tf, d), lambda i, j: (layer, j, 0)),
                pl.BlockSpec((None, CONV_W, tf), lambda i, j: (layer, 0, j)),
                pl.BlockSpec((None, 1, tf), lambda i, j: (layer, 0, j))]
    nl = wg.shape[0]
    args = [h, g_pre.reshape(nl, 1, d), g_post.reshape(nl, 1, d), wg, wu, wd, cw, cb.reshape(nl, 1, f)]
    if has_state:
        in_specs += [pl.BlockSpec((tm, tf), lambda i, j: (i, j))] * 2
        args += list(state_rows)
    return pl.pallas_call(
        functools.partial(_conv_ffn_kernel, tm=tm, seq_len=seq_len, has_state=has_state, tail=tail,
                          n_split=n_split),
        grid=(ni, nj),
        in_specs=in_specs,
        out_specs=[pl.BlockSpec((tm, d), lambda i, j: (i, 0)),
                   pl.BlockSpec((None, tail, tf), lambda i, j: (i, 0, j))],
        out_shape=[jax.ShapeDtypeStruct((m, d), F32),
                   jax.ShapeDtypeStruct((ni, tail, f), F32)],
        scratch_shapes=[pltpu.VMEM((tm, d), BF16),
                        pltpu.VMEM((tm + 8, tf), F32),
                        pltpu.VMEM((nj, 8, tf), F32)],
        compiler_params=_cparams(("arbitrary", "arbitrary")),
        name="conv_ffn",
    )(*args)


def _sb_logs(z, valid):
    sp = jnp.log2(1.0 + jnp.exp2(-jnp.abs(z)))
    log_stay = jnp.minimum(-z, 0.0) - sp
    log_beta = z + log_stay
    if valid is not None:
        log_stay = jnp.where(valid, log_stay, 0.0)
    return log_stay, log_beta


def _suffix_matrix(n, with_total):
    u = (np.arange(n)[:, None] > np.arange(n)[None, :]).astype(np.float32)
    if with_total:
        u = np.concatenate([u, np.ones((n, n), np.float32)], axis=1)
    return jnp.asarray(u, dtype=BF16)


def _sb_prompt_kernel(bias_ref, q_ref, k_ref, v_ref, u2_ref, o_ref, kb_ref, vb_ref, qs_ref, acc_ref, *, grp, hps):
    kh0 = pl.program_id(1) * hps
    qi = pl.program_id(2)
    tq = q_ref.shape[0]
    tk = SB_KEY_BLOCK
    rows = grp * tq

    @pl.when(qi == 0)
    def _():
        kb_ref[...] = k_ref[...].astype(BF16)
        vb_ref[...] = v_ref[...].astype(BF16)

    scale = (HD_A ** -0.5) * LOG2E
    row = lax.broadcasted_iota(jnp.int32, (rows, 1), 0)
    biases = []
    for hh in range(hps):
        for g in range(grp):
            c0 = (hh * grp + g) * HD_A
            qs_ref[hh, g * tq:(g + 1) * tq, :] = (q_ref[:, c0:c0 + HD_A] * scale).astype(BF16)
        bias = jnp.zeros((rows, 1), F32)
        for g in range(grp):
            bias = jnp.where(row // tq == g, bias_ref[(kh0 + hh) * grp + g] * LOG2E, bias)
        biases.append(bias)
    u2 = u2_ref[...]
    acc_ref[...] = jnp.zeros_like(acc_ref)

    def step(kb, carries, masked):
        start = pl.multiple_of(kb * tk, tk)
        valid = None
        if masked:
            q_pos = qi * tq + row % tq
            k_pos = kb * tk + lax.broadcasted_iota(jnp.int32, (rows, tk), 1)
            valid = k_pos < q_pos
        heads = range(hps)
        zs = [lax.dot_general(qs_ref[hh], kb_ref[pl.ds(start, tk), hh * HD_A:(hh + 1) * HD_A],
                              (((1,), (1,)), ((), ())), preferred_element_type=F32) + biases[hh] for hh in heads]
        logs = [_sb_logs(zs[hh], valid) for hh in heads]
        laters = [jnp.dot(logs[hh][0].astype(BF16), u2, preferred_element_type=F32) + carries[hh]
                  for hh in heads]
        out = []
        for hh in heads:
            a = jnp.exp2(logs[hh][1] + laters[hh])
            if masked:
                a = jnp.where(valid, a, 0.0)
            vblk = vb_ref[pl.ds(start, tk), hh * HD_A:(hh + 1) * HD_A]
            acc_ref[hh] += jnp.dot(a.astype(BF16), vblk, preferred_element_type=F32)
            out.append(carries[hh] + jnp.sum(logs[hh][0], axis=1, keepdims=True))
        return tuple(out)

    kb_diag = (qi * tq) // tk
    carries = step(kb_diag, tuple(jnp.zeros((rows, 1), F32) for _ in range(hps)), True)
    lax.fori_loop(0, kb_diag, lambda n, c: step(kb_diag - 1 - n, c, False), carries)
    for hh in range(hps):
        for g in range(grp):
            c0 = (hh * grp + g) * HD_A
            o_ref[:, c0:c0 + HD_A] = acc_ref[hh, g * tq:(g + 1) * tq, :].astype(o_ref.dtype)


def sb_prompt(qkv, sb_bias, batch, seq, n_heads):
    grp = n_heads // KV_A
    hps = SB_HEADS_PER_STEP
    nq = seq // Q_BLOCK
    kcol = n_heads // hps
    vcol = (n_heads + KV_A) // hps
    qw = hps * grp * HD_A
    return pl.pallas_call(
        functools.partial(_sb_prompt_kernel, grp=grp, hps=hps),
        grid=(batch, KV_A // hps, nq),
        in_specs=[pl.BlockSpec(memory_space=pltpu.SMEM),
                  pl.BlockSpec((Q_BLOCK, qw), lambda b, k, q: (b * nq + q, k)),
                  pl.BlockSpec((seq, hps * HD_A), lambda b, k, q: (b, kcol + k)),
                  pl.BlockSpec((seq, hps * HD_A), lambda b, k, q: (b, vcol + k)),
                  pl.BlockSpec((SB_KEY_BLOCK, SB_KEY_BLOCK), lambda b, k, q: (0, 0))],
        out_specs=pl.BlockSpec((Q_BLOCK, qw), lambda b, k, q: (b * nq + q, k)),
        out_shape=jax.ShapeDtypeStruct((batch * seq, n_heads * HD_A), BF16),
        scratch_shapes=[pltpu.VMEM((seq, hps * HD_A), BF16),
                        pltpu.VMEM((seq, hps * HD_A), BF16),
                        pltpu.VMEM((hps, grp * Q_BLOCK, HD_A), BF16),
                        pltpu.VMEM((hps, grp * Q_BLOCK, HD_A), F32)],
        compiler_params=_cparams(("arbitrary", "arbitrary", "arbitrary")),
        name="sb_prompt",
    )(sb_bias, qkv, qkv, qkv, _suffix_matrix(SB_KEY_BLOCK, False))


def _sb_decode_kernel(pt_ref, bias_ref, q_ref, kn_ref, vn_ref, *rest, pages, grp, t_s):
    k_refs = rest[:pages]
    v_refs = rest[pages:2 * pages]
    u2_ref, o_ref, qs_ref, acc_ref, carry_ref = rest[2 * pages:]
    c = pl.program_id(1)
    rk = grp * t_s
    rows = KV_A * rk
    page = u2_ref.shape[0]
    row = lax.broadcasted_iota(jnp.int32, (rows, 1), 0)
    bias = jnp.zeros((rows, 1), F32)
    for hh in range(KV_A * grp):
        bias = jnp.where(row // t_s == hh, bias_ref[hh] * LOG2E, bias)

    def head_rows(ref, k):
        return ref[pl.ds(k, page, stride=KV_A), :].astype(BF16)

    def process(krefs, vrefs, masked):
        n = len(krefs)
        zs = []
        for kr in krefs:
            zk = [lax.dot_general(qs_ref[k], head_rows(kr, k), (((1,), (1,)), ((), ())),
                                  preferred_element_type=F32) for k in range(KV_A)]
            zs.append(jnp.concatenate(zk, axis=0))
        z = jnp.concatenate(zs, axis=1) + bias
        valid = None
        if masked:
            valid = lax.broadcasted_iota(jnp.int32, (rows, n * page), 1) < row % t_s
        log_stay, log_beta = _sb_logs(z, valid)
        stay16 = log_stay.astype(BF16)
        u2 = u2_ref[...]
        carry = carry_ref[...]
        a_pages = []
        for j in range(n):
            sl = slice(j * page, (j + 1) * page)
            r = jnp.dot(stay16[:, sl], u2, preferred_element_type=F32)
            a = jnp.exp2(log_beta[:, sl] + r[:, :page] + carry)
            if masked:
                a = jnp.where(valid[:, sl], a, 0.0)
            a_pages.append(a.astype(BF16))
            carry = carry + r[:, page:]
        carry_ref[...] = carry
        for k in range(KV_A):
            acc = acc_ref[k]
            for j in range(n):
                acc = acc + jnp.dot(a_pages[j][k * rk:(k + 1) * rk, :], head_rows(vrefs[j], k),
                                    preferred_element_type=F32)
            acc_ref[k] = acc

    @pl.when(c == 0)
    def _():
        qs_ref[...] = (q_ref[...] * ((HD_A ** -0.5) * LOG2E)).astype(BF16)
        acc_ref[...] = jnp.zeros_like(acc_ref)
        carry_ref[...] = jnp.zeros_like(carry_ref)
        process([kn_ref], [vn_ref], True)

    process(k_refs, v_refs, False)

    @pl.when(c == pl.num_programs(1) - 1)
    def _():
        o_ref[...] = acc_ref[...]


def sb_decode(q, k_new, v_new, cache_k, cache_v, layer, page_table, sb_bias, *, pages):
    n_seq, _, rk, _ = q.shape
    n_pages = page_table.shape[1]
    prow = cache_k.shape[2]
    page = prow // KV_A
    grp = sb_bias.shape[0] // KV_A
    t_s = rk // grp
    n_chunks = n_pages // pages

    def page_spec(j):
        return pl.BlockSpec((None, None, prow, HD_A),
                            lambda b, c, pt: (layer, pt[b, n_pages - 1 - (c * pages + j)], 0, 0))

    blk4 = pl.BlockSpec((None, KV_A, rk, HD_A), lambda b, c, pt: (b, 0, 0, 0))
    new_spec = pl.BlockSpec((None, prow, HD_A), lambda b, c, pt: (b, 0, 0))
    grid_spec = pltpu.PrefetchScalarGridSpec(
        num_scalar_prefetch=1,
        grid=(n_seq, n_chunks),
        in_specs=([pl.BlockSpec(memory_space=pltpu.SMEM), blk4, new_spec, new_spec]
                  + [page_spec(j) for j in range(pages)] * 2
                  + [pl.BlockSpec((page, 2 * page), lambda b, c, pt: (0, 0))]),
        out_specs=blk4,
        scratch_shapes=[pltpu.VMEM((KV_A, rk, HD_A), BF16),
                        pltpu.VMEM((KV_A, rk, HD_A), F32),
                        pltpu.VMEM((KV_A * rk, page), F32)],
    )
    return pl.pallas_call(
        functools.partial(_sb_decode_kernel, pages=pages, grp=grp, t_s=t_s),
        grid_spec=grid_spec,
        out_shape=jax.ShapeDtypeStruct(q.shape, F32),
        compiler_params=_cparams(("arbitrary", "arbitrary")),
        name="sb_decode",
    )(page_table, sb_bias, q, k_new, v_new, *([cache_k] * pages), *([cache_v] * pages),
      _suffix_matrix(page, True))


def _t5_bucket_table(dist):
    d = np.maximum(dist, 0)
    large = MAX_EXACT + (np.log(np.maximum(d, 1).astype(np.float32) / np.float32(MAX_EXACT))
                         / np.float32(math.log(MAX_DISTANCE / MAX_EXACT))
                         * np.float32(N_BUCKETS - MAX_EXACT)).astype(np.int32)
    large = np.minimum(large, N_BUCKETS - 1)
    return np.where(d < MAX_EXACT, d, large).astype(np.int32)


def _win_softmax_pv(s, sink, v):
    m = jnp.maximum(jnp.max(s, axis=-1, keepdims=True), sink)
    p = jnp.exp(s - m)
    denom = jnp.sum(p, axis=-1, keepdims=True) + jnp.exp(sink - m)
    o = jnp.dot(p.astype(BF16), v, preferred_element_type=F32)
    return o / denom


def _win_prompt_kernel(rb_ref, sink_ref, q_ref, kvp_ref, kvc_ref, bkt_ref, o_ref, tbl_ref, sinkcol_ref, *, n_heads):
    b = pl.program_id(0)
    qi = pl.program_id(1)
    tq = q_ref.shape[0]
    grp = n_heads // KV_B
    kvw = KV_B * HD_B

    @pl.when((b == 0) & (qi == 0))
    def _():
        t_idx = lax.broadcasted_iota(jnp.int32, (tq, 2 * WINDOW), 0)
        s_idx = lax.broadcasted_iota(jnp.int32, (tq, 2 * WINDOW), 1)
        dist = t_idx + WINDOW - s_idx
        valid = (dist >= 0) & (dist < WINDOW)
        bucket = bkt_ref[...]

        def body(h, _):
            acc = jnp.zeros(bucket.shape, F32)
            for bb in range(N_BUCKETS):
                acc = jnp.where(bucket == bb, rb_ref[bb, h], acc)
            r0 = pl.multiple_of((h % grp) * tq, tq)
            tbl_ref[0, h // grp, pl.ds(r0, tq), :] = jnp.where(valid, acc, NEG)
            tbl_ref[1, h // grp, pl.ds(r0, tq), :] = jnp.where(valid & (s_idx >= WINDOW), acc, NEG)
            sinkcol_ref[h // grp, pl.ds(r0, tq), :] = jnp.full((tq, 1), sink_ref[h], F32)
            return 0
        lax.fori_loop(0, n_heads, body, 0)

    variant = (qi == 0).astype(jnp.int32)
    kv = jnp.concatenate([kvp_ref[...], kvc_ref[...]], axis=0).astype(BF16)
    scale = HD_B ** -0.5
    kvs = range(KV_B)
    qgs = [jnp.concatenate([q_ref[:, (k * grp + g) * HD_B:(k * grp + g + 1) * HD_B] for g in range(grp)], axis=0)
           for k in kvs]
    ss = [lax.dot_general((qgs[k] * scale).astype(BF16), kv[:, k * HD_B:(k + 1) * HD_B], (((1,), (1,)), ((), ())),
                          preferred_element_type=F32) + tbl_ref[variant, k] for k in kvs]
    ms = [jnp.maximum(jnp.max(ss[k], axis=-1, keepdims=True), sinkcol_ref[k]) for k in kvs]
    ps = [jnp.exp(ss[k] - ms[k]) for k in kvs]
    dens = [jnp.sum(ps[k], axis=-1, keepdims=True) + jnp.exp(sinkcol_ref[k] - ms[k]) for k in kvs]
    os_ = [jnp.dot(ps[k].astype(BF16), kv[:, kvw + k * HD_B:kvw + (k + 1) * HD_B], preferred_element_type=F32)
           / dens[k] for k in kvs]
    for k in kvs:
        for g in range(grp):
            h = k * grp + g
            o_ref[:, h * HD_B:(h + 1) * HD_B] = os_[k][g * tq:(g + 1) * tq, :].astype(o_ref.dtype)


def win_prompt(q, kv, sinks, rel_bias, batch, seq):
    n_heads = q.shape[1] // HD_B
    nq = seq // WINDOW
    t = np.arange(WINDOW)[:, None]
    s = np.arange(2 * WINDOW)[None, :]
    bucket = jnp.asarray(_t5_bucket_table(t + WINDOW - s))
    return pl.pallas_call(
        functools.partial(_win_prompt_kernel, n_heads=n_heads),
        grid=(batch, nq),
        in_specs=[pl.BlockSpec(memory_space=pltpu.SMEM),
                  pl.BlockSpec(memory_space=pltpu.SMEM),
                  pl.BlockSpec((WINDOW, q.shape[1]), lambda b, i: (b * nq + i, 0)),
                  pl.BlockSpec((WINDOW, kv.shape[1]), lambda b, i: (jnp.maximum(b * nq + i - 1, 0), 0)),
                  pl.BlockSpec((WINDOW, kv.shape[1]), lambda b, i: (b * nq + i, 0)),
                  pl.BlockSpec((WINDOW, 2 * WINDOW), lambda b, i: (0, 0))],
        out_specs=pl.BlockSpec((WINDOW, q.shape[1]), lambda b, i: (b * nq + i, 0)),
        out_shape=jax.ShapeDtypeStruct(q.shape, BF16),
        scratch_shapes=[pltpu.VMEM((2, KV_B, (n_heads // KV_B) * WINDOW, 2 * WINDOW), F32),
                        pltpu.VMEM((KV_B, (n_heads // KV_B) * WINDOW, 1), F32)],
        compiler_params=_cparams(("arbitrary", "arbitrary")),
        name="win_prompt",
    )(rel_bias, sinks, q, kv, kv, bucket)


def _win_decode_kernel(rb_ref, sink_ref, q_ref, k_ref, v_ref, bkt_ref, o_ref, tbl_ref, *, grp, t_s):
    b = pl.program_id(0)
    rk, nk = bkt_ref.shape
    t_idx = lax.broadcasted_iota(jnp.int32, (rk, nk), 0) % t_s
    j_idx = lax.broadcasted_iota(jnp.int32, (rk, nk), 1)

    @pl.when(b == 0)
    def _():
        dist = t_idx + WINDOW - j_idx
        valid = (dist >= 0) & (dist < WINDOW)
        bucket = bkt_ref[...]
        row_g = lax.broadcasted_iota(jnp.int32, (rk, nk), 0) // t_s
        for k in range(KV_B):
            def body(g, tbl):
                acc = jnp.zeros((rk, nk), F32)
                for bb in range(N_BUCKETS):
                    acc = jnp.where(bucket == bb, rb_ref[bb, k * grp + g], acc)
                return jnp.where(row_g == g, acc, tbl)
            tbl = lax.fori_loop(0, grp, body, jnp.zeros((rk, nk), F32))
            tbl_ref[k] = jnp.where(valid, tbl, NEG)

    row = lax.broadcasted_iota(jnp.int32, (rk, 1), 0)
    scale = HD_B ** -0.5
    for k in range(KV_B):
        sink = jnp.zeros((rk, 1), F32)
        for g in range(grp):
            sink = jnp.where(row // t_s == g, sink_ref[k * grp + g], sink)
        s = lax.dot_general(q_ref[k].astype(BF16), k_ref[k].astype(BF16), (((1,), (1,)), ((), ())),
                            preferred_element_type=F32) * scale + tbl_ref[k]
        o_ref[k] = _win_softmax_pv(s, sink, v_ref[k].astype(BF16))


def win_decode(q, k, v, sinks, rel_bias, t_s):
    n_seq, _, rk, _ = q.shape
    nk = k.shape[2]
    grp = rk // t_s
    t = (np.arange(rk) % t_s)[:, None]
    j = np.arange(nk)[None, :]
    bucket = jnp.asarray(_t5_bucket_table(t + WINDOW - j))
    blk_q = pl.BlockSpec((None, KV_B, rk, HD_B), lambda b: (b, 0, 0, 0))
    blk_k = pl.BlockSpec((None, KV_B, nk, HD_B), lambda b: (b, 0, 0, 0))
    return pl.pallas_call(
        functools.partial(_win_decode_kernel, grp=grp, t_s=t_s),
        grid=(n_seq,),
        in_specs=[pl.BlockSpec(memory_space=pltpu.SMEM),
                  pl.BlockSpec(memory_space=pltpu.SMEM),
                  blk_q, blk_k, blk_k,
                  pl.BlockSpec((rk, nk), lambda b: (0, 0))],
        out_specs=blk_q,
        out_shape=jax.ShapeDtypeStruct(q.shape, F32),
        scratch_shapes=[pltpu.VMEM((KV_B, rk, nk), F32)],
        compiler_params=_cparams(("arbitrary",)),
        name="win_decode",
    )(rel_bias, sinks, q, k, v, bucket)


def _heads_to_rows(x, n_seq, t_s, n_kv, hd):
    grp = x.shape[1] // (n_kv * hd)
    x = x.reshape(n_seq, t_s, n_kv, grp, hd).transpose(0, 2, 3, 1, 4)
    return x.reshape(n_seq, n_kv, grp * t_s, hd)


def _rows_to_heads(x, t_s):
    n_seq, n_kv, rk, hd = x.shape
    grp = rk // t_s
    x = x.reshape(n_seq, n_kv, grp, t_s, hd).transpose(0, 3, 1, 2, 4)
    return x.reshape(n_seq * t_s, n_kv * grp * hd)


def kernel(x_prompt, x_sample, cache_k_a, cache_v_a, page_table, cache_k_b, cache_v_b, state_conv,
           w_qkv_a, w_o_a, sb_bias, g_kv, w_kv_b, w_q_b, w_o_b, sinks_b, rel_bias,
           g_pre_mix, g_post_mix, g_pre_ffn, g_post_ffn, w_gate, w_up, w_down, conv_w, conv_b):
    batch, seq, d = x_prompt.shape
    n_seq, t_s, _ = x_sample.shape
    n_a = w_qkv_a.shape[0]
    depth = w_gate.shape[0]
    d_ff = w_gate.shape[2]
    page = cache_k_a.shape[2]
    h_a = w_o_a.shape[1] // HD_A
    mp, ms = batch * seq, n_seq * t_s
    qw = h_a * HD_A
    kvw_a = KV_A * HD_A
    kvw_b = KV_B * HD_B
    assert t_s >= CONV_W - 1 and seq % TM_PROMPT == 0

    cache_k4 = cache_k_a.reshape(cache_k_a.shape[0], cache_k_a.shape[1], page * KV_A, HD_A)
    cache_v4 = cache_v_a.reshape(cache_v_a.shape[0], cache_v_a.shape[1], page * KV_A, HD_A)
    w_kv_b3 = w_kv_b.reshape(1, d, 2 * kvw_b)
    w_gate, w_up, w_down = w_gate.astype(BF16), w_up.astype(BF16), w_down.astype(BF16)

    hp = x_prompt.reshape(mp, d)
    hs = x_sample.reshape(ms, d)
    ka_p, va_p, ka_s, va_s, conv_p, conv_s = [], [], [], [], [], []
    kv_p = kb_s = vb_s = kq_s = vq_s = None
    for l in range(depth):
        if l < n_a:
            qkv_p = norm_matmul(hp, g_pre_mix[l], w_qkv_a, l, TM_PROMPT, TN_PROJ)
            qkv_s = norm_matmul(hs, g_pre_mix[l], w_qkv_a, l, ms, TN_PROJ)
            op = sb_prompt(qkv_p, sb_bias[l], batch, seq, h_a)
            q_s = _heads_to_rows(qkv_s[:, :qw], n_seq, t_s, KV_A, HD_A)
            k_new = qkv_s[:, qw:qw + kvw_a]
            v_new = qkv_s[:, qw + kvw_a:]
            pad = ((0, 0), (0, (page - t_s) * KV_A), (0, 0))
            o_s = sb_decode(q_s, jnp.pad(k_new.reshape(n_seq, t_s * KV_A, HD_A), pad),
                            jnp.pad(v_new.reshape(n_seq, t_s * KV_A, HD_A), pad),
                            cache_k4, cache_v4, l, page_table, sb_bias[l], pages=SB_PAGES_PER_STEP)
            o_s = _rows_to_heads(o_s, t_s)
            hp = proj_res(op, w_o_a, l, g_post_mix[l], hp, TM_PROJ_RES)
            hs = proj_res(o_s, w_o_a, l, g_post_mix[l], hs, ms)
            ka_p.append(qkv_p[:, qw:qw + kvw_a].reshape(batch, seq, KV_A, HD_A))
            va_p.append(qkv_p[:, qw + kvw_a:].reshape(batch, seq, KV_A, HD_A))
            ka_s.append(k_new.reshape(n_seq, t_s, KV_A, HD_A))
            va_s.append(v_new.reshape(n_seq, t_s, KV_A, HD_A))
        else:
            j = l - n_a
            if j == 0:
                kv_p = norm_matmul(hp, g_kv, w_kv_b3, 0, TM_PROMPT, 2 * kvw_b)
                kv_s = norm_matmul(hs, g_kv, w_kv_b3, 0, ms, 2 * kvw_b)
                kb_s = jnp.concatenate([cache_k_b, kv_s[:, :kvw_b].reshape(n_seq, t_s, KV_B, HD_B)], axis=1)
                vb_s = jnp.concatenate([cache_v_b, kv_s[:, kvw_b:].reshape(n_seq, t_s, KV_B, HD_B)], axis=1)
                padk = ((0, 0), (0, 0), (0, 2 * WINDOW - (WINDOW + t_s)), (0, 0))
                kq_s = jnp.pad(kb_s.transpose(0, 2, 1, 3), padk)
                vq_s = jnp.pad(vb_s.transpose(0, 2, 1, 3), padk)
            q_p = norm_matmul(hp, g_pre_mix[l], w_q_b, j, TM_PROMPT, TN_PROJ)
            q_s = norm_matmul(hs, g_pre_mix[l], w_q_b, j, ms, TN_PROJ)
            op = win_prompt(q_p, kv_p, sinks_b[j], rel_bias, batch, seq)
            o_s = win_decode(_heads_to_rows(q_s, n_seq, t_s, KV_B, HD_B), kq_s, vq_s, sinks_b[j], rel_bias, t_s)
            o_s = _rows_to_heads(o_s, t_s)
            hp = proj_res(op, w_o_b, j, g_post_mix[l], hp, TM_PROJ_RES)
            hs = proj_res(o_s, w_o_b, j, g_post_mix[l], hs, ms)
        hp, gt_p = conv_ffn(hp, g_pre_ffn, g_post_ffn, w_gate, w_up, w_down, conv_w, conv_b, l,
                            tm=TM_PROMPT, tf=TF_PROMPT, seq_len=seq, n_split=FFN_ROW_GROUPS)
        st = state_conv[l]
        zeros = jnp.zeros((n_seq, t_s - 1, d_ff), F32)
        s1 = jnp.concatenate([st[:, 1:2], zeros], axis=1).reshape(ms, d_ff)
        s2 = jnp.concatenate([st, zeros[:, 1:]], axis=1).reshape(ms, d_ff)
        hs, gt_s = conv_ffn(hs, g_pre_ffn, g_post_ffn, w_gate, w_up, w_down, conv_w, conv_b, l,
                            tm=ms, tf=TF_SAMPLE, seq_len=t_s, n_split=1, state_rows=(s1, s2))
        tiles_per_seq = seq // TM_PROMPT
        conv_p.append(gt_p.reshape(batch, tiles_per_seq, 8, d_ff)[:, -1, 8 - (CONV_W - 1):, :])
        conv_s.append(gt_s.reshape(n_seq, t_s, d_ff)[:, t_s - (CONV_W - 1):, :])

    kvp4 = kv_p.reshape(batch, seq, 2, KV_B, HD_B)
    return (hp.reshape(batch, seq, d), hs.reshape(n_seq, t_s, d),
            jnp.stack(ka_p), jnp.stack(va_p), jnp.stack(ka_s), jnp.stack(va_s),
            kvp4[:, seq - WINDOW:, 0], kvp4[:, seq - WINDOW:, 1],
            kb_s[:, -WINDOW:], vb_s[:, -WINDOW:],
            jnp.stack(conv_p), jnp.stack(conv_s))
```

```python
import functools
import math

import numpy as np
import jax
import jax.numpy as jnp
from jax import lax
from jax.experimental import pallas as pl
from jax.experimental.pallas import tpu as pltpu

F32 = jnp.float32
BF16 = jnp.bfloat16

EPS = 1e-6
HD_A = 128
KV_A = 4
HD_B = 64
KV_B = 4
WINDOW = 128
Q_BLOCK = 128
N_BUCKETS = 32
MAX_EXACT = N_BUCKETS // 2
MAX_DISTANCE = WINDOW
CONV_W = 3
NEG = -1e30
LOG2E = math.log2(math.e)
SB_KEY_BLOCK = 256
SB_PAGES_PER_STEP = 16
VMEM_LIMIT = 56 * 1024 * 1024
TM_PROMPT = 1024
TF_PROMPT = 512
FFN_ROW_GROUPS = 4
TF_SAMPLE = 512
TN_PROJ = 512
TM_PROJ_RES = 256


def _cparams(sem):
    return pltpu.CompilerParams(dimension_semantics=sem, vmem_limit_bytes=VMEM_LIMIT)


def _rmsnorm_rows(x, g):
    ms = jnp.mean(x * x, axis=-1, keepdims=True)
    return x * lax.rsqrt(ms + EPS) * g


def _norm_matmul_kernel(h_ref, g_ref, w_ref, o_ref, hn_ref, *, scale):
    @pl.when(pl.program_id(1) == 0)
    def _():
        hn_ref[...] = _rmsnorm_rows(h_ref[...], g_ref[...]).astype(BF16)

    y = jnp.dot(hn_ref[...], w_ref[...].astype(BF16), preferred_element_type=F32)
    o_ref[...] = (y if scale is None else y * scale).astype(o_ref.dtype)


def norm_matmul(h, g, w, layer, tm, tn, out_dtype=F32, scale=None):
    m, d = h.shape
    n = w.shape[2]
    return pl.pallas_call(
        functools.partial(_norm_matmul_kernel, scale=scale),
        grid=(m // tm, n // tn),
        in_specs=[pl.BlockSpec((tm, d), lambda i, j: (i, 0)),
                  pl.BlockSpec((1, d), lambda i, j: (0, 0)),
                  pl.BlockSpec((None, d, tn), lambda i, j: (layer, 0, j))],
        out_specs=pl.BlockSpec((tm, tn), lambda i, j: (i, j)),
        out_shape=jax.ShapeDtypeStruct((m, n), out_dtype),
        scratch_shapes=[pltpu.VMEM((tm, d), BF16)],
        compiler_params=_cparams(("arbitrary", "arbitrary")),
        name="norm_matmul",
    )(h, g.reshape(1, d), w)


def _norm_qkv_kernel(*refs, n_q_tiles, scale, aliased):
    if aliased:
        h_ref, g_ref, w_ref, _, _, q_ref, k_ref, v_ref, hn_ref = refs
    else:
        h_ref, g_ref, w_ref, q_ref, k_ref, v_ref, hn_ref = refs
    j = pl.program_id(1)
    tm = h_ref.shape[0]

    @pl.when(j == 0)
    def _():
        hn_ref[...] = _rmsnorm_rows(h_ref[...], g_ref[...]).astype(BF16)

    y = jnp.dot(hn_ref[...], w_ref[...].astype(BF16), preferred_element_type=F32)

    @pl.when(j < n_q_tiles)
    def _():
        q_ref[...] = (y * scale).astype(q_ref.dtype)

    def rows_out(ref):
        for kh in range(KV_A):
            ref[pl.ds(kh, tm, stride=KV_A), :] = y[:, kh * HD_A:(kh + 1) * HD_A]

    @pl.when(j == n_q_tiles)
    def _():
        rows_out(k_ref)

    @pl.when(j == n_q_tiles + 1)
    def _():
        rows_out(v_ref)


def norm_qkv(h, g, w, layer, tm, q_scale, kv_prev=None):
    m, d = h.shape
    nl = w.shape[0]
    tn = KV_A * HD_A
    qw = w.shape[2] - 2 * tn
    nqt = qw // tn
    aliased = kv_prev is not None
    kv_shape = jax.ShapeDtypeStruct((nl, m * KV_A, HD_A), F32)
    kv_spec = pl.BlockSpec((None, tm * KV_A, HD_A), lambda i, j: (layer, i, 0))
    in_specs = [pl.BlockSpec((tm, d), lambda i, j: (i, 0)),
                pl.BlockSpec((1, d), lambda i, j: (0, 0)),
                pl.BlockSpec((None, d, tn), lambda i, j: (layer, 0, j))]
    args = [h, g.reshape(1, d), w]
    if aliased:
        in_specs += [pl.BlockSpec(memory_space=pl.ANY)] * 2
        args += list(kv_prev)
    return pl.pallas_call(
        functools.partial(_norm_qkv_kernel, n_q_tiles=nqt, scale=q_scale, aliased=aliased),
        grid=(m // tm, nqt + 2),
        in_specs=in_specs,
        out_specs=[pl.BlockSpec((tm, tn), lambda i, j: (i, jnp.minimum(j, nqt - 1))), kv_spec, kv_spec],
        out_shape=[jax.ShapeDtypeStruct((m, qw), BF16), kv_shape, kv_shape],
        scratch_shapes=[pltpu.VMEM((tm, d), BF16)],
        input_output_aliases={3: 1, 4: 2} if aliased else {},
        compiler_params=_cparams(("arbitrary", "arbitrary")),
        name="norm_qkv",
    )(*args)


def _proj_res_kernel(o_ref, w_ref, g_ref, h_ref, out_ref, wb_ref):
    @pl.when(pl.program_id(0) == 0)
    def _():
        wb_ref[...] = w_ref[...].astype(BF16)

    y = jnp.dot(o_ref[...].astype(BF16), wb_ref[...], preferred_element_type=F32)
    out_ref[...] = h_ref[...] + _rmsnorm_rows(y, g_ref[...])


def proj_res(o, w, layer, g, h, tm):
    m, k = o.shape
    d = w.shape[2]
    return pl.pallas_call(
        _proj_res_kernel,
        grid=(m // tm,),
        in_specs=[pl.BlockSpec((tm, k), lambda i: (i, 0)),
                  pl.BlockSpec((None, k, d), lambda i: (layer, 0, 0), pipeline_mode=pl.Buffered(1)),
                  pl.BlockSpec((1, d), lambda i: (0, 0)),
                  pl.BlockSpec((tm, d), lambda i: (i, 0))],
        out_specs=pl.BlockSpec((tm, d), lambda i: (i, 0)),
        out_shape=jax.ShapeDtypeStruct((m, d), F32),
        scratch_shapes=[pltpu.VMEM((k, d), BF16)],
        compiler_params=_cparams(("arbitrary",)),
        name="proj_res",
    )(o, w, g.reshape(1, d), h)


def _gelu_tanh(x):
    c = math.sqrt(2.0 / math.pi)
    return 0.5 * x * (1.0 + jnp.tanh(c * (x + 0.044715 * (x * x * x))))


def _conv_ffn_kernel(*refs, tm, seq_len, has_state, tail, n_split):
    if has_state:
        (h_ref, gpre_ref, gpost_ref, wg_ref, wu_ref, wd_ref, cw_ref, cb_ref, s1_ref, s2_ref,
         out_ref, gt_ref, hn_ref, gp_ref, carry_ref) = refs
    else:
        (h_ref, gpre_ref, gpost_ref, wg_ref, wu_ref, wd_ref, cw_ref, cb_ref,
         out_ref, gt_ref, hn_ref, gp_ref, carry_ref) = refs
    i = pl.program_id(0)
    j = pl.program_id(1)

    @pl.when(j == 0)
    def _():
        hn_ref[...] = _rmsnorm_rows(h_ref[...], gpre_ref[...]).astype(BF16)
        out_ref[...] = jnp.zeros_like(out_ref)

    @pl.when(i == 0)
    def _():
        carry_ref[j] = jnp.zeros(carry_ref.shape[1:], F32)

    wg = wg_ref[...]
    wu = wu_ref[...]
    wd = wd_ref[...]
    cw = cw_ref[...]
    cb = cb_ref[...]
    gp_ref[0:8, :] = carry_ref[j]
    th = tm // n_split
    gs, us = [], []
    for r in range(n_split):
        r0 = r * th
        hn = hn_ref[r0:r0 + th, :]
        g = jnp.dot(hn, wg, preferred_element_type=F32)
        us.append(jnp.dot(hn, wu, preferred_element_type=F32))
        gs.append(g)
        gp_ref[8 + r0:8 + r0 + th, :] = g
    carry_ref[j] = gs[-1][th - 8:, :]
    if tail == tm:
        for r in range(n_split):
            gt_ref[r * th:(r + 1) * th, :] = gs[r]
    else:
        gt_ref[...] = gs[-1][th - tail:, :]
    for r in range(n_split):
        r0 = r * th
        g, u = gs[r], us[r]
        g1 = gp_ref[7 + r0:7 + r0 + th, :]
        g2 = gp_ref[6 + r0:6 + r0 + th, :]
        pos = (i * tm + r0 + lax.broadcasted_iota(jnp.int32, (th, 1), 0)) % seq_len
        if has_state:
            g1 = jnp.where(pos >= 1, g1, s1_ref[r0:r0 + th, :])
            g2 = jnp.where(pos >= 2, g2, s2_ref[r0:r0 + th, :])
        else:
            g1 = jnp.where(pos >= 1, g1, 0.0)
            g2 = jnp.where(pos >= 2, g2, 0.0)
        gc = cb + g2 * cw[0:1, :] + g1 * cw[1:2, :] + g * cw[2:3, :]
        y = (_gelu_tanh(gc) * u).astype(BF16)
        out_ref[r0:r0 + th, :] += jnp.dot(y, wd, preferred_element_type=F32)

    @pl.when(j == pl.num_programs(1) - 1)
    def _():
        out_ref[...] = h_ref[...] + _rmsnorm_rows(out_ref[...], gpost_ref[...])


def conv_ffn(h, g_pre, g_post, wg, wu, wd, cw, cb, layer, *, tm, tf, seq_len, n_split, state_rows=None):
    m, d = h.shape
    f = wg.shape[2]
    has_state = state_rows is not None
    tail = tm if has_state else 8
    ni, nj = m // tm, f // tf
    in_specs = [pl.BlockSpec((tm, d), lambda i, j: (i, 0), pipeline_mode=pl.Buffered(1)),
                pl.BlockSpec((None, 1, d), lambda i, j: (layer, 0, 0)),
                pl.BlockSpec((None, 1, d), lambda i, j: (layer, 0, 0)),
                pl.BlockSpec((None, d, tf), lambda i, j: (layer, 0, j)),
                pl.BlockSpec((None, d, tf), lambda i, j: (layer, 0, j)),
                pl.BlockSpec((None, tf, d), lambda i, j: (layer, j, 0)),
                pl.BlockSpec((None, CONV_W, tf), lambda i, j: (layer, 0, j)),
                pl.BlockSpec((None, 1, tf), lambda i, j: (layer, 0, j))]
    nl = wg.shape[0]
    args = [h, g_pre.reshape(nl, 1, d), g_post.reshape(nl, 1, d), wg, wu, wd, cw, cb.reshape(nl, 1, f)]
    if has_state:
        in_specs += [pl.BlockSpec((tm, tf), lambda i, j: (i, j))] * 2
        args += list(state_rows)
    return pl.pallas_call(
        functools.partial(_conv_ffn_kernel, tm=tm, seq_len=seq_len, has_state=has_state, tail=tail,
                          n_split=n_split),
        grid=(ni, nj),
        in_specs=in_specs,
        out_specs=[pl.BlockSpec((tm, d), lambda i, j: (i, 0)),
                   pl.BlockSpec((None, tail, tf), lambda i, j: (i, 0, j))],
        out_shape=[jax.ShapeDtypeStruct((m, d), F32),
                   jax.ShapeDtypeStruct((ni, tail, f), F32)],
        scratch_shapes=[pltpu.VMEM((tm, d), BF16),
                        pltpu.VMEM((tm + 8, tf), F32),
                        pltpu.VMEM((nj, 8, tf), F32)],
        compiler_params=_cparams(("arbitrary", "arbitrary")),
        name="conv_ffn",
    )(*args)


def _sb_logs(z, valid):
    nz = -z
    sp = jnp.log2(1.0 + jnp.exp2(jnp.minimum(z, nz)))
    log_stay = jnp.minimum(nz, 0.0) - sp
    log_beta = z + log_stay
    if valid is not None:
        log_stay = jnp.where(valid, log_stay, 0.0)
    return log_stay, log_beta


def _suffix_matrix(n, with_total):
    u = (np.arange(n)[:, None] > np.arange(n)[None, :]).astype(np.float32)
    if with_total:
        u = np.concatenate([u, np.ones((n, n), np.float32)], axis=1)
    return jnp.asarray(u, dtype=BF16)


def _sb_prompt_kernel(bias_ref, q_ref, k_ref, v_ref, u2_ref, o_ref, kb_ref, vb_ref, qs_ref, acc_ref, *, grp):
    qi = pl.program_id(1)
    tq = q_ref.shape[0]
    seq = kb_ref.shape[0]
    tk = SB_KEY_BLOCK
    rows = grp * tq
    heads = range(KV_A)

    @pl.when(qi == 0)
    def _():
        for hh in heads:
            kb_ref[:, hh * HD_A:(hh + 1) * HD_A] = k_ref[pl.ds(hh, seq, stride=KV_A), :].astype(BF16)
            vb_ref[:, hh * HD_A:(hh + 1) * HD_A] = v_ref[pl.ds(hh, seq, stride=KV_A), :].astype(BF16)

    row = lax.broadcasted_iota(jnp.int32, (rows, 1), 0)
    biases = []
    for hh in heads:
        for g in range(grp):
            c0 = (hh * grp + g) * HD_A
            qs_ref[hh, g * tq:(g + 1) * tq, :] = q_ref[:, c0:c0 + HD_A]
        bias = jnp.zeros((rows, 1), F32)
        for g in range(grp):
            bias = jnp.where(row // tq == g, bias_ref[hh * grp + g] * LOG2E, bias)
        biases.append(bias)
    u2 = u2_ref[...]
    acc_ref[...] = jnp.zeros_like(acc_ref)

    def step(kb, carries, masked):
        start = pl.multiple_of(kb * tk, tk)
        valid = None
        if masked:
            q_pos = qi * tq + row % tq
            k_pos = kb * tk + lax.broadcasted_iota(jnp.int32, (rows, tk), 1)
            valid = k_pos < q_pos
        zs = [lax.dot_general(qs_ref[hh], kb_ref[pl.ds(start, tk), hh * HD_A:(hh + 1) * HD_A],
                              (((1,), (1,)), ((), ())), preferred_element_type=F32) + biases[hh] for hh in heads]
        logs = [_sb_logs(zs[hh], valid) for hh in heads]
        laters = [jnp.dot(logs[hh][0].astype(BF16), u2, preferred_element_type=F32) + carries[hh]
                  for hh in heads]
        out = []
        for hh in heads:
            a = jnp.exp2(logs[hh][1] + laters[hh])
            if masked:
                a = jnp.where(valid, a, 0.0)
            vblk = vb_ref[pl.ds(start, tk), hh * HD_A:(hh + 1) * HD_A]
            acc_ref[hh] += jnp.dot(a.astype(BF16), vblk, preferred_element_type=F32)
            out.append(carries[hh] + jnp.sum(logs[hh][0], axis=1, keepdims=True))
        return tuple(out)

    kb_diag = (qi * tq) // tk
    carries = step(kb_diag, tuple(jnp.zeros((rows, 1), F32) for _ in heads), True)
    lax.fori_loop(0, kb_diag, lambda n, c: step(kb_diag - 1 - n, c, False), carries)
    for hh in heads:
        for g in range(grp):
            c0 = (hh * grp + g) * HD_A
            o_ref[:, c0:c0 + HD_A] = acc_ref[hh, g * tq:(g + 1) * tq, :].astype(o_ref.dtype)


def sb_prompt(q, k_all, v_all, layer, sb_bias, batch, seq):
    qw = q.shape[1]
    grp = qw // (KV_A * HD_A)
    nq = seq // Q_BLOCK
    kv_spec = pl.BlockSpec((None, seq * KV_A, HD_A), lambda b, i: (layer, b, 0))
    return pl.pallas_call(
        functools.partial(_sb_prompt_kernel, grp=grp),
        grid=(batch, nq),
        in_specs=[pl.BlockSpec(memory_space=pltpu.SMEM),
                  pl.BlockSpec((Q_BLOCK, qw), lambda b, i: (b * nq + i, 0)),
                  kv_spec, kv_spec,
                  pl.BlockSpec((SB_KEY_BLOCK, SB_KEY_BLOCK), lambda b, i: (0, 0))],
        out_specs=pl.BlockSpec((Q_BLOCK, qw), lambda b, i: (b * nq + i, 0)),
        out_shape=jax.ShapeDtypeStruct((batch * seq, qw), BF16),
        scratch_shapes=[pltpu.VMEM((seq, KV_A * HD_A), BF16),
                        pltpu.VMEM((seq, KV_A * HD_A), BF16),
                        pltpu.VMEM((KV_A, grp * Q_BLOCK, HD_A), BF16),
                        pltpu.VMEM((KV_A, grp * Q_BLOCK, HD_A), F32)],
        compiler_params=_cparams(("arbitrary", "arbitrary")),
        name="sb_prompt",
    )(sb_bias, q, k_all, v_all, _suffix_matrix(SB_KEY_BLOCK, False))


def _sb_decode_kernel(pt_ref, bias_ref, q_ref, kn_ref, vn_ref, *rest, pages, grp, t_s):
    k_refs = rest[:pages]
    v_refs = rest[pages:2 * pages]
    u2_ref, o_ref, qs_ref, acc_ref, carry_ref = rest[2 * pages:]
    c = pl.program_id(1)
    rk = grp * t_s
    rows = KV_A * rk
    page = u2_ref.shape[0]
    row = lax.broadcasted_iota(jnp.int32, (rows, 1), 0)
    bias = jnp.zeros((rows, 1), F32)
    for hh in range(KV_A * grp):
        bias = jnp.where(row // t_s == hh, bias_ref[hh] * LOG2E, bias)

    def head_rows(ref, k):
        return ref[pl.ds(k, page, stride=KV_A), :].astype(BF16)

    def process(krefs, vrefs, masked):
        n = len(krefs)
        zs = []
        for kr in krefs:
            zk = [lax.dot_general(qs_ref[k], head_rows(kr, k), (((1,), (1,)), ((), ())),
                                  preferred_element_type=F32) for k in range(KV_A)]
            zs.append(jnp.concatenate(zk, axis=0))
        z = jnp.concatenate(zs, axis=1) + bias
        valid = None
        if masked:
            valid = lax.broadcasted_iota(jnp.int32, (rows, n * page), 1) < row % t_s
        log_stay, log_beta = _sb_logs(z, valid)
        stay16 = log_stay.astype(BF16)
        u2 = u2_ref[...]
        carry = carry_ref[...]
        a_pages = []
        for j in range(n):
            sl = slice(j * page, (j + 1) * page)
            r = jnp.dot(stay16[:, sl], u2, preferred_element_type=F32)
            a = jnp.exp2(log_beta[:, sl] + r[:, :page] + carry)
            if masked:
                a = jnp.where(valid[:, sl], a, 0.0)
            a_pages.append(a.astype(BF16))
            carry = carry + r[:, page:]
        carry_ref[...] = carry
        for k in range(KV_A):
            acc = acc_ref[k]
            for j in range(n):
                acc = acc + jnp.dot(a_pages[j][k * rk:(k + 1) * rk, :], head_rows(vrefs[j], k),
                                    preferred_element_type=F32)
            acc_ref[k] = acc

    @pl.when(c == 0)
    def _():
        qs_ref[...] = (q_ref[...] * ((HD_A ** -0.5) * LOG2E)).astype(BF16)
        acc_ref[...] = jnp.zeros_like(acc_ref)
        carry_ref[...] = jnp.zeros_like(carry_ref)
        process([kn_ref], [vn_ref], True)

    process(k_refs, v_refs, False)

    @pl.when(c == pl.num_programs(1) - 1)
    def _():
        o_ref[...] = acc_ref[...]


def sb_decode(q, k_new, v_new, cache_k, cache_v, layer, page_table, sb_bias, *, pages):
    n_seq, _, rk, _ = q.shape
    n_pages = page_table.shape[1]
    prow = cache_k.shape[2]
    page = prow // KV_A
    grp = sb_bias.shape[0] // KV_A
    t_s = rk // grp
    n_chunks = n_pages // pages

    def page_spec(j):
        return pl.BlockSpec((None, None, prow, HD_A),
                            lambda b, c, pt: (layer, pt[b, n_pages - 1 - (c * pages + j)], 0, 0))

    blk4 = pl.BlockSpec((None, KV_A, rk, HD_A), lambda b, c, pt: (b, 0, 0, 0))
    new_spec = pl.BlockSpec((None, prow, HD_A), lambda b, c, pt: (b, 0, 0))
    grid_spec = pltpu.PrefetchScalarGridSpec(
        num_scalar_prefetch=1,
        grid=(n_seq, n_chunks),
        in_specs=([pl.BlockSpec(memory_space=pltpu.SMEM), blk4, new_spec, new_spec]
                  + [page_spec(j) for j in range(pages)] * 2
                  + [pl.BlockSpec((page, 2 * page), lambda b, c, pt: (0, 0))]),
        out_specs=blk4,
        scratch_shapes=[pltpu.VMEM((KV_A, rk, HD_A), BF16),
                        pltpu.VMEM((KV_A, rk, HD_A), F32),
                        pltpu.VMEM((KV_A * rk, page), F32)],
    )
    return pl.pallas_call(
        functools.partial(_sb_decode_kernel, pages=pages, grp=grp, t_s=t_s),
        grid_spec=grid_spec,
        out_shape=jax.ShapeDtypeStruct(q.shape, F32),
        compiler_params=_cparams(("arbitrary", "arbitrary")),
        name="sb_decode",
    )(page_table, sb_bias, q, k_new, v_new, *([cache_k] * pages), *([cache_v] * pages),
      _suffix_matrix(page, True))


def _t5_bucket_table(dist):
    d = np.maximum(dist, 0)
    large = MAX_EXACT + (np.log(np.maximum(d, 1).astype(np.float32) / np.float32(MAX_EXACT))
                         / np.float32(math.log(MAX_DISTANCE / MAX_EXACT))
                         * np.float32(N_BUCKETS - MAX_EXACT)).astype(np.int32)
    large = np.minimum(large, N_BUCKETS - 1)
    return np.where(d < MAX_EXACT, d, large).astype(np.int32)


def _win_softmax_pv(s, sink, v):
    m = jnp.maximum(jnp.max(s, axis=-1, keepdims=True), sink)
    p = jnp.exp(s - m)
    denom = jnp.sum(p, axis=-1, keepdims=True) + jnp.exp(sink - m)
    o = jnp.dot(p.astype(BF16), v, preferred_element_type=F32)
    return o / denom


def _win_prompt_kernel(rb_ref, sink_ref, q_ref, kvp_ref, kvc_ref, bkt_ref, o_ref, tbl_ref, sinkcol_ref, *, n_heads):
    b = pl.program_id(0)
    qi = pl.program_id(1)
    tq = q_ref.shape[0]
    grp = n_heads // KV_B
    kvw = KV_B * HD_B

    @pl.when((b == 0) & (qi == 0))
    def _():
        t_idx = lax.broadcasted_iota(jnp.int32, (tq, 2 * WINDOW), 0)
        s_idx = lax.broadcasted_iota(jnp.int32, (tq, 2 * WINDOW), 1)
        dist = t_idx + WINDOW - s_idx
        valid = (dist >= 0) & (dist < WINDOW)
        bucket = bkt_ref[...]

        def body(h, _):
            acc = jnp.zeros(bucket.shape, F32)
            for bb in range(N_BUCKETS):
                acc = jnp.where(bucket == bb, rb_ref[bb, h], acc)
            r0 = pl.multiple_of((h % grp) * tq, tq)
            tbl_ref[0, h // grp, pl.ds(r0, tq), :] = jnp.where(valid, acc, NEG)
            tbl_ref[1, h // grp, pl.ds(r0, tq), :] = jnp.where(valid & (s_idx >= WINDOW), acc, NEG)
            sinkcol_ref[h // grp, pl.ds(r0, tq), :] = jnp.full((tq, 1), sink_ref[h], F32)
            return 0
        lax.fori_loop(0, n_heads, body, 0)

    variant = (qi == 0).astype(jnp.int32)
    kv = jnp.concatenate([kvp_ref[...], kvc_ref[...]], axis=0).astype(BF16)
    kvs = range(KV_B)
    qgs = [jnp.concatenate([q_ref[:, (k * grp + g) * HD_B:(k * grp + g + 1) * HD_B] for g in range(grp)], axis=0)
           for k in kvs]
    ss = [lax.dot_general(qgs[k], kv[:, k * HD_B:(k + 1) * HD_B], (((1,), (1,)), ((), ())),
                          preferred_element_type=F32) + tbl_ref[variant, k] for k in kvs]
    ms = [jnp.maximum(jnp.max(ss[k], axis=-1, keepdims=True), sinkcol_ref[k]) for k in kvs]
    ps = [jnp.exp(ss[k] - ms[k]) for k in kvs]
    dens = [jnp.sum(ps[k], axis=-1, keepdims=True) + jnp.exp(sinkcol_ref[k] - ms[k]) for k in kvs]
    os_ = [jnp.dot(ps[k].astype(BF16), kv[:, kvw + k * HD_B:kvw + (k + 1) * HD_B], preferred_element_type=F32)
           / dens[k] for k in kvs]
    for k in kvs:
        for g in range(grp):
            h = k * grp + g
            o_ref[:, h * HD_B:(h + 1) * HD_B] = os_[k][g * tq:(g + 1) * tq, :].astype(o_ref.dtype)


def win_prompt(q, kv, sinks, rel_bias, batch, seq):
    n_heads = q.shape[1] // HD_B
    nq = seq // WINDOW
    t = np.arange(WINDOW)[:, None]
    s = np.arange(2 * WINDOW)[None, :]
    bucket = jnp.asarray(_t5_bucket_table(t + WINDOW - s))
    return pl.pallas_call(
        functools.partial(_win_prompt_kernel, n_heads=n_heads),
        grid=(batch, nq),
        in_specs=[pl.BlockSpec(memory_space=pltpu.SMEM),
                  pl.BlockSpec(memory_space=pltpu.SMEM),
                  pl.BlockSpec((WINDOW, q.shape[1]), lambda b, i: (b * nq + i, 0)),
                  pl.BlockSpec((WINDOW, kv.shape[1]), lambda b, i: (jnp.maximum(b * nq + i - 1, 0), 0)),
                  pl.BlockSpec((WINDOW, kv.shape[1]), lambda b, i: (b * nq + i, 0)),
                  pl.BlockSpec((WINDOW, 2 * WINDOW), lambda b, i: (0, 0))],
        out_specs=pl.BlockSpec((WINDOW, q.shape[1]), lambda b, i: (b * nq + i, 0)),
        out_shape=jax.ShapeDtypeStruct(q.shape, BF16),
        scratch_shapes=[pltpu.VMEM((2, KV_B, (n_heads // KV_B) * WINDOW, 2 * WINDOW), F32),
                        pltpu.VMEM((KV_B, (n_heads // KV_B) * WINDOW, 1), F32)],
        compiler_params=_cparams(("arbitrary", "arbitrary")),
        name="win_prompt",
    )(rel_bias, sinks, q, kv, kv, bucket)


def _win_decode_kernel(rb_ref, sink_ref, q_ref, k_ref, v_ref, bkt_ref, o_ref, tbl_ref, *, grp, t_s):
    b = pl.program_id(0)
    rk, nk = bkt_ref.shape
    t_idx = lax.broadcasted_iota(jnp.int32, (rk, nk), 0) % t_s
    j_idx = lax.broadcasted_iota(jnp.int32, (rk, nk), 1)

    @pl.when(b == 0)
    def _():
        dist = t_idx + WINDOW - j_idx
        valid = (dist >= 0) & (dist < WINDOW)
        bucket = bkt_ref[...]
        row_g = lax.broadcasted_iota(jnp.int32, (rk, nk), 0) // t_s
        for k in range(KV_B):
            def body(g, tbl):
                acc = jnp.zeros((rk, nk), F32)
                for bb in range(N_BUCKETS):
                    acc = jnp.where(bucket == bb, rb_ref[bb, k * grp + g], acc)
                return jnp.where(row_g == g, acc, tbl)
            tbl = lax.fori_loop(0, grp, body, jnp.zeros((rk, nk), F32))
            tbl_ref[k] = jnp.where(valid, tbl, NEG)

    row = lax.broadcasted_iota(jnp.int32, (rk, 1), 0)
    scale = HD_B ** -0.5
    for k in range(KV_B):
        sink = jnp.zeros((rk, 1), F32)
        for g in range(grp):
            sink = jnp.where(row // t_s == g, sink_ref[k * grp + g], sink)
        s = lax.dot_general(q_ref[k].astype(BF16), k_ref[k].astype(BF16), (((1,), (1,)), ((), ())),
                            preferred_element_type=F32) * scale + tbl_ref[k]
        o_ref[k] = _win_softmax_pv(s, sink, v_ref[k].astype(BF16))


def win_decode(q, k, v, sinks, rel_bias, t_s):
    n_seq, _, rk, _ = q.shape
    nk = k.shape[2]
    grp = rk // t_s
    t = (np.arange(rk) % t_s)[:, None]
    j = np.arange(nk)[None, :]
    bucket = jnp.asarray(_t5_bucket_table(t + WINDOW - j))
    blk_q = pl.BlockSpec((None, KV_B, rk, HD_B), lambda b: (b, 0, 0, 0))
    blk_k = pl.BlockSpec((None, KV_B, nk, HD_B), lambda b: (b, 0, 0, 0))
    return pl.pallas_call(
        functools.partial(_win_decode_kernel, grp=grp, t_s=t_s),
        grid=(n_seq,),
        in_specs=[pl.BlockSpec(memory_space=pltpu.SMEM),
                  pl.BlockSpec(memory_space=pltpu.SMEM),
                  blk_q, blk_k, blk_k,
                  pl.BlockSpec((rk, nk), lambda b: (0, 0))],
        out_specs=blk_q,
        out_shape=jax.ShapeDtypeStruct(q.shape, F32),
        scratch_shapes=[pltpu.VMEM((KV_B, rk, nk), F32)],
        compiler_params=_cparams(("arbitrary",)),
        name="win_decode",
    )(rel_bias, sinks, q, k, v, bucket)


def _heads_to_rows(x, n_seq, t_s, n_kv, hd):
    grp = x.shape[1] // (n_kv * hd)
    x = x.reshape(n_seq, t_s, n_kv, grp, hd).transpose(0, 2, 3, 1, 4)
    return x.reshape(n_seq, n_kv, grp * t_s, hd)


def _rows_to_heads(x, t_s):
    n_seq, n_kv, rk, hd = x.shape
    grp = rk // t_s
    x = x.reshape(n_seq, n_kv, grp, t_s, hd).transpose(0, 3, 1, 2, 4)
    return x.reshape(n_seq * t_s, n_kv * grp * hd)


def kernel(x_prompt, x_sample, cache_k_a, cache_v_a, page_table, cache_k_b, cache_v_b, state_conv,
           w_qkv_a, w_o_a, sb_bias, g_kv, w_kv_b, w_q_b, w_o_b, sinks_b, rel_bias,
           g_pre_mix, g_post_mix, g_pre_ffn, g_post_ffn, w_gate, w_up, w_down, conv_w, conv_b):
    batch, seq, d = x_prompt.shape
    n_seq, t_s, _ = x_sample.shape
    n_a = w_qkv_a.shape[0]
    depth = w_gate.shape[0]
    d_ff = w_gate.shape[2]
    page = cache_k_a.shape[2]
    h_a = w_o_a.shape[1] // HD_A
    mp, ms = batch * seq, n_seq * t_s
    qw = h_a * HD_A
    kvw_a = KV_A * HD_A
    kvw_b = KV_B * HD_B
    assert t_s >= CONV_W - 1 and seq % TM_PROMPT == 0

    cache_k4 = cache_k_a.reshape(cache_k_a.shape[0], cache_k_a.shape[1], page * KV_A, HD_A)
    cache_v4 = cache_v_a.reshape(cache_v_a.shape[0], cache_v_a.shape[1], page * KV_A, HD_A)
    w_kv_b3 = w_kv_b.reshape(1, d, 2 * kvw_b)
    w_gate, w_up, w_down = w_gate.astype(BF16), w_up.astype(BF16), w_down.astype(BF16)

    hp = x_prompt.reshape(mp, d)
    hs = x_sample.reshape(ms, d)
    ka_s, va_s, conv_p, conv_s = [], [], [], []
    kv_all = kv_p = kb_s = vb_s = kq_s = vq_s = None
    for l in range(depth):
        if l < n_a:
            q_p, k_all, v_all = norm_qkv(hp, g_pre_mix[l], w_qkv_a, l, TM_PROMPT, (HD_A ** -0.5) * LOG2E, kv_all)
            kv_all = (k_all, v_all)
            qkv_s = norm_matmul(hs, g_pre_mix[l], w_qkv_a, l, ms, TN_PROJ)
            op = sb_prompt(q_p, k_all, v_all, l, sb_bias[l], batch, seq)
            q_s = _heads_to_rows(qkv_s[:, :qw], n_seq, t_s, KV_A, HD_A)
            k_new = qkv_s[:, qw:qw + kvw_a]
            v_new = qkv_s[:, qw + kvw_a:]
            pad = ((0, 0), (0, (page - t_s) * KV_A), (0, 0))
            o_s = sb_decode(q_s, jnp.pad(k_new.reshape(n_seq, t_s * KV_A, HD_A), pad),
                            jnp.pad(v_new.reshape(n_seq, t_s * KV_A, HD_A), pad),
                            cache_k4, cache_v4, l, page_table, sb_bias[l], pages=SB_PAGES_PER_STEP)
            o_s = _rows_to_heads(o_s, t_s)
            hp = proj_res(op, w_o_a, l, g_post_mix[l], hp, TM_PROJ_RES)
            hs = proj_res(o_s, w_o_a, l, g_post_mix[l], hs, ms)
            ka_s.append(k_new.reshape(n_seq, t_s, KV_A, HD_A))
            va_s.append(v_new.reshape(n_seq, t_s, KV_A, HD_A))
        else:
            j = l - n_a
            if j == 0:
                kv_p = norm_matmul(hp, g_kv, w_kv_b3, 0, TM_PROMPT, 2 * kvw_b)
                kv_s = norm_matmul(hs, g_kv, w_kv_b3, 0, ms, 2 * kvw_b)
                kb_s = jnp.concatenate([cache_k_b, kv_s[:, :kvw_b].reshape(n_seq, t_s, KV_B, HD_B)], axis=1)
                vb_s = jnp.concatenate([cache_v_b, kv_s[:, kvw_b:].reshape(n_seq, t_s, KV_B, HD_B)], axis=1)
                padk = ((0, 0), (0, 0), (0, 2 * WINDOW - (WINDOW + t_s)), (0, 0))
                kq_s = jnp.pad(kb_s.transpose(0, 2, 1, 3), padk)
                vq_s = jnp.pad(vb_s.transpose(0, 2, 1, 3), padk)
            q_p = norm_matmul(hp, g_pre_mix[l], w_q_b, j, TM_PROMPT, TN_PROJ, BF16, HD_B ** -0.5)
            q_s = norm_matmul(hs, g_pre_mix[l], w_q_b, j, ms, TN_PROJ)
            op = win_prompt(q_p, kv_p, sinks_b[j], rel_bias, batch, seq)
            o_s = win_decode(_heads_to_rows(q_s, n_seq, t_s, KV_B, HD_B), kq_s, vq_s, sinks_b[j], rel_bias, t_s)
            o_s = _rows_to_heads(o_s, t_s)
            hp = proj_res(op, w_o_b, j, g_post_mix[l], hp, TM_PROJ_RES)
            hs = proj_res(o_s, w_o_b, j, g_post_mix[l], hs, ms)
        hp, gt_p = conv_ffn(hp, g_pre_ffn, g_post_ffn, w_gate, w_up, w_down, conv_w, conv_b, l,
                            tm=TM_PROMPT, tf=TF_PROMPT, seq_len=seq, n_split=FFN_ROW_GROUPS)
        st = state_conv[l]
        zeros = jnp.zeros((n_seq, t_s - 1, d_ff), F32)
        s1 = jnp.concatenate([st[:, 1:2], zeros], axis=1).reshape(ms, d_ff)
        s2 = jnp.concatenate([st, zeros[:, 1:]], axis=1).reshape(ms, d_ff)
        hs, gt_s = conv_ffn(hs, g_pre_ffn, g_post_ffn, w_gate, w_up, w_down, conv_w, conv_b, l,
                            tm=ms, tf=TF_SAMPLE, seq_len=t_s, n_split=1, state_rows=(s1, s2))
        tiles_per_seq = seq // TM_PROMPT
        conv_p.append(gt_p.reshape(batch, tiles_per_seq, 8, d_ff)[:, -1, 8 - (CONV_W - 1):, :])
        conv_s.append(gt_s.reshape(n_seq, t_s, d_ff)[:, t_s - (CONV_W - 1):, :])

    kvp4 = kv_p.reshape(batch, seq, 2, KV_B, HD_B)
    return (hp.reshape(batch, seq, d), hs.reshape(n_seq, t_s, d),
            kv_all[0].reshape(n_a, batch, seq, KV_A, HD_A), kv_all[1].reshape(n_a, batch, seq, KV_A, HD_A),
            jnp.stack(ka_s), jnp.stack(va_s),
            kvp4[:, seq - WINDOW:, 0], kvp4[:, seq - WINDOW:, 1],
            kb_s[:, -WINDOW:], vb_s[:, -WINDOW:],
            jnp.stack(conv_p), jnp.stack(conv_s))
```

```python
import functools
import math

import numpy as np
import jax
import jax.numpy as jnp
from jax import lax
from jax.experimental import pallas as pl
from jax.experimental.pallas import tpu as pltpu

F32 = jnp.float32
BF16 = jnp.bfloat16

EPS = 1e-6
HD_A = 128
KV_A = 4
HD_B = 64
KV_B = 4
WINDOW = 128
Q_BLOCK = 128
N_BUCKETS = 32
MAX_EXACT = N_BUCKETS // 2
MAX_DISTANCE = WINDOW
CONV_W = 3
NEG = -1e30
LOG2E = math.log2(math.e)
SB_KEY_BLOCK = 256
SB_PAGES_PER_STEP = 16
VMEM_LIMIT = 62 * 1024 * 1024
TM_PROMPT = 1024
TF_PROMPT = 512
FFN_ROW_GROUPS = 4
TF_SAMPLE = 512
TN_PROJ = 512
TM_PROJ_RES = 256


def _cparams(sem):
    return pltpu.CompilerParams(dimension_semantics=sem, vmem_limit_bytes=VMEM_LIMIT)


def _rmsnorm_rows(x, g):
    ms = jnp.mean(x * x, axis=-1, keepdims=True)
    return x * lax.rsqrt(ms + EPS) * g


def _norm_matmul_kernel(h_ref, g_ref, w_ref, o_ref, hn_ref, *, scale):
    @pl.when(pl.program_id(1) == 0)
    def _():
        hn_ref[...] = _rmsnorm_rows(h_ref[...], g_ref[...]).astype(BF16)

    y = jnp.dot(hn_ref[...], w_ref[...].astype(BF16), preferred_element_type=F32)
    o_ref[...] = (y if scale is None else y * scale).astype(o_ref.dtype)


def norm_matmul(h, g, w, layer, tm, tn, out_dtype=F32, scale=None):
    m, d = h.shape
    n = w.shape[2]
    return pl.pallas_call(
        functools.partial(_norm_matmul_kernel, scale=scale),
        grid=(m // tm, n // tn),
        in_specs=[pl.BlockSpec((tm, d), lambda i, j: (i, 0)),
                  pl.BlockSpec((1, d), lambda i, j: (0, 0)),
                  pl.BlockSpec((None, d, tn), lambda i, j: (layer, 0, j))],
        out_specs=pl.BlockSpec((tm, tn), lambda i, j: (i, j)),
        out_shape=jax.ShapeDtypeStruct((m, n), out_dtype),
        scratch_shapes=[pltpu.VMEM((tm, d), BF16)],
        compiler_params=_cparams(("arbitrary", "arbitrary")),
        name="norm_matmul",
    )(h, g.reshape(1, d), w)


def _norm_qkv_kernel(*refs, n_q_tiles, scale, aliased):
    if aliased:
        h_ref, g_ref, w_ref, _, _, q_ref, k_ref, v_ref, hn_ref = refs
    else:
        h_ref, g_ref, w_ref, q_ref, k_ref, v_ref, hn_ref = refs
    j = pl.program_id(1)
    tm = h_ref.shape[0]

    @pl.when(j == 0)
    def _():
        hn_ref[...] = _rmsnorm_rows(h_ref[...], g_ref[...]).astype(BF16)

    y = jnp.dot(hn_ref[...], w_ref[...].astype(BF16), preferred_element_type=F32)

    @pl.when(j < n_q_tiles)
    def _():
        q_ref[...] = (y * scale).astype(q_ref.dtype)

    def rows_out(ref):
        for kh in range(KV_A):
            ref[pl.ds(kh, tm, stride=KV_A), :] = y[:, kh * HD_A:(kh + 1) * HD_A]

    @pl.when(j == n_q_tiles)
    def _():
        rows_out(k_ref)

    @pl.when(j == n_q_tiles + 1)
    def _():
        rows_out(v_ref)


def norm_qkv(h, g, w, layer, tm, q_scale, kv_prev=None):
    m, d = h.shape
    nl = w.shape[0]
    tn = KV_A * HD_A
    qw = w.shape[2] - 2 * tn
    nqt = qw // tn
    aliased = kv_prev is not None
    kv_shape = jax.ShapeDtypeStruct((nl, m * KV_A, HD_A), F32)
    kv_spec = pl.BlockSpec((None, tm * KV_A, HD_A), lambda i, j: (layer, i, 0))
    in_specs = [pl.BlockSpec((tm, d), lambda i, j: (i, 0)),
                pl.BlockSpec((1, d), lambda i, j: (0, 0)),
                pl.BlockSpec((None, d, tn), lambda i, j: (layer, 0, j))]
    args = [h, g.reshape(1, d), w]
    if aliased:
        in_specs += [pl.BlockSpec(memory_space=pl.ANY)] * 2
        args += list(kv_prev)
    return pl.pallas_call(
        functools.partial(_norm_qkv_kernel, n_q_tiles=nqt, scale=q_scale, aliased=aliased),
        grid=(m // tm, nqt + 2),
        in_specs=in_specs,
        out_specs=[pl.BlockSpec((tm, tn), lambda i, j: (i, jnp.minimum(j, nqt - 1))), kv_spec, kv_spec],
        out_shape=[jax.ShapeDtypeStruct((m, qw), BF16), kv_shape, kv_shape],
        scratch_shapes=[pltpu.VMEM((tm, d), BF16)],
        input_output_aliases={3: 1, 4: 2} if aliased else {},
        compiler_params=_cparams(("arbitrary", "arbitrary")),
        name="norm_qkv",
    )(*args)


def _proj_res_kernel(o_ref, w_ref, g_ref, h_ref, out_ref, wb_ref):
    @pl.when(pl.program_id(0) == 0)
    def _():
        wb_ref[...] = w_ref[...].astype(BF16)

    y = jnp.dot(o_ref[...].astype(BF16), wb_ref[...], preferred_element_type=F32)
    out_ref[...] = h_ref[...] + _rmsnorm_rows(y, g_ref[...])


def proj_res(o, w, layer, g, h, tm):
    m, k = o.shape
    d = w.shape[2]
    return pl.pallas_call(
        _proj_res_kernel,
        grid=(m // tm,),
        in_specs=[pl.BlockSpec((tm, k), lambda i: (i, 0)),
                  pl.BlockSpec((None, k, d), lambda i: (layer, 0, 0), pipeline_mode=pl.Buffered(1)),
                  pl.BlockSpec((1, d), lambda i: (0, 0)),
                  pl.BlockSpec((tm, d), lambda i: (i, 0))],
        out_specs=pl.BlockSpec((tm, d), lambda i: (i, 0)),
        out_shape=jax.ShapeDtypeStruct((m, d), F32),
        scratch_shapes=[pltpu.VMEM((k, d), BF16)],
        compiler_params=_cparams(("arbitrary",)),
        name="proj_res",
    )(o, w, g.reshape(1, d), h)


def _gelu_tanh(x):
    c = math.sqrt(2.0 / math.pi)
    return 0.5 * x * (1.0 + jnp.tanh(c * (x + 0.044715 * (x * x * x))))


def _conv_ffn_kernel(*refs, tm, seq_len, has_state, emit_w, tail, n_split):
    h_ref, gpre_ref, gpost_ref, wg_ref, wu_ref, wd_ref, cw_ref, cb_ref = refs[:8]
    refs = refs[8:]
    if has_state:
        s1_ref, s2_ref = refs[:2]
        refs = refs[2:]
    out_ref, gt_ref = refs[:2]
    refs = refs[2:]
    if emit_w:
        wg16_ref, wu16_ref, wd16_ref = refs[:3]
        refs = refs[3:]
    hn_ref, gp_ref, carry_ref = refs
    i = pl.program_id(0)
    j = pl.program_id(1)

    @pl.when(j == 0)
    def _():
        hn_ref[...] = _rmsnorm_rows(h_ref[...], gpre_ref[...]).astype(BF16)
        out_ref[...] = jnp.zeros_like(out_ref)

    @pl.when(i == 0)
    def _():
        carry_ref[j] = jnp.zeros(carry_ref.shape[1:], F32)

    wg = wg_ref[...].astype(BF16)
    wu = wu_ref[...].astype(BF16)
    wd = wd_ref[...].astype(BF16)
    if emit_w:
        wg16_ref[...] = wg
        wu16_ref[...] = wu
        wd16_ref[...] = wd
    cw = cw_ref[...]
    cb = cb_ref[...]
    gp_ref[0:8, :] = carry_ref[j]
    th = tm // n_split
    gs, us = [], []
    for r in range(n_split):
        r0 = r * th
        hn = hn_ref[r0:r0 + th, :]
        g = jnp.dot(hn, wg, preferred_element_type=F32)
        us.append(jnp.dot(hn, wu, preferred_element_type=F32))
        gs.append(g)
        gp_ref[8 + r0:8 + r0 + th, :] = g
    carry_ref[j] = gs[-1][th - 8:, :]
    if tail == tm:
        for r in range(n_split):
            gt_ref[r * th:(r + 1) * th, :] = gs[r]
    else:
        gt_ref[...] = gs[-1][th - tail:, :]
    for r in range(n_split):
        r0 = r * th
        g, u = gs[r], us[r]
        g1 = gp_ref[7 + r0:7 + r0 + th, :]
        g2 = gp_ref[6 + r0:6 + r0 + th, :]
        pos = (i * tm + r0 + lax.broadcasted_iota(jnp.int32, (th, 1), 0)) % seq_len
        if has_state:
            g1 = jnp.where(pos >= 1, g1, s1_ref[r0:r0 + th, :])
            g2 = jnp.where(pos >= 2, g2, s2_ref[r0:r0 + th, :])
        else:
            g1 = jnp.where(pos >= 1, g1, 0.0)
            g2 = jnp.where(pos >= 2, g2, 0.0)
        gc = cb + g2 * cw[0:1, :] + g1 * cw[1:2, :] + g * cw[2:3, :]
        y = (_gelu_tanh(gc) * u).astype(BF16)
        out_ref[r0:r0 + th, :] += jnp.dot(y, wd, preferred_element_type=F32)

    @pl.when(j == pl.num_programs(1) - 1)
    def _():
        out_ref[...] = h_ref[...] + _rmsnorm_rows(out_ref[...], gpost_ref[...])


def conv_ffn(h, g_pre, g_post, wg, wu, wd, cw, cb, layer, *, tm, tf, seq_len, n_split, state_rows=None):
    m, d = h.shape
    f = wg.shape[2]
    has_state = state_rows is not None
    emit_w = wg.dtype == F32
    tail = tm if has_state else 8
    ni, nj = m // tm, f // tf
    assert not emit_w or ni == 1
    in_specs = [pl.BlockSpec((tm, d), lambda i, j: (i, 0)),
                pl.BlockSpec((None, 1, d), lambda i, j: (layer, 0, 0)),
                pl.BlockSpec((None, 1, d), lambda i, j: (layer, 0, 0)),
                pl.BlockSpec((None, d, tf), lambda i, j: (wl, 0, j)),
                pl.BlockSpec((None, d, tf), lambda i, j: (wl, 0, j)),
                pl.BlockSpec((None, tf, d), lambda i, j: (wl, j, 0)),
                pl.BlockSpec((None, CONV_W, tf), lambda i, j: (layer, 0, j)),
                pl.BlockSpec((None, 1, tf), lambda i, j: (layer, 0, j))]
    nl = g_pre.shape[0]
    wl = layer if wg.shape[0] == nl else 0
    args =[h, g_pre.reshape(nl, 1, d), g_post.reshape(nl, 1, d), wg, wu, wd, cw, cb.reshape(nl, 1, f)]
    if has_state:
        in_specs += [pl.BlockSpec((tm, tf), lambda i, j: (i, j))] * 2
        args += list(state_rows)
    out_specs = [pl.BlockSpec((tm, d), lambda i, j: (i, 0)),
                 pl.BlockSpec((None, tail, tf), lambda i, j: (i, 0, j))]
    out_shape = [jax.ShapeDtypeStruct((m, d), F32),
                 jax.ShapeDtypeStruct((ni, tail, f), F32)]
    if emit_w:
        out_specs += [pl.BlockSpec((None, d, tf), lambda i, j: (0, 0, j)),
                      pl.BlockSpec((None, d, tf), lambda i, j: (0, 0, j)),
                      pl.BlockSpec((None, tf, d), lambda i, j: (0, j, 0))]
        out_shape += [jax.ShapeDtypeStruct((1, d, f), BF16), jax.ShapeDtypeStruct((1, d, f), BF16),
                      jax.ShapeDtypeStruct((1, f, d), BF16)]
    return pl.pallas_call(
        functools.partial(_conv_ffn_kernel, tm=tm, seq_len=seq_len, has_state=has_state, emit_w=emit_w,
                          tail=tail, n_split=n_split),
        grid=(ni, nj),
        in_specs=in_specs,
        out_specs=out_specs,
        out_shape=out_shape,
        scratch_shapes=[pltpu.VMEM((tm, d), BF16),
                        pltpu.VMEM((tm + 8, tf), F32),
                        pltpu.VMEM((nj, 8, tf), F32)],
        compiler_params=_cparams(("arbitrary", "arbitrary")),
        name="conv_ffn",
    )(*args)


def _sb_logs(z, valid):
    nz = -z
    sp = jnp.log2(1.0 + jnp.exp2(jnp.minimum(z, nz)))
    log_stay = jnp.minimum(nz, 0.0) - sp
    log_beta = z + log_stay
    if valid is not None:
        log_stay = jnp.where(valid, log_stay, 0.0)
    return log_stay, log_beta


def _suffix_matrix(n, with_total):
    u = (np.arange(n)[:, None] > np.arange(n)[None, :]).astype(np.float32)
    if with_total:
        u = np.concatenate([u, np.ones((n, n), np.float32)], axis=1)
    return jnp.asarray(u, dtype=BF16)


def _sb_prompt_kernel(bias_ref, q_ref, k_ref, v_ref, u2_ref, o_ref, kb_ref, vb_ref, qs_ref, acc_ref, *, grp):
    qi = pl.program_id(1)
    tq = q_ref.shape[0]
    seq = kb_ref.shape[0]
    tk = SB_KEY_BLOCK
    rows = grp * tq
    heads = range(KV_A)

    @pl.when(qi == 0)
    def _():
        for hh in heads:
            kb_ref[:, hh * HD_A:(hh + 1) * HD_A] = k_ref[pl.ds(hh, seq, stride=KV_A), :].astype(BF16)
            vb_ref[:, hh * HD_A:(hh + 1) * HD_A] = v_ref[pl.ds(hh, seq, stride=KV_A), :].astype(BF16)

    row = lax.broadcasted_iota(jnp.int32, (rows, 1), 0)
    biases = []
    for hh in heads:
        for g in range(grp):
            c0 = (hh * grp + g) * HD_A
            qs_ref[hh, g * tq:(g + 1) * tq, :] = q_ref[:, c0:c0 + HD_A]
        bias = jnp.zeros((rows, 1), F32)
        for g in range(grp):
            bias = jnp.where(row // tq == g, bias_ref[hh * grp + g] * LOG2E, bias)
        biases.append(bias)
    u2 = u2_ref[...]
    acc_ref[...] = jnp.zeros_like(acc_ref)

    def step(kb, carries, masked):
        start = pl.multiple_of(kb * tk, tk)
        valid = None
        if masked:
            q_pos = qi * tq + row % tq
            k_pos = kb * tk + lax.broadcasted_iota(jnp.int32, (rows, tk), 1)
            valid = k_pos < q_pos
        zs = [lax.dot_general(qs_ref[hh], kb_ref[pl.ds(start, tk), hh * HD_A:(hh + 1) * HD_A],
                              (((1,), (1,)), ((), ())), preferred_element_type=F32) + biases[hh] for hh in heads]
        logs = [_sb_logs(zs[hh], valid) for hh in heads]
        laters = [jnp.dot(logs[hh][0].astype(BF16), u2, preferred_element_type=F32) + carries[hh]
                  for hh in heads]
        out = []
        for hh in heads:
            a = jnp.exp2(logs[hh][1] + laters[hh])
            if masked:
                a = jnp.where(valid, a, 0.0)
            vblk = vb_ref[pl.ds(start, tk), hh * HD_A:(hh + 1) * HD_A]
            acc_ref[hh] += jnp.dot(a.astype(BF16), vblk, preferred_element_type=F32)
            out.append(carries[hh] + jnp.sum(logs[hh][0], axis=1, keepdims=True))
        return tuple(out)

    kb_diag = (qi * tq) // tk
    carries = step(kb_diag, tuple(jnp.zeros((rows, 1), F32) for _ in heads), True)
    lax.fori_loop(0, kb_diag, lambda n, c: step(kb_diag - 1 - n, c, False), carries)
    for hh in heads:
        for g in range(grp):
            c0 = (hh * grp + g) * HD_A
            o_ref[:, c0:c0 + HD_A] = acc_ref[hh, g * tq:(g + 1) * tq, :].astype(o_ref.dtype)


def sb_prompt(q, k_all, v_all, layer, sb_bias, batch, seq):
    qw = q.shape[1]
    grp = qw // (KV_A * HD_A)
    nq = seq // Q_BLOCK
    kv_spec = pl.BlockSpec((None, seq * KV_A, HD_A), lambda b, i: (layer, b, 0))
    return pl.pallas_call(
        functools.partial(_sb_prompt_kernel, grp=grp),
        grid=(batch, nq),
        in_specs=[pl.BlockSpec(memory_space=pltpu.SMEM),
                  pl.BlockSpec((Q_BLOCK, qw), lambda b, i: (b * nq + i, 0)),
                  kv_spec, kv_spec,
                  pl.BlockSpec((SB_KEY_BLOCK, SB_KEY_BLOCK), lambda b, i: (0, 0))],
        out_specs=pl.BlockSpec((Q_BLOCK, qw), lambda b, i: (b * nq + i, 0)),
        out_shape=jax.ShapeDtypeStruct((batch * seq, qw), BF16),
        scratch_shapes=[pltpu.VMEM((seq, KV_A * HD_A), BF16),
                        pltpu.VMEM((seq, KV_A * HD_A), BF16),
                        pltpu.VMEM((KV_A, grp * Q_BLOCK, HD_A), BF16),
                        pltpu.VMEM((KV_A, grp * Q_BLOCK, HD_A), F32)],
        compiler_params=_cparams(("arbitrary", "arbitrary")),
        name="sb_prompt",
    )(sb_bias, q, k_all, v_all, _suffix_matrix(SB_KEY_BLOCK, False))


def _sb_decode_kernel(pt_ref, bias_ref, q_ref, kn_ref, vn_ref, *rest, pages, grp, t_s):
    k_refs = rest[:pages]
    v_refs = rest[pages:2 * pages]
    u2_ref, o_ref, qs_ref, acc_ref, carry_ref = rest[2 * pages:]
    c = pl.program_id(1)
    rk = grp * t_s
    rows = KV_A * rk
    page = u2_ref.shape[0]
    row = lax.broadcasted_iota(jnp.int32, (rows, 1), 0)
    bias = jnp.zeros((rows, 1), F32)
    for hh in range(KV_A * grp):
        bias = jnp.where(row // t_s == hh, bias_ref[hh] * LOG2E, bias)

    def head_rows(ref, k):
        return ref[pl.ds(k, page, stride=KV_A), :].astype(BF16)

    def process(krefs, vrefs, masked):
        n = len(krefs)
        zs = []
        for kr in krefs:
            zk = [lax.dot_general(qs_ref[k], head_rows(kr, k), (((1,), (1,)), ((), ())),
                                  preferred_element_type=F32) for k in range(KV_A)]
            zs.append(jnp.concatenate(zk, axis=0))
        z = jnp.concatenate(zs, axis=1) + bias
        valid = None
        if masked:
            valid = lax.broadcasted_iota(jnp.int32, (rows, n * page), 1) < row % t_s
        log_stay, log_beta = _sb_logs(z, valid)
        stay16 = log_stay.astype(BF16)
        u2 = u2_ref[...]
        carry = carry_ref[...]
        a_pages = []
        for j in range(n):
            sl = slice(j * page, (j + 1) * page)
            r = jnp.dot(stay16[:, sl], u2, preferred_element_type=F32)
            a = jnp.exp2(log_beta[:, sl] + r[:, :page] + carry)
            if masked:
                a = jnp.where(valid[:, sl], a, 0.0)
            a_pages.append(a.astype(BF16))
            carry = carry + r[:, page:]
        carry_ref[...] = carry
        for k in range(KV_A):
            acc = acc_ref[k]
            for j in range(n):
                acc = acc + jnp.dot(a_pages[j][k * rk:(k + 1) * rk, :], head_rows(vrefs[j], k),
                                    preferred_element_type=F32)
            acc_ref[k] = acc

    @pl.when(c == 0)
    def _():
        qs_ref[...] = (q_ref[...] * ((HD_A ** -0.5) * LOG2E)).astype(BF16)
        acc_ref[...] = jnp.zeros_like(acc_ref)
        carry_ref[...] = jnp.zeros_like(carry_ref)
        process([kn_ref], [vn_ref], True)

    process(k_refs, v_refs, False)

    @pl.when(c == pl.num_programs(1) - 1)
    def _():
        o_ref[...] = acc_ref[...]


def sb_decode(q, k_new, v_new, cache_k, cache_v, layer, page_table, sb_bias, *, pages):
    n_seq, _, rk, _ = q.shape
    n_pages = page_table.shape[1]
    prow = cache_k.shape[2]
    page = prow // KV_A
    grp = sb_bias.shape[0] // KV_A
    t_s = rk // grp
    n_chunks = n_pages // pages

    def page_spec(j):
        return pl.BlockSpec((None, None, prow, HD_A),
                            lambda b, c, pt: (layer, pt[b, n_pages - 1 - (c * pages + j)], 0, 0))

    blk4 = pl.BlockSpec((None, KV_A, rk, HD_A), lambda b, c, pt: (b, 0, 0, 0))
    new_spec = pl.BlockSpec((None, prow, HD_A), lambda b, c, pt: (b, 0, 0))
    grid_spec = pltpu.PrefetchScalarGridSpec(
        num_scalar_prefetch=1,
        grid=(n_seq, n_chunks),
        in_specs=([pl.BlockSpec(memory_space=pltpu.SMEM), blk4, new_spec, new_spec]
                  + [page_spec(j) for j in range(pages)] * 2
                  + [pl.BlockSpec((page, 2 * page), lambda b, c, pt: (0, 0))]),
        out_specs=blk4,
        scratch_shapes=[pltpu.VMEM((KV_A, rk, HD_A), BF16),
                        pltpu.VMEM((KV_A, rk, HD_A), F32),
                        pltpu.VMEM((KV_A * rk, page), F32)],
    )
    return pl.pallas_call(
        functools.partial(_sb_decode_kernel, pages=pages, grp=grp, t_s=t_s),
        grid_spec=grid_spec,
        out_shape=jax.ShapeDtypeStruct(q.shape, F32),
        compiler_params=_cparams(("arbitrary", "arbitrary")),
        name="sb_decode",
    )(page_table, sb_bias, q, k_new, v_new, *([cache_k] * pages), *([cache_v] * pages),
      _suffix_matrix(page, True))


def _t5_bucket_table(dist):
    d = np.maximum(dist, 0)
    large = MAX_EXACT + (np.log(np.maximum(d, 1).astype(np.float32) / np.float32(MAX_EXACT))
                         / np.float32(math.log(MAX_DISTANCE / MAX_EXACT))
                         * np.float32(N_BUCKETS - MAX_EXACT)).astype(np.int32)
    large = np.minimum(large, N_BUCKETS - 1)
    return np.where(d < MAX_EXACT, d, large).astype(np.int32)


def _win_softmax_pv(s, sink, v):
    m = jnp.maximum(jnp.max(s, axis=-1, keepdims=True), sink)
    p = jnp.exp(s - m)
    denom = jnp.sum(p, axis=-1, keepdims=True) + jnp.exp(sink - m)
    o = jnp.dot(p.astype(BF16), v, preferred_element_type=F32)
    return o / denom


def _win_prompt_kernel(rb_ref, sink_ref, q_ref, kvp_ref, kvc_ref, bkt_ref, o_ref, tbl_ref, sinkcol_ref, *, n_heads):
    b = pl.program_id(0)
    qi = pl.program_id(1)
    tq = q_ref.shape[0]
    grp = n_heads // KV_B
    kvw = KV_B * HD_B

    @pl.when((b == 0) & (qi == 0))
    def _():
        t_idx = lax.broadcasted_iota(jnp.int32, (tq, 2 * WINDOW), 0)
        s_idx = lax.broadcasted_iota(jnp.int32, (tq, 2 * WINDOW), 1)
        dist = t_idx + WINDOW - s_idx
        valid = (dist >= 0) & (dist < WINDOW)
        bucket = bkt_ref[...]

        def body(h, _):
            acc = jnp.zeros(bucket.shape, F32)
            for bb in range(N_BUCKETS):
                acc = jnp.where(bucket == bb, rb_ref[bb, h], acc)
            r0 = pl.multiple_of((h % grp) * tq, tq)
            tbl_ref[0, h // grp, pl.ds(r0, tq), :] = jnp.where(valid, acc, NEG)
            tbl_ref[1, h // grp, pl.ds(r0, tq), :] = jnp.where(valid & (s_idx >= WINDOW), acc, NEG)
            sinkcol_ref[h // grp, pl.ds(r0, tq), :] = jnp.full((tq, 1), sink_ref[h], F32)
            return 0
        lax.fori_loop(0, n_heads, body, 0)

    variant = (qi == 0).astype(jnp.int32)
    kv = jnp.concatenate([kvp_ref[...], kvc_ref[...]], axis=0).astype(BF16)
    kvs = range(KV_B)
    qgs = [jnp.concatenate([q_ref[:, (k * grp + g) * HD_B:(k * grp + g + 1) * HD_B] for g in range(grp)], axis=0)
           for k in kvs]
    ss = [lax.dot_general(qgs[k], kv[:, k * HD_B:(k + 1) * HD_B], (((1,), (1,)), ((), ())),
                          preferred_element_type=F32) + tbl_ref[variant, k] for k in kvs]
    ms = [jnp.maximum(jnp.max(ss[k], axis=-1, keepdims=True), sinkcol_ref[k]) for k in kvs]
    ps = [jnp.exp(ss[k] - ms[k]) for k in kvs]
    dens = [jnp.sum(ps[k], axis=-1, keepdims=True) + jnp.exp(sinkcol_ref[k] - ms[k]) for k in kvs]
    os_ = [jnp.dot(ps[k].astype(BF16), kv[:, kvw + k * HD_B:kvw + (k + 1) * HD_B], preferred_element_type=F32)
           / dens[k] for k in kvs]
    for k in kvs:
        for g in range(grp):
            h = k * grp + g
            o_ref[:, h * HD_B:(h + 1) * HD_B] = os_[k][g * tq:(g + 1) * tq, :].astype(o_ref.dtype)


def win_prompt(q, kv, sinks, rel_bias, batch, seq):
    n_heads = q.shape[1] // HD_B
    nq = seq // WINDOW
    t = np.arange(WINDOW)[:, None]
    s = np.arange(2 * WINDOW)[None, :]
    bucket = jnp.asarray(_t5_bucket_table(t + WINDOW - s))
    return pl.pallas_call(
        functools.partial(_win_prompt_kernel, n_heads=n_heads),
        grid=(batch, nq),
        in_specs=[pl.BlockSpec(memory_space=pltpu.SMEM),
                  pl.BlockSpec(memory_space=pltpu.SMEM),
                  pl.BlockSpec((WINDOW, q.shape[1]), lambda b, i: (b * nq + i, 0)),
                  pl.BlockSpec((WINDOW, kv.shape[1]), lambda b, i: (jnp.maximum(b * nq + i - 1, 0), 0)),
                  pl.BlockSpec((WINDOW, kv.shape[1]), lambda b, i: (b * nq + i, 0)),
                  pl.BlockSpec((WINDOW, 2 * WINDOW), lambda b, i: (0, 0))],
        out_specs=pl.BlockSpec((WINDOW, q.shape[1]), lambda b, i: (b * nq + i, 0)),
        out_shape=jax.ShapeDtypeStruct(q.shape, BF16),
        scratch_shapes=[pltpu.VMEM((2, KV_B, (n_heads // KV_B) * WINDOW, 2 * WINDOW), F32),
                        pltpu.VMEM((KV_B, (n_heads // KV_B) * WINDOW, 1), F32)],
        compiler_params=_cparams(("arbitrary", "arbitrary")),
        name="win_prompt",
    )(rel_bias, sinks, q, kv, kv, bucket)


def _win_decode_kernel(rb_ref, sink_ref, q_ref, k_ref, v_ref, bkt_ref, o_ref, tbl_ref, *, grp, t_s):
    b = pl.program_id(0)
    rk, nk = bkt_ref.shape
    t_idx = lax.broadcasted_iota(jnp.int32, (rk, nk), 0) % t_s
    j_idx = lax.broadcasted_iota(jnp.int32, (rk, nk), 1)

    @pl.when(b == 0)
    def _():
        dist = t_idx + WINDOW - j_idx
        valid = (dist >= 0) & (dist < WINDOW)
        bucket = bkt_ref[...]
        row_g = lax.broadcasted_iota(jnp.int32, (rk, nk), 0) // t_s
        for k in range(KV_B):
            def body(g, tbl):
                acc = jnp.zeros((rk, nk), F32)
                for bb in range(N_BUCKETS):
                    acc = jnp.where(bucket == bb, rb_ref[bb, k * grp + g], acc)
                return jnp.where(row_g == g, acc, tbl)
            tbl = lax.fori_loop(0, grp, body, jnp.zeros((rk, nk), F32))
            tbl_ref[k] = jnp.where(valid, tbl, NEG)

    row = lax.broadcasted_iota(jnp.int32, (rk, 1), 0)
    scale = HD_B ** -0.5
    for k in range(KV_B):
        sink = jnp.zeros((rk, 1), F32)
        for g in range(grp):
            sink = jnp.where(row // t_s == g, sink_ref[k * grp + g], sink)
        s = lax.dot_general(q_ref[k].astype(BF16), k_ref[k].astype(BF16), (((1,), (1,)), ((), ())),
                            preferred_element_type=F32) * scale + tbl_ref[k]
        o_ref[k] = _win_softmax_pv(s, sink, v_ref[k].astype(BF16))


def win_decode(q, k, v, sinks, rel_bias, t_s):
    n_seq, _, rk, _ = q.shape
    nk = k.shape[2]
    grp = rk // t_s
    t = (np.arange(rk) % t_s)[:, None]
    j = np.arange(nk)[None, :]
    bucket = jnp.asarray(_t5_bucket_table(t + WINDOW - j))
    blk_q = pl.BlockSpec((None, KV_B, rk, HD_B), lambda b: (b, 0, 0, 0))
    blk_k = pl.BlockSpec((None, KV_B, nk, HD_B), lambda b: (b, 0, 0, 0))
    return pl.pallas_call(
        functools.partial(_win_decode_kernel, grp=grp, t_s=t_s),
        grid=(n_seq,),
        in_specs=[pl.BlockSpec(memory_space=pltpu.SMEM),
                  pl.BlockSpec(memory_space=pltpu.SMEM),
                  blk_q, blk_k, blk_k,
                  pl.BlockSpec((rk, nk), lambda b: (0, 0))],
        out_specs=blk_q,
        out_shape=jax.ShapeDtypeStruct(q.shape, F32),
        scratch_shapes=[pltpu.VMEM((KV_B, rk, nk), F32)],
        compiler_params=_cparams(("arbitrary",)),
        name="win_decode",
    )(rel_bias, sinks, q, k, v, bucket)


def _heads_to_rows(x, n_seq, t_s, n_kv, hd):
    grp = x.shape[1] // (n_kv * hd)
    x = x.reshape(n_seq, t_s, n_kv, grp, hd).transpose(0, 2, 3, 1, 4)
    return x.reshape(n_seq, n_kv, grp * t_s, hd)


def _rows_to_heads(x, t_s):
    n_seq, n_kv, rk, hd = x.shape
    grp = rk // t_s
    x = x.reshape(n_seq, n_kv, grp, t_s, hd).transpose(0, 3, 1, 2, 4)
    return x.reshape(n_seq * t_s, n_kv * grp * hd)


def kernel(x_prompt, x_sample, cache_k_a, cache_v_a, page_table, cache_k_b, cache_v_b, state_conv,
           w_qkv_a, w_o_a, sb_bias, g_kv, w_kv_b, w_q_b, w_o_b, sinks_b, rel_bias,
           g_pre_mix, g_post_mix, g_pre_ffn, g_post_ffn, w_gate, w_up, w_down, conv_w, conv_b):
    batch, seq, d = x_prompt.shape
    n_seq, t_s, _ = x_sample.shape
    n_a = w_qkv_a.shape[0]
    depth = w_gate.shape[0]
    d_ff = w_gate.shape[2]
    page = cache_k_a.shape[2]
    h_a = w_o_a.shape[1] // HD_A
    mp, ms = batch * seq, n_seq * t_s
    qw = h_a * HD_A
    kvw_a = KV_A * HD_A
    kvw_b = KV_B * HD_B
    assert t_s >= CONV_W - 1 and seq % TM_PROMPT == 0

    cache_k4 = cache_k_a.reshape(cache_k_a.shape[0], cache_k_a.shape[1], page * KV_A, HD_A)
    cache_v4 = cache_v_a.reshape(cache_v_a.shape[0], cache_v_a.shape[1], page * KV_A, HD_A)
    w_kv_b3 = w_kv_b.reshape(1, d, 2 * kvw_b)

    hp = x_prompt.reshape(mp, d)
    hs = x_sample.reshape(ms, d)
    ka_s, va_s, conv_p, conv_s = [], [], [], []
    kv_all = kv_p = kb_s = vb_s = kq_s = vq_s = None
    for l in range(depth):
        if l < n_a:
            q_p, k_all, v_all = norm_qkv(hp, g_pre_mix[l], w_qkv_a, l, TM_PROMPT, (HD_A ** -0.5) * LOG2E, kv_all)
            kv_all = (k_all, v_all)
            qkv_s = norm_matmul(hs, g_pre_mix[l], w_qkv_a, l, ms, TN_PROJ)
            op = sb_prompt(q_p, k_all, v_all, l, sb_bias[l], batch, seq)
            q_s = _heads_to_rows(qkv_s[:, :qw], n_seq, t_s, KV_A, HD_A)
            k_new = qkv_s[:, qw:qw + kvw_a]
            v_new = qkv_s[:, qw + kvw_a:]
            pad = ((0, 0), (0, (page - t_s) * KV_A), (0, 0))
            o_s = sb_decode(q_s, jnp.pad(k_new.reshape(n_seq, t_s * KV_A, HD_A), pad),
                            jnp.pad(v_new.reshape(n_seq, t_s * KV_A, HD_A), pad),
                            cache_k4, cache_v4, l, page_table, sb_bias[l], pages=SB_PAGES_PER_STEP)
            o_s = _rows_to_heads(o_s, t_s)
            hp = proj_res(op, w_o_a, l, g_post_mix[l], hp, TM_PROJ_RES)
            hs = proj_res(o_s, w_o_a, l, g_post_mix[l], hs, ms)
            ka_s.append(k_new.reshape(n_seq, t_s, KV_A, HD_A))
            va_s.append(v_new.reshape(n_seq, t_s, KV_A, HD_A))
        else:
            j = l - n_a
            if j == 0:
                kv_p = norm_matmul(hp, g_kv, w_kv_b3, 0, TM_PROMPT, 2 * kvw_b)
                kv_s = norm_matmul(hs, g_kv, w_kv_b3, 0, ms, 2 * kvw_b)
                kb_s = jnp.concatenate([cache_k_b, kv_s[:, :kvw_b].reshape(n_seq, t_s, KV_B, HD_B)], axis=1)
                vb_s = jnp.concatenate([cache_v_b, kv_s[:, kvw_b:].reshape(n_seq, t_s, KV_B, HD_B)], axis=1)
                padk = ((0, 0), (0, 0), (0, 2 * WINDOW - (WINDOW + t_s)), (0, 0))
                kq_s = jnp.pad(kb_s.transpose(0, 2, 1, 3), padk)
                vq_s = jnp.pad(vb_s.transpose(0, 2, 1, 3), padk)
            q_p = norm_matmul(hp, g_pre_mix[l], w_q_b, j, TM_PROMPT, TN_PROJ, BF16, HD_B ** -0.5)
            q_s = norm_matmul(hs, g_pre_mix[l], w_q_b, j, ms, TN_PROJ)
            op = win_prompt(q_p, kv_p, sinks_b[j], rel_bias, batch, seq)
            o_s = win_decode(_heads_to_rows(q_s, n_seq, t_s, KV_B, HD_B), kq_s, vq_s, sinks_b[j], rel_bias, t_s)
            o_s = _rows_to_heads(o_s, t_s)
            hp = proj_res(op, w_o_b, j, g_post_mix[l], hp, TM_PROJ_RES)
            hs = proj_res(o_s, w_o_b, j, g_post_mix[l], hs, ms)
        st = state_conv[l]
        zeros = jnp.zeros((n_seq, t_s - 1, d_ff), F32)
        s1 = jnp.concatenate([st[:, 1:2], zeros], axis=1).reshape(ms, d_ff)
        s2 = jnp.concatenate([st, zeros[:, 1:]], axis=1).reshape(ms, d_ff)
        hs, gt_s, wg16, wu16, wd16 = conv_ffn(hs, g_pre_ffn, g_post_ffn, w_gate, w_up, w_down, conv_w, conv_b, l,
                                              tm=ms, tf=TF_SAMPLE, seq_len=t_s, n_split=1, state_rows=(s1, s2))
        hp, gt_p = conv_ffn(hp, g_pre_ffn, g_post_ffn, wg16, wu16, wd16, conv_w, conv_b, l,
                            tm=TM_PROMPT, tf=TF_PROMPT, seq_len=seq, n_split=FFN_ROW_GROUPS)
        tiles_per_seq = seq // TM_PROMPT
        conv_p.append(gt_p.reshape(batch, tiles_per_seq, 8, d_ff)[:, -1, 8 - (CONV_W - 1):, :])
        conv_s.append(gt_s.reshape(n_seq, t_s, d_ff)[:, t_s - (CONV_W - 1):, :])

    kvp4 = kv_p.reshape(batch, seq, 2, KV_B, HD_B)
    return (hp.reshape(batch, seq, d), hs.reshape(n_seq, t_s, d),
            kv_all[0].reshape(n_a, batch, seq, KV_A, HD_A), kv_all[1].reshape(n_a, batch, seq, KV_A, HD_A),
            jnp.stack(ka_s), jnp.stack(va_s),
            kvp4[:, seq - WINDOW:, 0], kvp4[:, seq - WINDOW:, 1],
            kb_s[:, -WINDOW:], vb_s[:, -WINDOW:],
            jnp.stack(conv_p), jnp.stack(conv_s))
```

```python
import functools
import math

import numpy as np
import jax
import jax.numpy as jnp
from jax import lax
from jax.experimental import pallas as pl
from jax.experimental.pallas import tpu as pltpu

F32 = jnp.float32
BF16 = jnp.bfloat16

EPS = 1e-6
HD_A = 128
KV_A = 4
HD_B = 64
KV_B = 4
WINDOW = 128
Q_BLOCK = 128
N_BUCKETS = 32
MAX_EXACT = N_BUCKETS // 2
MAX_DISTANCE = WINDOW
CONV_W = 3
NEG = -1e30
LOG2E = math.log2(math.e)
SB_KEY_BLOCK = 256
SB_PAGES_PER_STEP = 16
VMEM_LIMIT = 62 * 1024 * 1024
TM_PROMPT = 1024
TF_PROMPT = 512
FFN_ROW_GROUPS = 8
TF_SAMPLE = 512
TN_PROJ = 512
TM_PROJ_RES = 512


def _cparams(sem):
    return pltpu.CompilerParams(dimension_semantics=sem, vmem_limit_bytes=VMEM_LIMIT)


def _rmsnorm_rows(x, g):
    ms = jnp.mean(x * x, axis=-1, keepdims=True)
    return x * lax.rsqrt(ms + EPS) * g


def _normed_tile_times_w(h_ref, g_ref, w_ref, hn_ref, wb_ref):
    i = pl.program_id(0)
    j = pl.program_id(1)

    @pl.when(j == 0)
    def _():
        hn_ref[...] = _rmsnorm_rows(h_ref[...], g_ref[...]).astype(BF16)

    @pl.when(i == 0)
    def _():
        wb_ref[j] = w_ref[...].astype(BF16)

    return jnp.dot(hn_ref[...], wb_ref[j], preferred_element_type=F32)


def _resident_w_spec(layer, d, tn, n_tiles):
    return pl.BlockSpec((None, d, tn), lambda i, j: (layer, 0, jnp.where(i == 0, j, n_tiles - 1)))


def _norm_matmul_kernel(h_ref, g_ref, w_ref, o_ref, hn_ref, wb_ref, *, scale):
    y = _normed_tile_times_w(h_ref, g_ref, w_ref, hn_ref, wb_ref)
    o_ref[...] = (y if scale is None else y * scale).astype(o_ref.dtype)


def norm_matmul(h, g, w, layer, tm, tn, out_dtype=F32, scale=None):
    m, d = h.shape
    n = w.shape[2]
    return pl.pallas_call(
        functools.partial(_norm_matmul_kernel, scale=scale),
        grid=(m // tm, n // tn),
        in_specs=[pl.BlockSpec((tm, d), lambda i, j: (i, 0)),
                  pl.BlockSpec((1, d), lambda i, j: (0, 0)),
                  _resident_w_spec(layer, d, tn, n // tn)],
        out_specs=pl.BlockSpec((tm, tn), lambda i, j: (i, j)),
        out_shape=jax.ShapeDtypeStruct((m, n), out_dtype),
        scratch_shapes=[pltpu.VMEM((tm, d), BF16),
                        pltpu.VMEM((n // tn, d, tn), BF16)],
        compiler_params=_cparams(("arbitrary", "arbitrary")),
        name="norm_matmul",
    )(h, g.reshape(1, d), w)


def _norm_qkv_kernel(*refs, n_q_tiles, scale, aliased):
    if aliased:
        h_ref, g_ref, w_ref, _, _, q_ref, k_ref, v_ref, hn_ref, wb_ref = refs
    else:
        h_ref, g_ref, w_ref, q_ref, k_ref, v_ref, hn_ref, wb_ref = refs
    j = pl.program_id(1)
    tm = h_ref.shape[0]
    y = _normed_tile_times_w(h_ref, g_ref, w_ref, hn_ref, wb_ref)

    @pl.when(j < n_q_tiles)
    def _():
        q_ref[...] = (y * scale).astype(q_ref.dtype)

    def rows_out(ref):
        for kh in range(KV_A):
            ref[pl.ds(kh, tm, stride=KV_A), :] = y[:, kh * HD_A:(kh + 1) * HD_A]

    @pl.when(j == n_q_tiles)
    def _():
        rows_out(k_ref)

    @pl.when(j == n_q_tiles + 1)
    def _():
        rows_out(v_ref)


def norm_qkv(h, g, w, layer, tm, q_scale, kv_prev=None):
    m, d = h.shape
    nl = w.shape[0]
    tn = KV_A * HD_A
    qw = w.shape[2] - 2 * tn
    nqt = qw // tn
    aliased = kv_prev is not None
    kv_shape = jax.ShapeDtypeStruct((nl, m * KV_A, HD_A), F32)
    kv_spec = pl.BlockSpec((None, tm * KV_A, HD_A), lambda i, j: (layer, i, 0))
    in_specs = [pl.BlockSpec((tm, d), lambda i, j: (i, 0)),
                pl.BlockSpec((1, d), lambda i, j: (0, 0)),
                _resident_w_spec(layer, d, tn, nqt + 2)]
    args = [h, g.reshape(1, d), w]
    if aliased:
        in_specs += [pl.BlockSpec(memory_space=pl.ANY)] * 2
        args += list(kv_prev)
    return pl.pallas_call(
        functools.partial(_norm_qkv_kernel, n_q_tiles=nqt, scale=q_scale, aliased=aliased),
        grid=(m // tm, nqt + 2),
        in_specs=in_specs,
        out_specs=[pl.BlockSpec((tm, tn), lambda i, j: (i, jnp.minimum(j, nqt - 1))), kv_spec, kv_spec],
        out_shape=[jax.ShapeDtypeStruct((m, qw), BF16), kv_shape, kv_shape],
        scratch_shapes=[pltpu.VMEM((tm, d), BF16),
                        pltpu.VMEM((nqt + 2, d, tn), BF16)],
        input_output_aliases={3: 1, 4: 2} if aliased else {},
        compiler_params=_cparams(("arbitrary", "arbitrary")),
        name="norm_qkv",
    )(*args)


def _proj_res_kernel(o_ref, w_ref, g_ref, h_ref, out_ref, wb_ref):
    @pl.when(pl.program_id(0) == 0)
    def _():
        wb_ref[...] = w_ref[...].astype(BF16)

    y = jnp.dot(o_ref[...].astype(BF16), wb_ref[...], preferred_element_type=F32)
    out_ref[...] = h_ref[...] + _rmsnorm_rows(y, g_ref[...])


def proj_res(o, w, layer, g, h, tm):
    m, k = o.shape
    d = w.shape[2]
    return pl.pallas_call(
        _proj_res_kernel,
        grid=(m // tm,),
        in_specs=[pl.BlockSpec((tm, k), lambda i: (i, 0)),
                  pl.BlockSpec((None, k, d), lambda i: (layer, 0, 0), pipeline_mode=pl.Buffered(1)),
                  pl.BlockSpec((1, d), lambda i: (0, 0)),
                  pl.BlockSpec((tm, d), lambda i: (i, 0))],
        out_specs=pl.BlockSpec((tm, d), lambda i: (i, 0)),
        out_shape=jax.ShapeDtypeStruct((m, d), F32),
        scratch_shapes=[pltpu.VMEM((k, d), BF16)],
        compiler_params=_cparams(("arbitrary",)),
        name="proj_res",
    )(o, w, g.reshape(1, d), h)


def _gelu_tanh(x):
    c = math.sqrt(2.0 / math.pi)
    return 0.5 * x * (1.0 + jnp.tanh(c * (x + 0.044715 * (x * x * x))))


def _conv_ffn_kernel(*refs, tm, seq_len, has_state, emit_w, tail, n_split):
    h_ref, gpre_ref, gpost_ref, wg_ref, wu_ref, wd_ref, cw_ref, cb_ref = refs[:8]
    refs = refs[8:]
    if has_state:
        s1_ref, s2_ref = refs[:2]
        refs = refs[2:]
    out_ref, gt_ref = refs[:2]
    refs = refs[2:]
    if emit_w:
        wg16_ref, wu16_ref, wd16_ref = refs[:3]
        refs = refs[3:]
    hn_ref, gp_ref, carry_ref = refs
    i = pl.program_id(0)
    j = pl.program_id(1)

    @pl.when(j == 0)
    def _():
        hn_ref[...] = _rmsnorm_rows(h_ref[...], gpre_ref[...]).astype(BF16)
        out_ref[...] = jnp.zeros_like(out_ref)

    @pl.when(i == 0)
    def _():
        carry_ref[j] = jnp.zeros(carry_ref.shape[1:], F32)

    wg = wg_ref[...].astype(BF16)
    wu = wu_ref[...].astype(BF16)
    wd = wd_ref[...].astype(BF16)
    if emit_w:
        wg16_ref[...] = wg
        wu16_ref[...] = wu
        wd16_ref[...] = wd
    cw = cw_ref[...]
    cb = cb_ref[...]
    gp_ref[0:8, :] = carry_ref[j]
    th = tm // n_split
    gs, us = [], []
    for r in range(n_split):
        r0 = r * th
        hn = hn_ref[r0:r0 + th, :]
        g = jnp.dot(hn, wg, preferred_element_type=F32)
        us.append(jnp.dot(hn, wu, preferred_element_type=F32))
        gs.append(g)
        gp_ref[8 + r0:8 + r0 + th, :] = g
    carry_ref[j] = gs[-1][th - 8:, :]
    if tail == tm:
        for r in range(n_split):
            gt_ref[r * th:(r + 1) * th, :] = gs[r]
    else:
        gt_ref[...] = gs[-1][th - tail:, :]
    for r in range(n_split):
        r0 = r * th
        g, u = gs[r], us[r]
        g1 = gp_ref[7 + r0:7 + r0 + th, :]
        g2 = gp_ref[6 + r0:6 + r0 + th, :]
        pos = (i * tm + r0 + lax.broadcasted_iota(jnp.int32, (th, 1), 0)) % seq_len
        if has_state:
            g1 = jnp.where(pos >= 1, g1, s1_ref[r0:r0 + th, :])
            g2 = jnp.where(pos >= 2, g2, s2_ref[r0:r0 + th, :])
        else:
            g1 = jnp.where(pos >= 1, g1, 0.0)
            g2 = jnp.where(pos >= 2, g2, 0.0)
        gc = cb + g2 * cw[0:1, :] + g1 * cw[1:2, :] + g * cw[2:3, :]
        y = (_gelu_tanh(gc) * u).astype(BF16)
        out_ref[r0:r0 + th, :] += jnp.dot(y, wd, preferred_element_type=F32)

    @pl.when(j == pl.num_programs(1) - 1)
    def _():
        out_ref[...] = h_ref[...] + _rmsnorm_rows(out_ref[...], gpost_ref[...])


def conv_ffn(h, g_pre, g_post, wg, wu, wd, cw, cb, layer, *, tm, tf, seq_len, n_split, state_rows=None):
    m, d = h.shape
    f = wg.shape[2]
    has_state = state_rows is not None
    emit_w = wg.dtype == F32
    tail = tm if has_state else 8
    ni, nj = m // tm, f // tf
    assert not emit_w or ni == 1
    in_specs = [pl.BlockSpec((tm, d), lambda i, j: (i, 0)),
                pl.BlockSpec((None, 1, d), lambda i, j: (layer, 0, 0)),
                pl.BlockSpec((None, 1, d), lambda i, j: (layer, 0, 0)),
                pl.BlockSpec((None, d, tf), lambda i, j: (wl, 0, j)),
                pl.BlockSpec((None, d, tf), lambda i, j: (wl, 0, j)),
                pl.BlockSpec((None, tf, d), lambda i, j: (wl, j, 0)),
                pl.BlockSpec((None, CONV_W, tf), lambda i, j: (layer, 0, j)),
                pl.BlockSpec((None, 1, tf), lambda i, j: (layer, 0, j))]
    nl = g_pre.shape[0]
    wl = layer if wg.shape[0] == nl else 0
    args =[h, g_pre.reshape(nl, 1, d), g_post.reshape(nl, 1, d), wg, wu, wd, cw, cb.reshape(nl, 1, f)]
    if has_state:
        in_specs += [pl.BlockSpec((tm, tf), lambda i, j: (i, j))] * 2
        args += list(state_rows)
    out_specs = [pl.BlockSpec((tm, d), lambda i, j: (i, 0)),
                 pl.BlockSpec((None, tail, tf), lambda i, j: (i, 0, j))]
    out_shape = [jax.ShapeDtypeStruct((m, d), F32),
                 jax.ShapeDtypeStruct((ni, tail, f), F32)]
    if emit_w:
        out_specs += [pl.BlockSpec((None, d, tf), lambda i, j: (0, 0, j)),
                      pl.BlockSpec((None, d, tf), lambda i, j: (0, 0, j)),
                      pl.BlockSpec((None, tf, d), lambda i, j: (0, j, 0))]
        out_shape += [jax.ShapeDtypeStruct((1, d, f), BF16), jax.ShapeDtypeStruct((1, d, f), BF16),
                      jax.ShapeDtypeStruct((1, f, d), BF16)]
    return pl.pallas_call(
        functools.partial(_conv_ffn_kernel, tm=tm, seq_len=seq_len, has_state=has_state, emit_w=emit_w,
                          tail=tail, n_split=n_split),
        grid=(ni, nj),
        in_specs=in_specs,
        out_specs=out_specs,
        out_shape=out_shape,
        scratch_shapes=[pltpu.VMEM((tm, d), BF16),
                        pltpu.VMEM((tm + 8, tf), F32),
                        pltpu.VMEM((nj, 8, tf), F32)],
        compiler_params=_cparams(("arbitrary", "arbitrary")),
        name="conv_ffn",
    )(*args)


def _sb_logs(z, valid):
    nz = -z
    sp = jnp.log2(1.0 + jnp.exp2(jnp.minimum(z, nz)))
    log_stay = jnp.minimum(nz, 0.0) - sp
    log_beta = z + log_stay
    if valid is not None:
        log_stay = jnp.where(valid, log_stay, 0.0)
    return log_stay, log_beta


def _suffix_matrix(n, with_total):
    u = (np.arange(n)[:, None] > np.arange(n)[None, :]).astype(np.float32)
    if with_total:
        u = np.concatenate([u, np.ones((n, n), np.float32)], axis=1)
    return jnp.asarray(u, dtype=BF16)


def _sb_prompt_kernel(bias_ref, q_ref, k_ref, v_ref, u2_ref, o_ref, kb_ref, vb_ref, qs_ref, acc_ref, *, grp):
    qi = pl.program_id(1)
    tq = q_ref.shape[0]
    seq = kb_ref.shape[0]
    tk = SB_KEY_BLOCK
    rows = grp * tq
    heads = range(KV_A)

    @pl.when(qi == 0)
    def _():
        for hh in heads:
            kb_ref[:, hh * HD_A:(hh + 1) * HD_A] = k_ref[pl.ds(hh, seq, stride=KV_A), :].astype(BF16)
            vb_ref[:, hh * HD_A:(hh + 1) * HD_A] = v_ref[pl.ds(hh, seq, stride=KV_A), :].astype(BF16)

    row = lax.broadcasted_iota(jnp.int32, (rows, 1), 0)
    biases = []
    for hh in heads:
        for g in range(grp):
            c0 = (hh * grp + g) * HD_A
            qs_ref[hh, g * tq:(g + 1) * tq, :] = q_ref[:, c0:c0 + HD_A]
        bias = jnp.zeros((rows, 1), F32)
        for g in range(grp):
            bias = jnp.where(row // tq == g, bias_ref[hh * grp + g] * LOG2E, bias)
        biases.append(bias)
    u2 = u2_ref[...]
    acc_ref[...] = jnp.zeros_like(acc_ref)

    def step(kb, n_blk, carries, masked):
        start = pl.multiple_of(kb * tk, tk)
        width = n_blk * tk
        valid = None
        if masked:
            q_pos = qi * tq + row % tq
            k_pos = kb * tk + lax.broadcasted_iota(jnp.int32, (rows, width), 1)
            valid = k_pos < q_pos
        zs = [lax.dot_general(qs_ref[hh], kb_ref[pl.ds(start, width), hh * HD_A:(hh + 1) * HD_A],
                              (((1,), (1,)), ((), ())), preferred_element_type=F32) + biases[hh] for hh in heads]
        logs = [_sb_logs(zs[hh], valid) for hh in heads]
        laters, out = [], []
        for hh in heads:
            stay16 = logs[hh][0].astype(BF16)
            carry = carries[hh]
            parts = [None] * n_blk
            for blk in reversed(range(n_blk)):
                sl = slice(blk * tk, (blk + 1) * tk)
                parts[blk] = jnp.dot(stay16[:, sl], u2, preferred_element_type=F32) + carry
                carry = carry + jnp.sum(logs[hh][0][:, sl], axis=1, keepdims=True)
            laters.append(parts[0] if n_blk == 1 else jnp.concatenate(parts, axis=1))
            out.append(carry)
        for hh in heads:
            a = jnp.exp2(logs[hh][1] + laters[hh])
            if masked:
                a = jnp.where(valid, a, 0.0)
            vblk = vb_ref[pl.ds(start, width), hh * HD_A:(hh + 1) * HD_A]
            acc_ref[hh] += jnp.dot(a.astype(BF16), vblk, preferred_element_type=F32)
        return tuple(out)

    kb_diag = (qi * tq) // tk
    odd = kb_diag % 2
    carries = step(kb_diag, 1, tuple(jnp.zeros((rows, 1), F32) for _ in heads), True)
    carries = lax.fori_loop(0, odd, lambda n, c: step(kb_diag - 1, 1, c, False), carries)
    lax.fori_loop(0, kb_diag // 2, lambda n, c: step(kb_diag - odd - 2 - 2 * n, 2, c, False), carries)
    for hh in heads:
        for g in range(grp):
            c0 = (hh * grp + g) * HD_A
            o_ref[:, c0:c0 + HD_A] = acc_ref[hh, g * tq:(g + 1) * tq, :].astype(o_ref.dtype)


def sb_prompt(q, k_all, v_all, layer, sb_bias, batch, seq):
    qw = q.shape[1]
    grp = qw // (KV_A * HD_A)
    nq = seq // Q_BLOCK
    kv_spec = pl.BlockSpec((None, seq * KV_A, HD_A), lambda b, i: (layer, b, 0))
    return pl.pallas_call(
        functools.partial(_sb_prompt_kernel, grp=grp),
        grid=(batch, nq),
        in_specs=[pl.BlockSpec(memory_space=pltpu.SMEM),
                  pl.BlockSpec((Q_BLOCK, qw), lambda b, i: (b * nq + i, 0)),
                  kv_spec, kv_spec,
                  pl.BlockSpec((SB_KEY_BLOCK, SB_KEY_BLOCK), lambda b, i: (0, 0))],
        out_specs=pl.BlockSpec((Q_BLOCK, qw), lambda b, i: (b * nq + i, 0)),
        out_shape=jax.ShapeDtypeStruct((batch * seq, qw), BF16),
        scratch_shapes=[pltpu.VMEM((seq, KV_A * HD_A), BF16),
                        pltpu.VMEM((seq, KV_A * HD_A), BF16),
                        pltpu.VMEM((KV_A, grp * Q_BLOCK, HD_A), BF16),
                        pltpu.VMEM((KV_A, grp * Q_BLOCK, HD_A), F32)],
        compiler_params=_cparams(("arbitrary", "arbitrary")),
        name="sb_prompt",
    )(sb_bias, q, k_all, v_all, _suffix_matrix(SB_KEY_BLOCK, False))


def _sb_decode_kernel(pt_ref, bias_ref, q_ref, kn_ref, vn_ref, *rest, pages, grp, t_s):
    k_refs = rest[:pages]
    v_refs = rest[pages:2 * pages]
    u2_ref, o_ref, qs_ref, acc_ref, carry_ref = rest[2 * pages:]
    c = pl.program_id(1)
    rk = grp * t_s
    rows = KV_A * rk
    page = u2_ref.shape[0]
    row = lax.broadcasted_iota(jnp.int32, (rows, 1), 0)
    bias = jnp.zeros((rows, 1), F32)
    for hh in range(KV_A * grp):
        bias = jnp.where(row // t_s == hh, bias_ref[hh] * LOG2E, bias)

    def head_rows(ref, k):
        return ref[pl.ds(k, page, stride=KV_A), :].astype(BF16)

    def process(krefs, vrefs, masked):
        n = len(krefs)
        zs = []
        for kr in krefs:
            zk = [lax.dot_general(qs_ref[k], head_rows(kr, k), (((1,), (1,)), ((), ())),
                                  preferred_element_type=F32) for k in range(KV_A)]
            zs.append(jnp.concatenate(zk, axis=0))
        z = jnp.concatenate(zs, axis=1) + bias
        valid = None
        if masked:
            valid = lax.broadcasted_iota(jnp.int32, (rows, n * page), 1) < row % t_s
        log_stay, log_beta = _sb_logs(z, valid)
        stay16 = log_stay.astype(BF16)
        u2 = u2_ref[...]
        carry = carry_ref[...]
        a_pages = []
        for j in range(n):
            sl = slice(j * page, (j + 1) * page)
            r = jnp.dot(stay16[:, sl], u2, preferred_element_type=F32)
            a = jnp.exp2(log_beta[:, sl] + r[:, :page] + carry)
            if masked:
                a = jnp.where(valid[:, sl], a, 0.0)
            a_pages.append(a.astype(BF16))
            carry = carry + r[:, page:]
        carry_ref[...] = carry
        for k in range(KV_A):
            acc = acc_ref[k]
            for j in range(n):
                acc = acc + jnp.dot(a_pages[j][k * rk:(k + 1) * rk, :], head_rows(vrefs[j], k),
                                    preferred_element_type=F32)
            acc_ref[k] = acc

    @pl.when(c == 0)
    def _():
        qs_ref[...] = (q_ref[...] * ((HD_A ** -0.5) * LOG2E)).astype(BF16)
        acc_ref[...] = jnp.zeros_like(acc_ref)
        carry_ref[...] = jnp.zeros_like(carry_ref)
        process([kn_ref], [vn_ref], True)

    process(k_refs, v_refs, False)

    @pl.when(c == pl.num_programs(1) - 1)
    def _():
        o_ref[...] = acc_ref[...]


def sb_decode(q, k_new, v_new, cache_k, cache_v, layer, page_table, sb_bias, *, pages):
    n_seq, _, rk, _ = q.shape
    n_pages = page_table.shape[1]
    prow = cache_k.shape[2]
    page = prow // KV_A
    grp = sb_bias.shape[0] // KV_A
    t_s = rk // grp
    n_chunks = n_pages // pages

    def page_spec(j):
        return pl.BlockSpec((None, None, prow, HD_A),
                            lambda b, c, pt: (layer, pt[b, n_pages - 1 - (c * pages + j)], 0, 0))

    blk4 = pl.BlockSpec((None, KV_A, rk, HD_A), lambda b, c, pt: (b, 0, 0, 0))
    new_spec = pl.BlockSpec((None, prow, HD_A), lambda b, c, pt: (b, 0, 0))
    grid_spec = pltpu.PrefetchScalarGridSpec(
        num_scalar_prefetch=1,
        grid=(n_seq, n_chunks),
        in_specs=([pl.BlockSpec(memory_space=pltpu.SMEM), blk4, new_spec, new_spec]
                  + [page_spec(j) for j in range(pages)] * 2
                  + [pl.BlockSpec((page, 2 * page), lambda b, c, pt: (0, 0))]),
        out_specs=blk4,
        scratch_shapes=[pltpu.VMEM((KV_A, rk, HD_A), BF16),
                        pltpu.VMEM((KV_A, rk, HD_A), F32),
                        pltpu.VMEM((KV_A * rk, page), F32)],
    )
    return pl.pallas_call(
        functools.partial(_sb_decode_kernel, pages=pages, grp=grp, t_s=t_s),
        grid_spec=grid_spec,
        out_shape=jax.ShapeDtypeStruct(q.shape, F32),
        compiler_params=_cparams(("arbitrary", "arbitrary")),
        name="sb_decode",
    )(page_table, sb_bias, q, k_new, v_new, *([cache_k] * pages), *([cache_v] * pages),
      _suffix_matrix(page, True))


def _t5_bucket_table(dist):
    d = np.maximum(dist, 0)
    large = MAX_EXACT + (np.log(np.maximum(d, 1).astype(np.float32) / np.float32(MAX_EXACT))
                         / np.float32(math.log(MAX_DISTANCE / MAX_EXACT))
                         * np.float32(N_BUCKETS - MAX_EXACT)).astype(np.int32)
    large = np.minimum(large, N_BUCKETS - 1)
    return np.where(d < MAX_EXACT, d, large).astype(np.int32)


def _win_softmax_pv(s, sink, v):
    m = jnp.maximum(jnp.max(s, axis=-1, keepdims=True), sink)
    p = jnp.exp(s - m)
    denom = jnp.sum(p, axis=-1, keepdims=True) + jnp.exp(sink - m)
    o = jnp.dot(p.astype(BF16), v, preferred_element_type=F32)
    return o / denom


def _win_prompt_kernel(rb_ref, sink_ref, q_ref, kvp_ref, kvc_ref, bkt_ref, o_ref, tbl_ref, sinkcol_ref, *, n_heads):
    b = pl.program_id(0)
    qi = pl.program_id(1)
    tq = q_ref.shape[0]
    grp = n_heads // KV_B
    kvw = KV_B * HD_B

    @pl.when((b == 0) & (qi == 0))
    def _():
        t_idx = lax.broadcasted_iota(jnp.int32, (tq, 2 * WINDOW), 0)
        s_idx = lax.broadcasted_iota(jnp.int32, (tq, 2 * WINDOW), 1)
        dist = t_idx + WINDOW - s_idx
        valid = (dist >= 0) & (dist < WINDOW)
        bucket = bkt_ref[...]

        def body(h, _):
            acc = jnp.zeros(bucket.shape, F32)
            for bb in range(N_BUCKETS):
                acc = jnp.where(bucket == bb, rb_ref[bb, h], acc)
            r0 = pl.multiple_of((h % grp) * tq, tq)
            tbl_ref[0, h // grp, pl.ds(r0, tq), :] = jnp.where(valid, acc, NEG)
            tbl_ref[1, h // grp, pl.ds(r0, tq), :] = jnp.where(valid & (s_idx >= WINDOW), acc, NEG)
            sinkcol_ref[h // grp, pl.ds(r0, tq), :] = jnp.full((tq, 1), sink_ref[h], F32)
            return 0
        lax.fori_loop(0, n_heads, body, 0)

    variant = (qi == 0).astype(jnp.int32)
    kv = jnp.concatenate([kvp_ref[...], kvc_ref[...]], axis=0).astype(BF16)
    kvs = range(KV_B)
    qgs = [jnp.concatenate([q_ref[:, (k * grp + g) * HD_B:(k * grp + g + 1) * HD_B] for g in range(grp)], axis=0)
           for k in kvs]
    ss = [lax.dot_general(qgs[k], kv[:, k * HD_B:(k + 1) * HD_B], (((1,), (1,)), ((), ())),
                          preferred_element_type=F32) + tbl_ref[variant, k] for k in kvs]
    ms = [jnp.maximum(jnp.max(ss[k], axis=-1, keepdims=True), sinkcol_ref[k]) for k in kvs]
    ps = [jnp.exp(ss[k] - ms[k]) for k in kvs]
    dens = [jnp.sum(ps[k], axis=-1, keepdims=True) + jnp.exp(sinkcol_ref[k] - ms[k]) for k in kvs]
    os_ = [jnp.dot(ps[k].astype(BF16), kv[:, kvw + k * HD_B:kvw + (k + 1) * HD_B], preferred_element_type=F32)
           / dens[k] for k in kvs]
    for k in kvs:
        for g in range(grp):
            h = k * grp + g
            o_ref[:, h * HD_B:(h + 1) * HD_B] = os_[k][g * tq:(g + 1) * tq, :].astype(o_ref.dtype)


def win_prompt(q, kv, sinks, rel_bias, batch, seq):
    n_heads = q.shape[1] // HD_B
    nq = seq // WINDOW
    t = np.arange(WINDOW)[:, None]
    s = np.arange(2 * WINDOW)[None, :]
    bucket = jnp.asarray(_t5_bucket_table(t + WINDOW - s))
    return pl.pallas_call(
        functools.partial(_win_prompt_kernel, n_heads=n_heads),
        grid=(batch, nq),
        in_specs=[pl.BlockSpec(memory_space=pltpu.SMEM),
                  pl.BlockSpec(memory_space=pltpu.SMEM),
                  pl.BlockSpec((WINDOW, q.shape[1]), lambda b, i: (b * nq + i, 0)),
                  pl.BlockSpec((WINDOW, kv.shape[1]), lambda b, i: (jnp.maximum(b * nq + i - 1, 0), 0)),
                  pl.BlockSpec((WINDOW, kv.shape[1]), lambda b, i: (b * nq + i, 0)),
                  pl.BlockSpec((WINDOW, 2 * WINDOW), lambda b, i: (0, 0))],
        out_specs=pl.BlockSpec((WINDOW, q.shape[1]), lambda b, i: (b * nq + i, 0)),
        out_shape=jax.ShapeDtypeStruct(q.shape, BF16),
        scratch_shapes=[pltpu.VMEM((2, KV_B, (n_heads // KV_B) * WINDOW, 2 * WINDOW), F32),
                        pltpu.VMEM((KV_B, (n_heads // KV_B) * WINDOW, 1), F32)],
        compiler_params=_cparams(("arbitrary", "arbitrary")),
        name="win_prompt",
    )(rel_bias, sinks, q, kv, kv, bucket)


def _win_decode_kernel(rb_ref, sink_ref, q_ref, k_ref, v_ref, bkt_ref, o_ref, tbl_ref, *, grp, t_s):
    b = pl.program_id(0)
    rk, nk = bkt_ref.shape
    t_idx = lax.broadcasted_iota(jnp.int32, (rk, nk), 0) % t_s
    j_idx = lax.broadcasted_iota(jnp.int32, (rk, nk), 1)

    @pl.when(b == 0)
    def _():
        dist = t_idx + WINDOW - j_idx
        valid = (dist >= 0) & (dist < WINDOW)
        bucket = bkt_ref[...]
        row_g = lax.broadcasted_iota(jnp.int32, (rk, nk), 0) // t_s
        for k in range(KV_B):
            def body(g, tbl):
                acc = jnp.zeros((rk, nk), F32)
                for bb in range(N_BUCKETS):
                    acc = jnp.where(bucket == bb, rb_ref[bb, k * grp + g], acc)
                return jnp.where(row_g == g, acc, tbl)
            tbl = lax.fori_loop(0, grp, body, jnp.zeros((rk, nk), F32))
            tbl_ref[k] = jnp.where(valid, tbl, NEG)

    row = lax.broadcasted_iota(jnp.int32, (rk, 1), 0)
    scale = HD_B ** -0.5
    for k in range(KV_B):
        sink = jnp.zeros((rk, 1), F32)
        for g in range(grp):
            sink = jnp.where(row // t_s == g, sink_ref[k * grp + g], sink)
        s = lax.dot_general(q_ref[k].astype(BF16), k_ref[k].astype(BF16), (((1,), (1,)), ((), ())),
                            preferred_element_type=F32) * scale + tbl_ref[k]
        o_ref[k] = _win_softmax_pv(s, sink, v_ref[k].astype(BF16))


def win_decode(q, k, v, sinks, rel_bias, t_s):
    n_seq, _, rk, _ = q.shape
    nk = k.shape[2]
    grp = rk // t_s
    t = (np.arange(rk) % t_s)[:, None]
    j = np.arange(nk)[None, :]
    bucket = jnp.asarray(_t5_bucket_table(t + WINDOW - j))
    blk_q = pl.BlockSpec((None, KV_B, rk, HD_B), lambda b: (b, 0, 0, 0))
    blk_k = pl.BlockSpec((None, KV_B, nk, HD_B), lambda b: (b, 0, 0, 0))
    return pl.pallas_call(
        functools.partial(_win_decode_kernel, grp=grp, t_s=t_s),
        grid=(n_seq,),
        in_specs=[pl.BlockSpec(memory_space=pltpu.SMEM),
                  pl.BlockSpec(memory_space=pltpu.SMEM),
                  blk_q, blk_k, blk_k,
                  pl.BlockSpec((rk, nk), lambda b: (0, 0))],
        out_specs=blk_q,
        out_shape=jax.ShapeDtypeStruct(q.shape, F32),
        scratch_shapes=[pltpu.VMEM((KV_B, rk, nk), F32)],
        compiler_params=_cparams(("arbitrary",)),
        name="win_decode",
    )(rel_bias, sinks, q, k, v, bucket)


def _heads_to_rows(x, n_seq, t_s, n_kv, hd):
    grp = x.shape[1] // (n_kv * hd)
    x = x.reshape(n_seq, t_s, n_kv, grp, hd).transpose(0, 2, 3, 1, 4)
    return x.reshape(n_seq, n_kv, grp * t_s, hd)


def _rows_to_heads(x, t_s):
    n_seq, n_kv, rk, hd = x.shape
    grp = rk // t_s
    x = x.reshape(n_seq, n_kv, grp, t_s, hd).transpose(0, 3, 1, 2, 4)
    return x.reshape(n_seq * t_s, n_kv * grp * hd)


def kernel(x_prompt, x_sample, cache_k_a, cache_v_a, page_table, cache_k_b, cache_v_b, state_conv,
           w_qkv_a, w_o_a, sb_bias, g_kv, w_kv_b, w_q_b, w_o_b, sinks_b, rel_bias,
           g_pre_mix, g_post_mix, g_pre_ffn, g_post_ffn, w_gate, w_up, w_down, conv_w, conv_b):
    batch, seq, d = x_prompt.shape
    n_seq, t_s, _ = x_sample.shape
    n_a = w_qkv_a.shape[0]
    depth = w_gate.shape[0]
    d_ff = w_gate.shape[2]
    page = cache_k_a.shape[2]
    h_a = w_o_a.shape[1] // HD_A
    mp, ms = batch * seq, n_seq * t_s
    qw = h_a * HD_A
    kvw_a = KV_A * HD_A
    kvw_b = KV_B * HD_B
    assert t_s >= CONV_W - 1 and seq % TM_PROMPT == 0

    cache_k4 = cache_k_a.reshape(cache_k_a.shape[0], cache_k_a.shape[1], page * KV_A, HD_A)
    cache_v4 = cache_v_a.reshape(cache_v_a.shape[0], cache_v_a.shape[1], page * KV_A, HD_A)
    w_kv_b3 = w_kv_b.reshape(1, d, 2 * kvw_b)

    hp = x_prompt.reshape(mp, d)
    hs = x_sample.reshape(ms, d)
    ka_s, va_s, conv_p, conv_s = [], [], [], []
    kv_all = kv_p = kb_s = vb_s = kq_s = vq_s = None
    for l in range(depth):
        if l < n_a:
            q_p, k_all, v_all = norm_qkv(hp, g_pre_mix[l], w_qkv_a, l, TM_PROMPT, (HD_A ** -0.5) * LOG2E, kv_all)
            kv_all = (k_all, v_all)
            qkv_s = norm_matmul(hs, g_pre_mix[l], w_qkv_a, l, ms, TN_PROJ)
            op = sb_prompt(q_p, k_all, v_all, l, sb_bias[l], batch, seq)
            q_s = _heads_to_rows(qkv_s[:, :qw], n_seq, t_s, KV_A, HD_A)
            k_new = qkv_s[:, qw:qw + kvw_a]
            v_new = qkv_s[:, qw + kvw_a:]
            pad = ((0, 0), (0, (page - t_s) * KV_A), (0, 0))
            o_s = sb_decode(q_s, jnp.pad(k_new.reshape(n_seq, t_s * KV_A, HD_A), pad),
                            jnp.pad(v_new.reshape(n_seq, t_s * KV_A, HD_A), pad),
                            cache_k4, cache_v4, l, page_table, sb_bias[l], pages=SB_PAGES_PER_STEP)
            o_s = _rows_to_heads(o_s, t_s)
            hp = proj_res(op, w_o_a, l, g_post_mix[l], hp, TM_PROJ_RES)
            hs = proj_res(o_s, w_o_a, l, g_post_mix[l], hs, ms)
            ka_s.append(k_new.reshape(n_seq, t_s, KV_A, HD_A))
            va_s.append(v_new.reshape(n_seq, t_s, KV_A, HD_A))
        else:
            j = l - n_a
            if j == 0:
                kv_p = norm_matmul(hp, g_kv, w_kv_b3, 0, TM_PROMPT, 2 * kvw_b)
                kv_s = norm_matmul(hs, g_kv, w_kv_b3, 0, ms, 2 * kvw_b)
                kb_s = jnp.concatenate([cache_k_b, kv_s[:, :kvw_b].reshape(n_seq, t_s, KV_B, HD_B)], axis=1)
                vb_s = jnp.concatenate([cache_v_b, kv_s[:, kvw_b:].reshape(n_seq, t_s, KV_B, HD_B)], axis=1)
                padk = ((0, 0), (0, 0), (0, 2 * WINDOW - (WINDOW + t_s)), (0, 0))
                kq_s = jnp.pad(kb_s.transpose(0, 2, 1, 3), padk)
                vq_s = jnp.pad(vb_s.transpose(0, 2, 1, 3), padk)
            q_p = norm_matmul(hp, g_pre_mix[l], w_q_b, j, TM_PROMPT, TN_PROJ, BF16, HD_B ** -0.5)
            q_s = norm_matmul(hs, g_pre_mix[l], w_q_b, j, ms, TN_PROJ)
            op = win_prompt(q_p, kv_p, sinks_b[j], rel_bias, batch, seq)
            o_s = win_decode(_heads_to_rows(q_s, n_seq, t_s, KV_B, HD_B), kq_s, vq_s, sinks_b[j], rel_bias, t_s)
            o_s = _rows_to_heads(o_s, t_s)
            hp = proj_res(op, w_o_b, j, g_post_mix[l], hp, TM_PROJ_RES)
            hs = proj_res(o_s, w_o_b, j, g_post_mix[l], hs, ms)
        st = state_conv[l]
        zeros = jnp.zeros((n_seq, t_s - 1, d_ff), F32)
        s1 = jnp.concatenate([st[:, 1:2], zeros], axis=1).reshape(ms, d_ff)
        s2 = jnp.concatenate([st, zeros[:, 1:]], axis=1).reshape(ms, d_ff)
        hs, gt_s, wg16, wu16, wd16 = conv_ffn(hs, g_pre_ffn, g_post_ffn, w_gate, w_up, w_down, conv_w, conv_b, l,
                                              tm=ms, tf=TF_SAMPLE, seq_len=t_s, n_split=1, state_rows=(s1, s2))
        hp, gt_p = conv_ffn(hp, g_pre_ffn, g_post_ffn, wg16, wu16, wd16, conv_w, conv_b, l,
                            tm=TM_PROMPT, tf=TF_PROMPT, seq_len=seq, n_split=FFN_ROW_GROUPS)
        tiles_per_seq = seq // TM_PROMPT
        conv_p.append(gt_p.reshape(batch, tiles_per_seq, 8, d_ff)[:, -1, 8 - (CONV_W - 1):, :])
        conv_s.append(gt_s.reshape(n_seq, t_s, d_ff)[:, t_s - (CONV_W - 1):, :])

    kvp4 = kv_p.reshape(batch, seq, 2, KV_B, HD_B)
    return (hp.reshape(batch, seq, d), hs.reshape(n_seq, t_s, d),
            kv_all[0].reshape(n_a, batch, seq, KV_A, HD_A), kv_all[1].reshape(n_a, batch, seq, KV_A, HD_A),
            jnp.stack(ka_s), jnp.stack(va_s),
            kvp4[:, seq - WINDOW:, 0], kvp4[:, seq - WINDOW:, 1],
            kb_s[:, -WINDOW:], vb_s[:, -WINDOW:],
            jnp.stack(conv_p), jnp.stack(conv_s))
```

```python
import functools
import math

import numpy as np
import jax
import jax.numpy as jnp
from jax import lax
from jax.experimental import pallas as pl
from jax.experimental.pallas import tpu as pltpu

F32 = jnp.float32
BF16 = jnp.bfloat16

EPS = 1e-6
HD_A = 128
KV_A = 4
HD_B = 64
KV_B = 4
WINDOW = 128
Q_BLOCK = 128
N_BUCKETS = 32
MAX_EXACT = N_BUCKETS // 2
MAX_DISTANCE = WINDOW
CONV_W = 3
NEG = -1e30
LOG2E = math.log2(math.e)
SB_KEY_BLOCK = 256
SB_PAGES_PER_STEP = 16
VMEM_LIMIT = 62 * 1024 * 1024
TM_PROMPT = 1024
TF_PROMPT = 512
FFN_ROW_GROUPS = 8
TF_SAMPLE = 512
TN_PROJ = 512
TM_PROJ_RES = 512
WIN_DECODE_SEQS = 8


def _cparams(sem):
    return pltpu.CompilerParams(dimension_semantics=sem, vmem_limit_bytes=VMEM_LIMIT)


def _rmsnorm_rows(x, g):
    ms = jnp.mean(x * x, axis=-1, keepdims=True)
    return x * lax.rsqrt(ms + EPS) * g


def _normed_tile_times_w(h_ref, g_ref, w_ref, hn_ref, wb_ref):
    i = pl.program_id(0)
    j = pl.program_id(1)

    @pl.when(j == 0)
    def _():
        hn_ref[...] = _rmsnorm_rows(h_ref[...], g_ref[...]).astype(BF16)

    @pl.when(i == 0)
    def _():
        wb_ref[j] = w_ref[...].astype(BF16)

    return jnp.dot(hn_ref[...], wb_ref[j], preferred_element_type=F32)


def _resident_w_spec(layer, d, tn, n_tiles):
    return pl.BlockSpec((None, d, tn), lambda i, j: (layer, 0, jnp.where(i == 0, j, n_tiles - 1)))


def _norm_matmul_kernel(h_ref, g_ref, w_ref, o_ref, hn_ref, wb_ref, *, scale):
    y = _normed_tile_times_w(h_ref, g_ref, w_ref, hn_ref, wb_ref)
    o_ref[...] = (y if scale is None else y * scale).astype(o_ref.dtype)


def norm_matmul(h, g, w, layer, tm, tn, out_dtype=F32, scale=None):
    m, d = h.shape
    n = w.shape[2]
    return pl.pallas_call(
        functools.partial(_norm_matmul_kernel, scale=scale),
        grid=(m // tm, n // tn),
        in_specs=[pl.BlockSpec((tm, d), lambda i, j: (i, 0)),
                  pl.BlockSpec((1, d), lambda i, j: (0, 0)),
                  _resident_w_spec(layer, d, tn, n // tn)],
        out_specs=pl.BlockSpec((tm, tn), lambda i, j: (i, j)),
        out_shape=jax.ShapeDtypeStruct((m, n), out_dtype),
        scratch_shapes=[pltpu.VMEM((tm, d), BF16),
                        pltpu.VMEM((n // tn, d, tn), BF16)],
        compiler_params=_cparams(("arbitrary", "arbitrary")),
        name="norm_matmul",
    )(h, g.reshape(1, d), w)


def _norm_qkv_kernel(*refs, n_q_tiles, scale, aliased):
    if aliased:
        h_ref, g_ref, w_ref, _, _, q_ref, k_ref, v_ref, hn_ref, wb_ref = refs
    else:
        h_ref, g_ref, w_ref, q_ref, k_ref, v_ref, hn_ref, wb_ref = refs
    j = pl.program_id(1)
    tm = h_ref.shape[0]
    y = _normed_tile_times_w(h_ref, g_ref, w_ref, hn_ref, wb_ref)

    @pl.when(j < n_q_tiles)
    def _():
        q_ref[...] = (y * scale).astype(q_ref.dtype)

    def rows_out(ref):
        for kh in range(KV_A):
            ref[pl.ds(kh, tm, stride=KV_A), :] = y[:, kh * HD_A:(kh + 1) * HD_A]

    @pl.when(j == n_q_tiles)
    def _():
        rows_out(k_ref)

    @pl.when(j == n_q_tiles + 1)
    def _():
        rows_out(v_ref)


def norm_qkv(h, g, w, layer, tm, q_scale, kv_prev=None):
    m, d = h.shape
    nl = w.shape[0]
    tn = KV_A * HD_A
    qw = w.shape[2] - 2 * tn
    nqt = qw // tn
    aliased = kv_prev is not None
    kv_shape = jax.ShapeDtypeStruct((nl, m * KV_A, HD_A), F32)
    kv_spec = pl.BlockSpec((None, tm * KV_A, HD_A), lambda i, j: (layer, i, 0))
    in_specs = [pl.BlockSpec((tm, d), lambda i, j: (i, 0)),
                pl.BlockSpec((1, d), lambda i, j: (0, 0)),
                _resident_w_spec(layer, d, tn, nqt + 2)]
    args = [h, g.reshape(1, d), w]
    if aliased:
        in_specs += [pl.BlockSpec(memory_space=pl.ANY)] * 2
        args += list(kv_prev)
    return pl.pallas_call(
        functools.partial(_norm_qkv_kernel, n_q_tiles=nqt, scale=q_scale, aliased=aliased),
        grid=(m // tm, nqt + 2),
        in_specs=in_specs,
        out_specs=[pl.BlockSpec((tm, tn), lambda i, j: (i, jnp.minimum(j, nqt - 1))), kv_spec, kv_spec],
        out_shape=[jax.ShapeDtypeStruct((m, qw), BF16), kv_shape, kv_shape],
        scratch_shapes=[pltpu.VMEM((tm, d), BF16),
                        pltpu.VMEM((nqt + 2, d, tn), BF16)],
        input_output_aliases={3: 1, 4: 2} if aliased else {},
        compiler_params=_cparams(("arbitrary", "arbitrary")),
        name="norm_qkv",
    )(*args)


def _proj_res_kernel(o_ref, w_ref, g_ref, h_ref, out_ref, wb_ref):
    @pl.when(pl.program_id(0) == 0)
    def _():
        wb_ref[...] = w_ref[...].astype(BF16)

    y = jnp.dot(o_ref[...].astype(BF16), wb_ref[...], preferred_element_type=F32)
    out_ref[...] = h_ref[...] + _rmsnorm_rows(y, g_ref[...])


def proj_res(o, w, layer, g, h, tm):
    m, k = o.shape
    d = w.shape[2]
    return pl.pallas_call(
        _proj_res_kernel,
        grid=(m // tm,),
        in_specs=[pl.BlockSpec((tm, k), lambda i: (i, 0)),
                  pl.BlockSpec((None, k, d), lambda i: (layer, 0, 0), pipeline_mode=pl.Buffered(1)),
                  pl.BlockSpec((1, d), lambda i: (0, 0)),
                  pl.BlockSpec((tm, d), lambda i: (i, 0))],
        out_specs=pl.BlockSpec((tm, d), lambda i: (i, 0)),
        out_shape=jax.ShapeDtypeStruct((m, d), F32),
        scratch_shapes=[pltpu.VMEM((k, d), BF16)],
        compiler_params=_cparams(("arbitrary",)),
        name="proj_res",
    )(o, w, g.reshape(1, d), h)


def _gelu_tanh(x):
    c = math.sqrt(2.0 / math.pi)
    return 0.5 * x * (1.0 + jnp.tanh(c * (x + 0.044715 * (x * x * x))))


def _conv_ffn_kernel(*refs, tm, seq_len, has_state, emit_w, tail, n_split):
    h_ref, gpre_ref, gpost_ref, wg_ref, wu_ref, wd_ref, cw_ref, cb_ref = refs[:8]
    refs = refs[8:]
    if has_state:
        s1_ref, s2_ref = refs[:2]
        refs = refs[2:]
    out_ref, gt_ref = refs[:2]
    refs = refs[2:]
    if emit_w:
        wg16_ref, wu16_ref, wd16_ref = refs[:3]
        refs = refs[3:]
    hn_ref, gp_ref, carry_ref = refs
    i = pl.program_id(0)
    j = pl.program_id(1)

    @pl.when(j == 0)
    def _():
        hn_ref[...] = _rmsnorm_rows(h_ref[...], gpre_ref[...]).astype(BF16)
        out_ref[...] = jnp.zeros_like(out_ref)

    @pl.when(i == 0)
    def _():
        carry_ref[j] = jnp.zeros(carry_ref.shape[1:], F32)

    wg = wg_ref[...].astype(BF16)
    wu = wu_ref[...].astype(BF16)
    wd = wd_ref[...].astype(BF16)
    if emit_w:
        wg16_ref[...] = wg
        wu16_ref[...] = wu
        wd16_ref[...] = wd
    cw = cw_ref[...]
    cb = cb_ref[...]
    gp_ref[0:8, :] = carry_ref[j]
    th = tm // n_split
    gs, us = [], []
    for r in range(n_split):
        r0 = r * th
        hn = hn_ref[r0:r0 + th, :]
        g = jnp.dot(hn, wg, preferred_element_type=F32)
        us.append(jnp.dot(hn, wu, preferred_element_type=F32))
        gs.append(g)
        gp_ref[8 + r0:8 + r0 + th, :] = g
    carry_ref[j] = gs[-1][th - 8:, :]
    if tail == tm:
        for r in range(n_split):
            gt_ref[r * th:(r + 1) * th, :] = gs[r]
    else:
        gt_ref[...] = gs[-1][th - tail:, :]
    for r in range(n_split):
        r0 = r * th
        g, u = gs[r], us[r]
        g1 = gp_ref[7 + r0:7 + r0 + th, :]
        g2 = gp_ref[6 + r0:6 + r0 + th, :]
        pos = (i * tm + r0 + lax.broadcasted_iota(jnp.int32, (th, 1), 0)) % seq_len
        if has_state:
            g1 = jnp.where(pos >= 1, g1, s1_ref[r0:r0 + th, :])
            g2 = jnp.where(pos >= 2, g2, s2_ref[r0:r0 + th, :])
        else:
            g1 = jnp.where(pos >= 1, g1, 0.0)
            g2 = jnp.where(pos >= 2, g2, 0.0)
        gc = cb + g2 * cw[0:1, :] + g1 * cw[1:2, :] + g * cw[2:3, :]
        y = (_gelu_tanh(gc) * u).astype(BF16)
        out_ref[r0:r0 + th, :] += jnp.dot(y, wd, preferred_element_type=F32)

    @pl.when(j == pl.num_programs(1) - 1)
    def _():
        out_ref[...] = h_ref[...] + _rmsnorm_rows(out_ref[...], gpost_ref[...])


def conv_ffn(h, g_pre, g_post, wg, wu, wd, cw, cb, layer, *, tm, tf, seq_len, n_split, state_rows=None):
    m, d = h.shape
    f = wg.shape[2]
    has_state = state_rows is not None
    emit_w = wg.dtype == F32
    tail = tm if has_state else 8
    ni, nj = m // tm, f // tf
    assert not emit_w or ni == 1
    in_specs = [pl.BlockSpec((tm, d), lambda i, j: (i, 0)),
                pl.BlockSpec((None, 1, d), lambda i, j: (layer, 0, 0)),
                pl.BlockSpec((None, 1, d), lambda i, j: (layer, 0, 0)),
                pl.BlockSpec((None, d, tf), lambda i, j: (wl, 0, j)),
                pl.BlockSpec((None, d, tf), lambda i, j: (wl, 0, j)),
                pl.BlockSpec((None, tf, d), lambda i, j: (wl, j, 0)),
                pl.BlockSpec((None, CONV_W, tf), lambda i, j: (layer, 0, j)),
                pl.BlockSpec((None, 1, tf), lambda i, j: (layer, 0, j))]
    nl = g_pre.shape[0]
    wl = layer if wg.shape[0] == nl else 0
    args =[h, g_pre.reshape(nl, 1, d), g_post.reshape(nl, 1, d), wg, wu, wd, cw, cb.reshape(nl, 1, f)]
    if has_state:
        in_specs += [pl.BlockSpec((tm, tf), lambda i, j: (i, j))] * 2
        args += list(state_rows)
    out_specs = [pl.BlockSpec((tm, d), lambda i, j: (i, 0)),
                 pl.BlockSpec((None, tail, tf), lambda i, j: (i, 0, j))]
    out_shape = [jax.ShapeDtypeStruct((m, d), F32),
                 jax.ShapeDtypeStruct((ni, tail, f), F32)]
    if emit_w:
        out_specs += [pl.BlockSpec((None, d, tf), lambda i, j: (0, 0, j)),
                      pl.BlockSpec((None, d, tf), lambda i, j: (0, 0, j)),
                      pl.BlockSpec((None, tf, d), lambda i, j: (0, j, 0))]
        out_shape += [jax.ShapeDtypeStruct((1, d, f), BF16), jax.ShapeDtypeStruct((1, d, f), BF16),
                      jax.ShapeDtypeStruct((1, f, d), BF16)]
    return pl.pallas_call(
        functools.partial(_conv_ffn_kernel, tm=tm, seq_len=seq_len, has_state=has_state, emit_w=emit_w,
                          tail=tail, n_split=n_split),
        grid=(ni, nj),
        in_specs=in_specs,
        out_specs=out_specs,
        out_shape=out_shape,
        scratch_shapes=[pltpu.VMEM((tm, d), BF16),
                        pltpu.VMEM((tm + 8, tf), F32),
                        pltpu.VMEM((nj, 8, tf), F32)],
        compiler_params=_cparams(("arbitrary", "arbitrary")),
        name="conv_ffn",
    )(*args)


def _sb_logs(z, valid):
    nz = -z
    sp = jnp.log2(1.0 + jnp.exp2(jnp.minimum(z, nz)))
    log_stay = jnp.minimum(nz, 0.0) - sp
    log_beta = z + log_stay
    if valid is not None:
        log_stay = jnp.where(valid, log_stay, 0.0)
    return log_stay, log_beta


def _suffix_matrix(n, with_total):
    u = (np.arange(n)[:, None] > np.arange(n)[None, :]).astype(np.float32)
    if with_total:
        u = np.concatenate([u, np.ones((n, n), np.float32)], axis=1)
    return jnp.asarray(u, dtype=BF16)


def _sb_prompt_kernel(bias_ref, q_ref, k_ref, v_ref, u2_ref, o_ref, kb_ref, vb_ref, qs_ref, acc_ref, *, grp):
    qi = pl.program_id(1)
    tq = q_ref.shape[0]
    seq = kb_ref.shape[0]
    tk = SB_KEY_BLOCK
    rows = grp * tq
    heads = range(KV_A)

    @pl.when(qi == 0)
    def _():
        for hh in heads:
            kb_ref[:, hh * HD_A:(hh + 1) * HD_A] = k_ref[pl.ds(hh, seq, stride=KV_A), :].astype(BF16)
            vb_ref[:, hh * HD_A:(hh + 1) * HD_A] = v_ref[pl.ds(hh, seq, stride=KV_A), :].astype(BF16)

    row = lax.broadcasted_iota(jnp.int32, (rows, 1), 0)
    biases = []
    for hh in heads:
        for g in range(grp):
            c0 = (hh * grp + g) * HD_A
            qs_ref[hh, g * tq:(g + 1) * tq, :] = q_ref[:, c0:c0 + HD_A]
        bias = jnp.zeros((rows, 1), F32)
        for g in range(grp):
            bias = jnp.where(row // tq == g, bias_ref[hh * grp + g] * LOG2E, bias)
        biases.append(bias)
    u2 = u2_ref[...]
    acc_ref[...] = jnp.zeros_like(acc_ref)

    def step(kb, n_blk, carries, masked):
        start = pl.multiple_of(kb * tk, tk)
        width = n_blk * tk
        valid = None
        if masked:
            q_pos = qi * tq + row % tq
            k_pos = kb * tk + lax.broadcasted_iota(jnp.int32, (rows, width), 1)
            valid = k_pos < q_pos
        zs = [lax.dot_general(qs_ref[hh], kb_ref[pl.ds(start, width), hh * HD_A:(hh + 1) * HD_A],
                              (((1,), (1,)), ((), ())), preferred_element_type=F32) + biases[hh] for hh in heads]
        logs = [_sb_logs(zs[hh], valid) for hh in heads]
        laters, out = [], []
        for hh in heads:
            stay16 = logs[hh][0].astype(BF16)
            carry = carries[hh]
            parts = [None] * n_blk
            for blk in reversed(range(n_blk)):
                sl = slice(blk * tk, (blk + 1) * tk)
                parts[blk] = jnp.dot(stay16[:, sl], u2, preferred_element_type=F32) + carry
                carry = carry + jnp.sum(logs[hh][0][:, sl], axis=1, keepdims=True)
            laters.append(parts[0] if n_blk == 1 else jnp.concatenate(parts, axis=1))
            out.append(carry)
        for hh in heads:
            a = jnp.exp2(logs[hh][1] + laters[hh])
            if masked:
                a = jnp.where(valid, a, 0.0)
            vblk = vb_ref[pl.ds(start, width), hh * HD_A:(hh + 1) * HD_A]
            acc_ref[hh] += jnp.dot(a.astype(BF16), vblk, preferred_element_type=F32)
        return tuple(out)

    kb_diag = (qi * tq) // tk
    odd = kb_diag % 2
    carries = step(kb_diag, 1, tuple(jnp.zeros((rows, 1), F32) for _ in heads), True)
    carries = lax.fori_loop(0, odd, lambda n, c: step(kb_diag - 1, 1, c, False), carries)
    lax.fori_loop(0, kb_diag // 2, lambda n, c: step(kb_diag - odd - 2 - 2 * n, 2, c, False), carries)
    for hh in heads:
        for g in range(grp):
            c0 = (hh * grp + g) * HD_A
            o_ref[:, c0:c0 + HD_A] = acc_ref[hh, g * tq:(g + 1) * tq, :].astype(o_ref.dtype)


def sb_prompt(q, k_all, v_all, layer, sb_bias, batch, seq):
    qw = q.shape[1]
    grp = qw // (KV_A * HD_A)
    nq = seq // Q_BLOCK
    kv_spec = pl.BlockSpec((None, seq * KV_A, HD_A), lambda b, i: (layer, b, 0))
    return pl.pallas_call(
        functools.partial(_sb_prompt_kernel, grp=grp),
        grid=(batch, nq),
        in_specs=[pl.BlockSpec(memory_space=pltpu.SMEM),
                  pl.BlockSpec((Q_BLOCK, qw), lambda b, i: (b * nq + i, 0)),
                  kv_spec, kv_spec,
                  pl.BlockSpec((SB_KEY_BLOCK, SB_KEY_BLOCK), lambda b, i: (0, 0))],
        out_specs=pl.BlockSpec((Q_BLOCK, qw), lambda b, i: (b * nq + i, 0)),
        out_shape=jax.ShapeDtypeStruct((batch * seq, qw), BF16),
        scratch_shapes=[pltpu.VMEM((seq, KV_A * HD_A), BF16),
                        pltpu.VMEM((seq, KV_A * HD_A), BF16),
                        pltpu.VMEM((KV_A, grp * Q_BLOCK, HD_A), BF16),
                        pltpu.VMEM((KV_A, grp * Q_BLOCK, HD_A), F32)],
        compiler_params=_cparams(("arbitrary", "arbitrary")),
        name="sb_prompt",
    )(sb_bias, q, k_all, v_all, _suffix_matrix(SB_KEY_BLOCK, False))


def _sb_decode_kernel(pt_ref, bias_ref, q_ref, kn_ref, vn_ref, *rest, pages, grp, t_s):
    k_refs = rest[:pages]
    v_refs = rest[pages:2 * pages]
    u2_ref, o_ref, qs_ref, acc_ref, carry_ref = rest[2 * pages:]
    c = pl.program_id(1)
    rk = grp * t_s
    rows = KV_A * rk
    page = u2_ref.shape[0]
    row = lax.broadcasted_iota(jnp.int32, (rows, 1), 0)
    bias = jnp.zeros((rows, 1), F32)
    for hh in range(KV_A * grp):
        bias = jnp.where(row // t_s == hh, bias_ref[hh] * LOG2E, bias)

    def head_rows(ref, k):
        return ref[pl.ds(k, page, stride=KV_A), :].astype(BF16)

    def process(krefs, vrefs, masked):
        n = len(krefs)
        zs = []
        for kr in krefs:
            zk = [lax.dot_general(qs_ref[k], head_rows(kr, k), (((1,), (1,)), ((), ())),
                                  preferred_element_type=F32) for k in range(KV_A)]
            zs.append(jnp.concatenate(zk, axis=0))
        z = jnp.concatenate(zs, axis=1) + bias
        valid = None
        if masked:
            valid = lax.broadcasted_iota(jnp.int32, (rows, n * page), 1) < row % t_s
        log_stay, log_beta = _sb_logs(z, valid)
        stay16 = log_stay.astype(BF16)
        u2 = u2_ref[...]
        carry = carry_ref[...]
        a_pages = []
        for j in range(n):
            sl = slice(j * page, (j + 1) * page)
            r = jnp.dot(stay16[:, sl], u2, preferred_element_type=F32)
            a = jnp.exp2(log_beta[:, sl] + r[:, :page] + carry)
            if masked:
                a = jnp.where(valid[:, sl], a, 0.0)
            a_pages.append(a.astype(BF16))
            carry = carry + r[:, page:]
        carry_ref[...] = carry
        for k in range(KV_A):
            acc = acc_ref[k]
            for j in range(n):
                acc = acc + jnp.dot(a_pages[j][k * rk:(k + 1) * rk, :], head_rows(vrefs[j], k),
                                    preferred_element_type=F32)
            acc_ref[k] = acc

    @pl.when(c == 0)
    def _():
        qs_ref[...] = (q_ref[...] * ((HD_A ** -0.5) * LOG2E)).astype(BF16)
        acc_ref[...] = jnp.zeros_like(acc_ref)
        carry_ref[...] = jnp.zeros_like(carry_ref)
        process([kn_ref], [vn_ref], True)

    process(k_refs, v_refs, False)

    @pl.when(c == pl.num_programs(1) - 1)
    def _():
        o_ref[...] = acc_ref[...]


def sb_decode(q, k_new, v_new, cache_k, cache_v, layer, page_table, sb_bias, *, pages):
    n_seq, _, rk, _ = q.shape
    n_pages = page_table.shape[1]
    prow = cache_k.shape[2]
    page = prow // KV_A
    grp = sb_bias.shape[0] // KV_A
    t_s = rk // grp
    n_chunks = n_pages // pages

    def page_spec(j):
        return pl.BlockSpec((None, None, prow, HD_A),
                            lambda b, c, pt: (layer, pt[b, n_pages - 1 - (c * pages + j)], 0, 0))

    blk4 = pl.BlockSpec((None, KV_A, rk, HD_A), lambda b, c, pt: (b, 0, 0, 0))
    new_spec = pl.BlockSpec((None, prow, HD_A), lambda b, c, pt: (b, 0, 0))
    grid_spec = pltpu.PrefetchScalarGridSpec(
        num_scalar_prefetch=1,
        grid=(n_seq, n_chunks),
        in_specs=([pl.BlockSpec(memory_space=pltpu.SMEM), blk4, new_spec, new_spec]
                  + [page_spec(j) for j in range(pages)] * 2
                  + [pl.BlockSpec((page, 2 * page), lambda b, c, pt: (0, 0))]),
        out_specs=blk4,
        scratch_shapes=[pltpu.VMEM((KV_A, rk, HD_A), BF16),
                        pltpu.VMEM((KV_A, rk, HD_A), F32),
                        pltpu.VMEM((KV_A * rk, page), F32)],
    )
    return pl.pallas_call(
        functools.partial(_sb_decode_kernel, pages=pages, grp=grp, t_s=t_s),
        grid_spec=grid_spec,
        out_shape=jax.ShapeDtypeStruct(q.shape, F32),
        compiler_params=_cparams(("arbitrary", "arbitrary")),
        name="sb_decode",
    )(page_table, sb_bias, q, k_new, v_new, *([cache_k] * pages), *([cache_v] * pages),
      _suffix_matrix(page, True))


def _t5_bucket_table(dist):
    d = np.maximum(dist, 0)
    large = MAX_EXACT + (np.log(np.maximum(d, 1).astype(np.float32) / np.float32(MAX_EXACT))
                         / np.float32(math.log(MAX_DISTANCE / MAX_EXACT))
                         * np.float32(N_BUCKETS - MAX_EXACT)).astype(np.int32)
    large = np.minimum(large, N_BUCKETS - 1)
    return np.where(d < MAX_EXACT, d, large).astype(np.int32)


def _win_prompt_kernel(rb_ref, sink_ref, q_ref, kvp_ref, kvc_ref, bkt_ref, o_ref, tbl_ref, sinkcol_ref, *, n_heads):
    b = pl.program_id(0)
    qi = pl.program_id(1)
    tq = q_ref.shape[0]
    grp = n_heads // KV_B
    kvw = KV_B * HD_B

    @pl.when((b == 0) & (qi == 0))
    def _():
        t_idx = lax.broadcasted_iota(jnp.int32, (tq, 2 * WINDOW), 0)
        s_idx = lax.broadcasted_iota(jnp.int32, (tq, 2 * WINDOW), 1)
        dist = t_idx + WINDOW - s_idx
        valid = (dist >= 0) & (dist < WINDOW)
        bucket = bkt_ref[...]

        def body(h, _):
            acc = jnp.zeros(bucket.shape, F32)
            for bb in range(N_BUCKETS):
                acc = jnp.where(bucket == bb, rb_ref[bb, h], acc)
            r0 = pl.multiple_of((h % grp) * tq, tq)
            tbl_ref[0, h // grp, pl.ds(r0, tq), :] = jnp.where(valid, acc, NEG)
            tbl_ref[1, h // grp, pl.ds(r0, tq), :] = jnp.where(valid & (s_idx >= WINDOW), acc, NEG)
            sinkcol_ref[h // grp, pl.ds(r0, tq), :] = jnp.full((tq, 1), sink_ref[h], F32)
            return 0
        lax.fori_loop(0, n_heads, body, 0)

    variant = (qi == 0).astype(jnp.int32)
    kv = jnp.concatenate([kvp_ref[...], kvc_ref[...]], axis=0).astype(BF16)
    kvs = range(KV_B)
    qgs = [jnp.concatenate([q_ref[:, (k * grp + g) * HD_B:(k * grp + g + 1) * HD_B] for g in range(grp)], axis=0)
           for k in kvs]
    ss = [lax.dot_general(qgs[k], kv[:, k * HD_B:(k + 1) * HD_B], (((1,), (1,)), ((), ())),
                          preferred_element_type=F32) + tbl_ref[variant, k] for k in kvs]
    ms = [jnp.maximum(jnp.max(ss[k], axis=-1, keepdims=True), sinkcol_ref[k]) for k in kvs]
    ps = [jnp.exp(ss[k] - ms[k]) for k in kvs]
    dens = [jnp.sum(ps[k], axis=-1, keepdims=True) + jnp.exp(sinkcol_ref[k] - ms[k]) for k in kvs]
    os_ = [jnp.dot(ps[k].astype(BF16), kv[:, kvw + k * HD_B:kvw + (k + 1) * HD_B], preferred_element_type=F32)
           / dens[k] for k in kvs]
    for k in kvs:
        for g in range(grp):
            h = k * grp + g
            o_ref[:, h * HD_B:(h + 1) * HD_B] = os_[k][g * tq:(g + 1) * tq, :].astype(o_ref.dtype)


def win_prompt(q, kv, sinks, rel_bias, batch, seq):
    n_heads = q.shape[1] // HD_B
    nq = seq // WINDOW
    t = np.arange(WINDOW)[:, None]
    s = np.arange(2 * WINDOW)[None, :]
    bucket = jnp.asarray(_t5_bucket_table(t + WINDOW - s))
    return pl.pallas_call(
        functools.partial(_win_prompt_kernel, n_heads=n_heads),
        grid=(batch, nq),
        in_specs=[pl.BlockSpec(memory_space=pltpu.SMEM),
                  pl.BlockSpec(memory_space=pltpu.SMEM),
                  pl.BlockSpec((WINDOW, q.shape[1]), lambda b, i: (b * nq + i, 0)),
                  pl.BlockSpec((WINDOW, kv.shape[1]), lambda b, i: (jnp.maximum(b * nq + i - 1, 0), 0)),
                  pl.BlockSpec((WINDOW, kv.shape[1]), lambda b, i: (b * nq + i, 0)),
                  pl.BlockSpec((WINDOW, 2 * WINDOW), lambda b, i: (0, 0))],
        out_specs=pl.BlockSpec((WINDOW, q.shape[1]), lambda b, i: (b * nq + i, 0)),
        out_shape=jax.ShapeDtypeStruct(q.shape, BF16),
        scratch_shapes=[pltpu.VMEM((2, KV_B, (n_heads // KV_B) * WINDOW, 2 * WINDOW), F32),
                        pltpu.VMEM((KV_B, (n_heads // KV_B) * WINDOW, 1), F32)],
        compiler_params=_cparams(("arbitrary", "arbitrary")),
        name="win_prompt",
    )(rel_bias, sinks, q, kv, kv, bucket)


def _win_decode_kernel(rb_ref, sink_ref, q_ref, k_ref, v_ref, bkt_ref, o_ref, tbl_ref, sinkcol_ref, *, grp, t_s):
    rk, nk = bkt_ref.shape
    n_sub = q_ref.shape[0]

    @pl.when(pl.program_id(0) == 0)
    def _():
        t_idx = lax.broadcasted_iota(jnp.int32, (rk, nk), 0) % t_s
        j_idx = lax.broadcasted_iota(jnp.int32, (rk, nk), 1)
        dist = t_idx + WINDOW - j_idx
        valid = (dist >= 0) & (dist < WINDOW)
        bucket = bkt_ref[...]
        row_g = lax.broadcasted_iota(jnp.int32, (rk, nk), 0) // t_s
        row = lax.broadcasted_iota(jnp.int32, (rk, 1), 0)
        for k in range(KV_B):
            def body(g, tbl):
                acc = jnp.zeros((rk, nk), F32)
                for bb in range(N_BUCKETS):
                    acc = jnp.where(bucket == bb, rb_ref[bb, k * grp + g], acc)
                return jnp.where(row_g == g, acc, tbl)
            tbl = lax.fori_loop(0, grp, body, jnp.zeros((rk, nk), F32))
            tbl_ref[k] = jnp.where(valid, tbl, NEG)
            sink = jnp.zeros((rk, 1), F32)
            for g in range(grp):
                sink = jnp.where(row // t_s == g, sink_ref[k * grp + g], sink)
            sinkcol_ref[k] = sink

    scale = HD_B ** -0.5
    items = [(s, k) for s in range(n_sub) for k in range(KV_B)]
    ss = [lax.dot_general(q_ref[s, k].astype(BF16), k_ref[s, k].astype(BF16), (((1,), (1,)), ((), ())),
                          preferred_element_type=F32) * scale + tbl_ref[k] for s, k in items]
    ms = [jnp.maximum(jnp.max(ss[n], axis=-1, keepdims=True), sinkcol_ref[k]) for n, (s, k) in enumerate(items)]
    ps = [jnp.exp(ss[n] - ms[n]) for n in range(len(items))]
    dens = [jnp.sum(ps[n], axis=-1, keepdims=True) + jnp.exp(sinkcol_ref[k] - ms[n])
            for n, (s, k) in enumerate(items)]
    for n, (s, k) in enumerate(items):
        o_ref[s, k] = jnp.dot(ps[n].astype(BF16), v_ref[s, k].astype(BF16), preferred_element_type=F32) / dens[n]


def win_decode(q, k, v, sinks, rel_bias, t_s):
    n_seq, _, rk, _ = q.shape
    nk = k.shape[2]
    grp = rk // t_s
    n_sub = WIN_DECODE_SEQS
    t = (np.arange(rk) % t_s)[:, None]
    j = np.arange(nk)[None, :]
    bucket = jnp.asarray(_t5_bucket_table(t + WINDOW - j))
    blk_q = pl.BlockSpec((n_sub, KV_B, rk, HD_B), lambda b: (b, 0, 0, 0))
    blk_k = pl.BlockSpec((n_sub, KV_B, nk, HD_B), lambda b: (b, 0, 0, 0))
    return pl.pallas_call(
        functools.partial(_win_decode_kernel, grp=grp, t_s=t_s),
        grid=(n_seq // n_sub,),
        in_specs=[pl.BlockSpec(memory_space=pltpu.SMEM),
                  pl.BlockSpec(memory_space=pltpu.SMEM),
                  blk_q, blk_k, blk_k,
                  pl.BlockSpec((rk, nk), lambda b: (0, 0))],
        out_specs=blk_q,
        out_shape=jax.ShapeDtypeStruct(q.shape, F32),
        scratch_shapes=[pltpu.VMEM((KV_B, rk, nk), F32),
                        pltpu.VMEM((KV_B, rk, 1), F32)],
        compiler_params=_cparams(("arbitrary",)),
        name="win_decode",
    )(rel_bias, sinks, q, k, v, bucket)


def _heads_to_rows(x, n_seq, t_s, n_kv, hd):
    grp = x.shape[1] // (n_kv * hd)
    x = x.reshape(n_seq, t_s, n_kv, grp, hd).transpose(0, 2, 3, 1, 4)
    return x.reshape(n_seq, n_kv, grp * t_s, hd)


def _rows_to_heads(x, t_s):
    n_seq, n_kv, rk, hd = x.shape
    grp = rk // t_s
    x = x.reshape(n_seq, n_kv, grp, t_s, hd).transpose(0, 3, 1, 2, 4)
    return x.reshape(n_seq * t_s, n_kv * grp * hd)


def kernel(x_prompt, x_sample, cache_k_a, cache_v_a, page_table, cache_k_b, cache_v_b, state_conv,
           w_qkv_a, w_o_a, sb_bias, g_kv, w_kv_b, w_q_b, w_o_b, sinks_b, rel_bias,
           g_pre_mix, g_post_mix, g_pre_ffn, g_post_ffn, w_gate, w_up, w_down, conv_w, conv_b):
    batch, seq, d = x_prompt.shape
    n_seq, t_s, _ = x_sample.shape
    n_a = w_qkv_a.shape[0]
    depth = w_gate.shape[0]
    d_ff = w_gate.shape[2]
    page = cache_k_a.shape[2]
    h_a = w_o_a.shape[1] // HD_A
    mp, ms = batch * seq, n_seq * t_s
    qw = h_a * HD_A
    kvw_a = KV_A * HD_A
    kvw_b = KV_B * HD_B
    assert t_s >= CONV_W - 1 and seq % TM_PROMPT == 0

    cache_k4 = cache_k_a.reshape(cache_k_a.shape[0], cache_k_a.shape[1], page * KV_A, HD_A)
    cache_v4 = cache_v_a.reshape(cache_v_a.shape[0], cache_v_a.shape[1], page * KV_A, HD_A)
    w_kv_b3 = w_kv_b.reshape(1, d, 2 * kvw_b)

    hp = x_prompt.reshape(mp, d)
    hs = x_sample.reshape(ms, d)
    ka_s, va_s, conv_p, conv_s = [], [], [], []
    kv_all = kv_p = kb_s = vb_s = kq_s = vq_s = None
    for l in range(depth):
        if l < n_a:
            q_p, k_all, v_all = norm_qkv(hp, g_pre_mix[l], w_qkv_a, l, TM_PROMPT, (HD_A ** -0.5) * LOG2E, kv_all)
            kv_all = (k_all, v_all)
            qkv_s = norm_matmul(hs, g_pre_mix[l], w_qkv_a, l, ms, TN_PROJ)
            op = sb_prompt(q_p, k_all, v_all, l, sb_bias[l], batch, seq)
            q_s = _heads_to_rows(qkv_s[:, :qw], n_seq, t_s, KV_A, HD_A)
            k_new = qkv_s[:, qw:qw + kvw_a]
            v_new = qkv_s[:, qw + kvw_a:]
            pad = ((0, 0), (0, (page - t_s) * KV_A), (0, 0))
            o_s = sb_decode(q_s, jnp.pad(k_new.reshape(n_seq, t_s * KV_A, HD_A), pad),
                            jnp.pad(v_new.reshape(n_seq, t_s * KV_A, HD_A), pad),
                            cache_k4, cache_v4, l, page_table, sb_bias[l], pages=SB_PAGES_PER_STEP)
            o_s = _rows_to_heads(o_s, t_s)
            hp = proj_res(op, w_o_a, l, g_post_mix[l], hp, TM_PROJ_RES)
            hs = proj_res(o_s, w_o_a, l, g_post_mix[l], hs, ms)
            ka_s.append(k_new.reshape(n_seq, t_s, KV_A, HD_A))
            va_s.append(v_new.reshape(n_seq, t_s, KV_A, HD_A))
        else:
            j = l - n_a
            if j == 0:
                kv_p = norm_matmul(hp, g_kv, w_kv_b3, 0, TM_PROMPT, 2 * kvw_b)
                kv_s = norm_matmul(hs, g_kv, w_kv_b3, 0, ms, 2 * kvw_b)
                kb_s = jnp.concatenate([cache_k_b, kv_s[:, :kvw_b].reshape(n_seq, t_s, KV_B, HD_B)], axis=1)
                vb_s = jnp.concatenate([cache_v_b, kv_s[:, kvw_b:].reshape(n_seq, t_s, KV_B, HD_B)], axis=1)
                padk = ((0, 0), (0, 0), (0, 2 * WINDOW - (WINDOW + t_s)), (0, 0))
                kq_s = jnp.pad(kb_s.transpose(0, 2, 1, 3), padk)
                vq_s = jnp.pad(vb_s.transpose(0, 2, 1, 3), padk)
            q_p = norm_matmul(hp, g_pre_mix[l], w_q_b, j, TM_PROMPT, TN_PROJ, BF16, HD_B ** -0.5)
            q_s = norm_matmul(hs, g_pre_mix[l], w_q_b, j, ms, TN_PROJ)
            op = win_prompt(q_p, kv_p, sinks_b[j], rel_bias, batch, seq)
            o_s = win_decode(_heads_to_rows(q_s, n_seq, t_s, KV_B, HD_B), kq_s, vq_s, sinks_b[j], rel_bias, t_s)
            o_s = _rows_to_heads(o_s, t_s)
            hp = proj_res(op, w_o_b, j, g_post_mix[l], hp, TM_PROJ_RES)
            hs = proj_res(o_s, w_o_b, j, g_post_mix[l], hs, ms)
        st = state_conv[l]
        zeros = jnp.zeros((n_seq, t_s - 1, d_ff), F32)
        s1 = jnp.concatenate([st[:, 1:2], zeros], axis=1).reshape(ms, d_ff)
        s2 = jnp.concatenate([st, zeros[:, 1:]], axis=1).reshape(ms, d_ff)
        hs, gt_s, wg16, wu16, wd16 = conv_ffn(hs, g_pre_ffn, g_post_ffn, w_gate, w_up, w_down, conv_w, conv_b, l,
                                              tm=ms, tf=TF_SAMPLE, seq_len=t_s, n_split=1, state_rows=(s1, s2))
        hp, gt_p = conv_ffn(hp, g_pre_ffn, g_post_ffn, wg16, wu16, wd16, conv_w, conv_b, l,
                            tm=TM_PROMPT, tf=TF_PROMPT, seq_len=seq, n_split=FFN_ROW_GROUPS)
        tiles_per_seq = seq // TM_PROMPT
        conv_p.append(gt_p.reshape(batch, tiles_per_seq, 8, d_ff)[:, -1, 8 - (CONV_W - 1):, :])
        conv_s.append(gt_s.reshape(n_seq, t_s, d_ff)[:, t_s - (CONV_W - 1):, :])

    kvp4 = kv_p.reshape(batch, seq, 2 * kvw_b)[:, seq - WINDOW:].reshape(batch, WINDOW, 2, KV_B, HD_B)
    return (hp.reshape(batch, seq, d), hs.reshape(n_seq, t_s, d),
            kv_all[0].reshape(n_a, batch, seq, KV_A, HD_A), kv_all[1].reshape(n_a, batch, seq, KV_A, HD_A),
            jnp.stack(ka_s), jnp.stack(va_s),
            kvp4[:, :, 0], kvp4[:, :, 1],
            kb_s[:, -WINDOW:], vb_s[:, -WINDOW:],
            jnp.stack(conv_p), jnp.stack(conv_s))
```

```python
import functools
import math

import numpy as np
import jax
import jax.numpy as jnp
from jax import lax
from jax.experimental import pallas as pl
from jax.experimental.pallas import tpu as pltpu

F32 = jnp.float32
BF16 = jnp.bfloat16

EPS = 1e-6
HD_A = 128
KV_A = 4
HD_B = 64
KV_B = 4
WINDOW = 128
Q_BLOCK = 128
N_BUCKETS = 32
MAX_EXACT = N_BUCKETS // 2
MAX_DISTANCE = WINDOW
CONV_W = 3
NEG = -1e30
LOG2E = math.log2(math.e)
SB_KEY_BLOCK = 256
SB_PAGES_PER_STEP = 16
VMEM_LIMIT = 62 * 1024 * 1024
TM_PROMPT = 1024
TF_PROMPT = 512
FFN_ROW_GROUPS = 8
TF_SAMPLE = 512
TN_PROJ = 512
TM_PROJ_RES = 512
PROJ_RES_GROUP_ROWS = 128
WIN_DECODE_SEQS = 8


def _cparams(sem):
    return pltpu.CompilerParams(dimension_semantics=sem, vmem_limit_bytes=VMEM_LIMIT)


def _rmsnorm_rows(x, g):
    ms = jnp.mean(x * x, axis=-1, keepdims=True)
    return x * lax.rsqrt(ms + EPS) * g


def _normed_tile_times_w(h_ref, g_ref, w_ref, hn_ref, wb_ref):
    i = pl.program_id(0)
    j = pl.program_id(1)

    @pl.when(j == 0)
    def _():
        hn_ref[...] = _rmsnorm_rows(h_ref[...], g_ref[...]).astype(BF16)

    @pl.when(i == 0)
    def _():
        wb_ref[j] = w_ref[...].astype(BF16)

    return jnp.dot(hn_ref[...], wb_ref[j], preferred_element_type=F32)


def _resident_w_spec(layer, d, tn, n_tiles):
    return pl.BlockSpec((None, d, tn), lambda i, j: (layer, 0, jnp.where(i == 0, j, n_tiles - 1)))


def _norm_matmul_kernel(h_ref, g_ref, w_ref, o_ref, hn_ref, wb_ref, *, scale):
    y = _normed_tile_times_w(h_ref, g_ref, w_ref, hn_ref, wb_ref)
    o_ref[...] = (y if scale is None else y * scale).astype(o_ref.dtype)


def norm_matmul(h, g, w, layer, tm, tn, out_dtype=F32, scale=None):
    m, d = h.shape
    n = w.shape[2]
    return pl.pallas_call(
        functools.partial(_norm_matmul_kernel, scale=scale),
        grid=(m // tm, n // tn),
        in_specs=[pl.BlockSpec((tm, d), lambda i, j: (i, 0)),
                  pl.BlockSpec((1, d), lambda i, j: (0, 0)),
                  _resident_w_spec(layer, d, tn, n // tn)],
        out_specs=pl.BlockSpec((tm, tn), lambda i, j: (i, j)),
        out_shape=jax.ShapeDtypeStruct((m, n), out_dtype),
        scratch_shapes=[pltpu.VMEM((tm, d), BF16),
                        pltpu.VMEM((n // tn, d, tn), BF16)],
        compiler_params=_cparams(("arbitrary", "arbitrary")),
        name="norm_matmul",
    )(h, g.reshape(1, d), w)


def _norm_qkv_kernel(*refs, n_q_tiles, scale, aliased):
    if aliased:
        h_ref, g_ref, w_ref, _, _, q_ref, k_ref, v_ref, hn_ref, wb_ref = refs
    else:
        h_ref, g_ref, w_ref, q_ref, k_ref, v_ref, hn_ref, wb_ref = refs
    j = pl.program_id(1)
    tm = h_ref.shape[0]
    y = _normed_tile_times_w(h_ref, g_ref, w_ref, hn_ref, wb_ref)

    @pl.when(j < n_q_tiles)
    def _():
        q_ref[...] = (y * scale).astype(q_ref.dtype)

    def rows_out(ref):
        for kh in range(KV_A):
            ref[pl.ds(kh, tm, stride=KV_A), :] = y[:, kh * HD_A:(kh + 1) * HD_A]

    @pl.when(j == n_q_tiles)
    def _():
        rows_out(k_ref)

    @pl.when(j == n_q_tiles + 1)
    def _():
        rows_out(v_ref)


def norm_qkv(h, g, w, layer, tm, q_scale, kv_prev=None):
    m, d = h.shape
    nl = w.shape[0]
    tn = KV_A * HD_A
    qw = w.shape[2] - 2 * tn
    nqt = qw // tn
    aliased = kv_prev is not None
    kv_shape = jax.ShapeDtypeStruct((nl, m * KV_A, HD_A), F32)
    kv_spec = pl.BlockSpec((None, tm * KV_A, HD_A), lambda i, j: (layer, i, 0))
    in_specs = [pl.BlockSpec((tm, d), lambda i, j: (i, 0)),
                pl.BlockSpec((1, d), lambda i, j: (0, 0)),
                _resident_w_spec(layer, d, tn, nqt + 2)]
    args = [h, g.reshape(1, d), w]
    if aliased:
        in_specs += [pl.BlockSpec(memory_space=pl.ANY)] * 2
        args += list(kv_prev)
    return pl.pallas_call(
        functools.partial(_norm_qkv_kernel, n_q_tiles=nqt, scale=q_scale, aliased=aliased),
        grid=(m // tm, nqt + 2),
        in_specs=in_specs,
        out_specs=[pl.BlockSpec((tm, tn), lambda i, j: (i, jnp.minimum(j, nqt - 1))), kv_spec, kv_spec],
        out_shape=[jax.ShapeDtypeStruct((m, qw), BF16), kv_shape, kv_shape],
        scratch_shapes=[pltpu.VMEM((tm, d), BF16),
                        pltpu.VMEM((nqt + 2, d, tn), BF16)],
        input_output_aliases={3: 1, 4: 2} if aliased else {},
        compiler_params=_cparams(("arbitrary", "arbitrary")),
        name="norm_qkv",
    )(*args)


def _proj_res_kernel(o_ref, w_ref, g_ref, h_ref, out_ref, wb_ref, *, n_split):
    @pl.when(pl.program_id(0) == 0)
    def _():
        wb_ref[...] = w_ref[...].astype(BF16)

    tm = o_ref.shape[0]
    th = tm // n_split
    ys = [jnp.dot(o_ref[r * th:(r + 1) * th, :].astype(BF16), wb_ref[...], preferred_element_type=F32)
          for r in range(n_split)]
    for r in range(n_split):
        out_ref[r * th:(r + 1) * th, :] = h_ref[r * th:(r + 1) * th, :] + _rmsnorm_rows(ys[r], g_ref[...])


def proj_res(o, w, layer, g, h, tm):
    m, k = o.shape
    d = w.shape[2]
    return pl.pallas_call(
        functools.partial(_proj_res_kernel, n_split=max(1, tm // PROJ_RES_GROUP_ROWS)),
        grid=(m // tm,),
        in_specs=[pl.BlockSpec((tm, k), lambda i: (i, 0)),
                  pl.BlockSpec((None, k, d), lambda i: (layer, 0, 0), pipeline_mode=pl.Buffered(1)),
                  pl.BlockSpec((1, d), lambda i: (0, 0)),
                  pl.BlockSpec((tm, d), lambda i: (i, 0))],
        out_specs=pl.BlockSpec((tm, d), lambda i: (i, 0)),
        out_shape=jax.ShapeDtypeStruct((m, d), F32),
        scratch_shapes=[pltpu.VMEM((k, d), BF16)],
        compiler_params=_cparams(("arbitrary",)),
        name="proj_res",
    )(o, w, g.reshape(1, d), h)


def _gelu_tanh(x):
    c = math.sqrt(2.0 / math.pi)
    return 0.5 * x * (1.0 + jnp.tanh(c * (x + 0.044715 * (x * x * x))))


def _conv_ffn_kernel(*refs, tm, seq_len, has_state, emit_w, tail, n_split):
    h_ref, gpre_ref, gpost_ref, wg_ref, wu_ref, wd_ref, cw_ref, cb_ref = refs[:8]
    refs = refs[8:]
    if has_state:
        s1_ref, s2_ref = refs[:2]
        refs = refs[2:]
    out_ref, gt_ref = refs[:2]
    refs = refs[2:]
    if emit_w:
        wg16_ref, wu16_ref, wd16_ref = refs[:3]
        refs = refs[3:]
    hn_ref, gp_ref, carry_ref = refs
    i = pl.program_id(0)
    j = pl.program_id(1)

    @pl.when(i == 0)
    def _():
        carry_ref[j] = jnp.zeros(carry_ref.shape[1:], F32)

    th = tm // n_split

    def step(first, last):
        wg = wg_ref[...].astype(BF16)
        wu = wu_ref[...].astype(BF16)
        wd = wd_ref[...].astype(BF16)
        if emit_w:
            wg16_ref[...] = wg
            wu16_ref[...] = wu
            wd16_ref[...] = wd
        cw = cw_ref[...]
        cb = cb_ref[...]
        gp_ref[0:8, :] = carry_ref[j]
        gs, us = [], []
        for r in range(n_split):
            r0 = r * th
            if first:
                hn = _rmsnorm_rows(h_ref[r0:r0 + th, :], gpre_ref[...]).astype(BF16)
                hn_ref[r0:r0 + th, :] = hn
            else:
                hn = hn_ref[r0:r0 + th, :]
            g = jnp.dot(hn, wg, preferred_element_type=F32)
            us.append(jnp.dot(hn, wu, preferred_element_type=F32))
            gs.append(g)
            gp_ref[8 + r0:8 + r0 + th, :] = g
        carry_ref[j] = gs[-1][th - 8:, :]
        if tail == tm:
            for r in range(n_split):
                gt_ref[r * th:(r + 1) * th, :] = gs[r]
        else:
            gt_ref[...] = gs[-1][th - tail:, :]
        for r in range(n_split):
            r0 = r * th
            g, u = gs[r], us[r]
            g1 = gp_ref[7 + r0:7 + r0 + th, :]
            g2 = gp_ref[6 + r0:6 + r0 + th, :]
            pos = (i * tm + r0 + lax.broadcasted_iota(jnp.int32, (th, 1), 0)) % seq_len
            if has_state:
                g1 = jnp.where(pos >= 1, g1, s1_ref[r0:r0 + th, :])
                g2 = jnp.where(pos >= 2, g2, s2_ref[r0:r0 + th, :])
            else:
                g1 = jnp.where(pos >= 1, g1, 0.0)
                g2 = jnp.where(pos >= 2, g2, 0.0)
            gc = cb + g2 * cw[0:1, :] + g1 * cw[1:2, :] + g * cw[2:3, :]
            y = (_gelu_tanh(gc) * u).astype(BF16)
            acc = jnp.dot(y, wd, preferred_element_type=F32)
            if not first:
                acc = out_ref[r0:r0 + th, :] + acc
            if last:
                acc = h_ref[r0:r0 + th, :] + _rmsnorm_rows(acc, gpost_ref[...])
            out_ref[r0:r0 + th, :] = acc

    nj = pl.num_programs(1)
    pl.when(j == 0)(lambda: step(True, False))
    pl.when((j > 0) & (j < nj - 1))(lambda: step(False, False))
    pl.when(j == nj - 1)(lambda: step(False, True))


def conv_ffn(h, g_pre, g_post, wg, wu, wd, cw, cb, layer, *, tm, tf, seq_len, n_split, state_rows=None):
    m, d = h.shape
    f = wg.shape[2]
    has_state = state_rows is not None
    emit_w = wg.dtype == F32
    tail = tm if has_state else 8
    ni, nj = m // tm, f // tf
    assert (not emit_w or ni == 1) and nj >= 2
    in_specs = [pl.BlockSpec((tm, d), lambda i, j: (i, 0)),
                pl.BlockSpec((None, 1, d), lambda i, j: (layer, 0, 0)),
                pl.BlockSpec((None, 1, d), lambda i, j: (layer, 0, 0)),
                pl.BlockSpec((None, d, tf), lambda i, j: (wl, 0, j)),
                pl.BlockSpec((None, d, tf), lambda i, j: (wl, 0, j)),
                pl.BlockSpec((None, tf, d), lambda i, j: (wl, j, 0)),
                pl.BlockSpec((None, CONV_W, tf), lambda i, j: (layer, 0, j)),
                pl.BlockSpec((None, 1, tf), lambda i, j: (layer, 0, j))]
    nl = g_pre.shape[0]
    wl = layer if wg.shape[0] == nl else 0
    args =[h, g_pre.reshape(nl, 1, d), g_post.reshape(nl, 1, d), wg, wu, wd, cw, cb.reshape(nl, 1, f)]
    if has_state:
        in_specs += [pl.BlockSpec((tm, tf), lambda i, j: (i, j))] * 2
        args += list(state_rows)
    out_specs = [pl.BlockSpec((tm, d), lambda i, j: (i, 0)),
                 pl.BlockSpec((None, tail, tf), lambda i, j: (i, 0, j))]
    out_shape = [jax.ShapeDtypeStruct((m, d), F32),
                 jax.ShapeDtypeStruct((ni, tail, f), F32)]
    if emit_w:
        out_specs += [pl.BlockSpec((None, d, tf), lambda i, j: (0, 0, j)),
                      pl.BlockSpec((None, d, tf), lambda i, j: (0, 0, j)),
                      pl.BlockSpec((None, tf, d), lambda i, j: (0, j, 0))]
        out_shape += [jax.ShapeDtypeStruct((1, d, f), BF16), jax.ShapeDtypeStruct((1, d, f), BF16),
                      jax.ShapeDtypeStruct((1, f, d), BF16)]
    return pl.pallas_call(
        functools.partial(_conv_ffn_kernel, tm=tm, seq_len=seq_len, has_state=has_state, emit_w=emit_w,
                          tail=tail, n_split=n_split),
        grid=(ni, nj),
        in_specs=in_specs,
        out_specs=out_specs,
        out_shape=out_shape,
        scratch_shapes=[pltpu.VMEM((tm, d), BF16),
                        pltpu.VMEM((tm + 8, tf), F32),
                        pltpu.VMEM((nj, 8, tf), F32)],
        compiler_params=_cparams(("arbitrary", "arbitrary")),
        name="conv_ffn",
    )(*args)


def _sb_logs(z, valid):
    nz = -z
    sp = jnp.log2(1.0 + jnp.exp2(jnp.minimum(z, nz)))
    log_stay = jnp.minimum(nz, 0.0) - sp
    log_beta = z + log_stay
    if valid is not None:
        log_stay = jnp.where(valid, log_stay, 0.0)
    return log_stay, log_beta


def _suffix_matrix(n, with_total):
    u = (np.arange(n)[:, None] > np.arange(n)[None, :]).astype(np.float32)
    if with_total:
        u = np.concatenate([u, np.ones((n, n), np.float32)], axis=1)
    return jnp.asarray(u, dtype=BF16)


def _sb_prompt_kernel(bias_ref, q_ref, k_ref, v_ref, u2_ref, o_ref, kb_ref, vb_ref, qs_ref, acc_ref, *, grp):
    qi = pl.program_id(1)
    tq = q_ref.shape[0]
    seq = kb_ref.shape[0]
    tk = SB_KEY_BLOCK
    rows = grp * tq
    heads = range(KV_A)

    @pl.when(qi == 0)
    def _():
        for hh in heads:
            kb_ref[:, hh * HD_A:(hh + 1) * HD_A] = k_ref[pl.ds(hh, seq, stride=KV_A), :].astype(BF16)
            vb_ref[:, hh * HD_A:(hh + 1) * HD_A] = v_ref[pl.ds(hh, seq, stride=KV_A), :].astype(BF16)

    row = lax.broadcasted_iota(jnp.int32, (rows, 1), 0)
    biases = []
    for hh in heads:
        for g in range(grp):
            c0 = (hh * grp + g) * HD_A
            qs_ref[hh, g * tq:(g + 1) * tq, :] = q_ref[:, c0:c0 + HD_A]
        bias = jnp.zeros((rows, 1), F32)
        for g in range(grp):
            bias = jnp.where(row // tq == g, bias_ref[hh * grp + g] * LOG2E, bias)
        biases.append(bias)
    u2 = u2_ref[...]
    acc_ref[...] = jnp.zeros_like(acc_ref)

    def step(kb, n_blk, carries, masked):
        start = pl.multiple_of(kb * tk, tk)
        width = n_blk * tk
        valid = None
        if masked:
            q_pos = qi * tq + row % tq
            k_pos = kb * tk + lax.broadcasted_iota(jnp.int32, (rows, width), 1)
            valid = k_pos < q_pos
        zs = [lax.dot_general(qs_ref[hh], kb_ref[pl.ds(start, width), hh * HD_A:(hh + 1) * HD_A],
                              (((1,), (1,)), ((), ())), preferred_element_type=F32) + biases[hh] for hh in heads]
        logs = [_sb_logs(zs[hh], valid) for hh in heads]
        laters, out = [], []
        for hh in heads:
            stay16 = logs[hh][0].astype(BF16)
            carry = carries[hh]
            parts = [None] * n_blk
            for blk in reversed(range(n_blk)):
                sl = slice(blk * tk, (blk + 1) * tk)
                parts[blk] = jnp.dot(stay16[:, sl], u2, preferred_element_type=F32) + carry
                carry = carry + jnp.sum(logs[hh][0][:, sl], axis=1, keepdims=True)
            laters.append(parts[0] if n_blk == 1 else jnp.concatenate(parts, axis=1))
            out.append(carry)
        for hh in heads:
            a = jnp.exp2(logs[hh][1] + laters[hh])
            if masked:
                a = jnp.where(valid, a, 0.0)
            vblk = vb_ref[pl.ds(start, width), hh * HD_A:(hh + 1) * HD_A]
            acc_ref[hh] += jnp.dot(a.astype(BF16), vblk, preferred_element_type=F32)
        return tuple(out)

    kb_diag = (qi * tq) // tk
    odd = kb_diag % 2
    carries = step(kb_diag, 1, tuple(jnp.zeros((rows, 1), F32) for _ in heads), True)
    carries = lax.fori_loop(0, odd, lambda n, c: step(kb_diag - 1, 1, c, False), carries)
    lax.fori_loop(0, kb_diag // 2, lambda n, c: step(kb_diag - odd - 2 - 2 * n, 2, c, False), carries)
    for hh in heads:
        for g in range(grp):
            c0 = (hh * grp + g) * HD_A
            o_ref[:, c0:c0 + HD_A] = acc_ref[hh, g * tq:(g + 1) * tq, :].astype(o_ref.dtype)


def sb_prompt(q, k_all, v_all, layer, sb_bias, batch, seq):
    qw = q.shape[1]
    grp = qw // (KV_A * HD_A)
    nq = seq // Q_BLOCK
    kv_spec = pl.BlockSpec((None, seq * KV_A, HD_A), lambda b, i: (layer, b, 0))
    return pl.pallas_call(
        functools.partial(_sb_prompt_kernel, grp=grp),
        grid=(batch, nq),
        in_specs=[pl.BlockSpec(memory_space=pltpu.SMEM),
                  pl.BlockSpec((Q_BLOCK, qw), lambda b, i: (b * nq + i, 0)),
                  kv_spec, kv_spec,
                  pl.BlockSpec((SB_KEY_BLOCK, SB_KEY_BLOCK), lambda b, i: (0, 0))],
        out_specs=pl.BlockSpec((Q_BLOCK, qw), lambda b, i: (b * nq + i, 0)),
        out_shape=jax.ShapeDtypeStruct((batch * seq, qw), BF16),
        scratch_shapes=[pltpu.VMEM((seq, KV_A * HD_A), BF16),
                        pltpu.VMEM((seq, KV_A * HD_A), BF16),
                        pltpu.VMEM((KV_A, grp * Q_BLOCK, HD_A), BF16),
                        pltpu.VMEM((KV_A, grp * Q_BLOCK, HD_A), F32)],
        compiler_params=_cparams(("arbitrary", "arbitrary")),
        name="sb_prompt",
    )(sb_bias, q, k_all, v_all, _suffix_matrix(SB_KEY_BLOCK, False))


def _sb_decode_kernel(pt_ref, bias_ref, q_ref, kn_ref, vn_ref, *rest, pages, grp, t_s):
    k_refs = rest[:pages]
    v_refs = rest[pages:2 * pages]
    u2_ref, o_ref, qs_ref, acc_ref, carry_ref = rest[2 * pages:]
    c = pl.program_id(1)
    rk = grp * t_s
    rows = KV_A * rk
    page = u2_ref.shape[0]
    row = lax.broadcasted_iota(jnp.int32, (rows, 1), 0)
    bias = jnp.zeros((rows, 1), F32)
    for hh in range(KV_A * grp):
        bias = jnp.where(row // t_s == hh, bias_ref[hh] * LOG2E, bias)

    def head_rows(ref, k):
        return ref[pl.ds(k, page, stride=KV_A), :].astype(BF16)

    def process(krefs, vrefs, masked):
        n = len(krefs)
        zs = []
        for kr in krefs:
            zk = [lax.dot_general(qs_ref[k], head_rows(kr, k), (((1,), (1,)), ((), ())),
                                  preferred_element_type=F32) for k in range(KV_A)]
            zs.append(jnp.concatenate(zk, axis=0))
        z = jnp.concatenate(zs, axis=1) + bias
        valid = None
        if masked:
            valid = lax.broadcasted_iota(jnp.int32, (rows, n * page), 1) < row % t_s
        log_stay, log_beta = _sb_logs(z, valid)
        stay16 = log_stay.astype(BF16)
        u2 = u2_ref[...]
        carry = carry_ref[...]
        a_pages = []
        for j in range(n):
            sl = slice(j * page, (j + 1) * page)
            r = jnp.dot(stay16[:, sl], u2, preferred_element_type=F32)
            a = jnp.exp2(log_beta[:, sl] + r[:, :page] + carry)
            if masked:
                a = jnp.where(valid[:, sl], a, 0.0)
            a_pages.append(a.astype(BF16))
            carry = carry + r[:, page:]
        carry_ref[...] = carry
        for k in range(KV_A):
            acc = acc_ref[k]
            for j in range(n):
                acc = acc + jnp.dot(a_pages[j][k * rk:(k + 1) * rk, :], head_rows(vrefs[j], k),
                                    preferred_element_type=F32)
            acc_ref[k] = acc

    @pl.when(c == 0)
    def _():
        qs_ref[...] = (q_ref[...] * ((HD_A ** -0.5) * LOG2E)).astype(BF16)
        acc_ref[...] = jnp.zeros_like(acc_ref)
        carry_ref[...] = jnp.zeros_like(carry_ref)
        process([kn_ref], [vn_ref], True)

    process(k_refs, v_refs, False)

    @pl.when(c == pl.num_programs(1) - 1)
    def _():
        o_ref[...] = acc_ref[...]


def sb_decode(q, k_new, v_new, cache_k, cache_v, layer, page_table, sb_bias, *, pages):
    n_seq, _, rk, _ = q.shape
    n_pages = page_table.shape[1]
    prow = cache_k.shape[2]
    page = prow // KV_A
    grp = sb_bias.shape[0] // KV_A
    t_s = rk // grp
    n_chunks = n_pages // pages

    def page_spec(j):
        return pl.BlockSpec((None, None, prow, HD_A),
                            lambda b, c, pt: (layer, pt[b, n_pages - 1 - (c * pages + j)], 0, 0))

    blk4 = pl.BlockSpec((None, KV_A, rk, HD_A), lambda b, c, pt: (b, 0, 0, 0))
    new_spec = pl.BlockSpec((None, prow, HD_A), lambda b, c, pt: (b, 0, 0))
    grid_spec = pltpu.PrefetchScalarGridSpec(
        num_scalar_prefetch=1,
        grid=(n_seq, n_chunks),
        in_specs=([pl.BlockSpec(memory_space=pltpu.SMEM), blk4, new_spec, new_spec]
                  + [page_spec(j) for j in range(pages)] * 2
                  + [pl.BlockSpec((page, 2 * page), lambda b, c, pt: (0, 0))]),
        out_specs=blk4,
        scratch_shapes=[pltpu.VMEM((KV_A, rk, HD_A), BF16),
                        pltpu.VMEM((KV_A, rk, HD_A), F32),
                        pltpu.VMEM((KV_A * rk, page), F32)],
    )
    return pl.pallas_call(
        functools.partial(_sb_decode_kernel, pages=pages, grp=grp, t_s=t_s),
        grid_spec=grid_spec,
        out_shape=jax.ShapeDtypeStruct(q.shape, F32),
        compiler_params=_cparams(("arbitrary", "arbitrary")),
        name="sb_decode",
    )(page_table, sb_bias, q, k_new, v_new, *([cache_k] * pages), *([cache_v] * pages),
      _suffix_matrix(page, True))


def _t5_bucket_table(dist):
    d = np.maximum(dist, 0)
    large = MAX_EXACT + (np.log(np.maximum(d, 1).astype(np.float32) / np.float32(MAX_EXACT))
                         / np.float32(math.log(MAX_DISTANCE / MAX_EXACT))
                         * np.float32(N_BUCKETS - MAX_EXACT)).astype(np.int32)
    large = np.minimum(large, N_BUCKETS - 1)
    return np.where(d < MAX_EXACT, d, large).astype(np.int32)


def _win_prompt_kernel(rb_ref, sink_ref, q_ref, kvp_ref, kvc_ref, bkt_ref, o_ref, tbl_ref, sinkcol_ref, *, n_heads):
    b = pl.program_id(0)
    qi = pl.program_id(1)
    tq = q_ref.shape[0]
    grp = n_heads // KV_B
    kvw = KV_B * HD_B

    @pl.when((b == 0) & (qi == 0))
    def _():
        t_idx = lax.broadcasted_iota(jnp.int32, (tq, 2 * WINDOW), 0)
        s_idx = lax.broadcasted_iota(jnp.int32, (tq, 2 * WINDOW), 1)
        dist = t_idx + WINDOW - s_idx
        valid = (dist >= 0) & (dist < WINDOW)
        bucket = bkt_ref[...]

        def body(h, _):
            acc = jnp.zeros(bucket.shape, F32)
            for bb in range(N_BUCKETS):
                acc = jnp.where(bucket == bb, rb_ref[bb, h], acc)
            r0 = pl.multiple_of((h % grp) * tq, tq)
            tbl_ref[0, h // grp, pl.ds(r0, tq), :] = jnp.where(valid, acc, NEG)
            tbl_ref[1, h // grp, pl.ds(r0, tq), :] = jnp.where(valid & (s_idx >= WINDOW), acc, NEG)
            sinkcol_ref[h // grp, pl.ds(r0, tq), :] = jnp.full((tq, 1), sink_ref[h], F32)
            return 0
        lax.fori_loop(0, n_heads, body, 0)

    variant = (qi == 0).astype(jnp.int32)
    kv = jnp.concatenate([kvp_ref[...], kvc_ref[...]], axis=0).astype(BF16)
    kvs = range(KV_B)
    qgs = [jnp.concatenate([q_ref[:, (k * grp + g) * HD_B:(k * grp + g + 1) * HD_B] for g in range(grp)], axis=0)
           for k in kvs]
    ss = [lax.dot_general(qgs[k], kv[:, k * HD_B:(k + 1) * HD_B], (((1,), (1,)), ((), ())),
                          preferred_element_type=F32) + tbl_ref[variant, k] for k in kvs]
    ms = [jnp.maximum(jnp.max(ss[k], axis=-1, keepdims=True), sinkcol_ref[k]) for k in kvs]
    ps = [jnp.exp(ss[k] - ms[k]) for k in kvs]
    dens = [jnp.sum(ps[k], axis=-1, keepdims=True) + jnp.exp(sinkcol_ref[k] - ms[k]) for k in kvs]
    os_ = [jnp.dot(ps[k].astype(BF16), kv[:, kvw + k * HD_B:kvw + (k + 1) * HD_B], preferred_element_type=F32)
           / dens[k] for k in kvs]
    for k in kvs:
        for g in range(grp):
            h = k * grp + g
            o_ref[:, h * HD_B:(h + 1) * HD_B] = os_[k][g * tq:(g + 1) * tq, :].astype(o_ref.dtype)


def win_prompt(q, kv, sinks, rel_bias, batch, seq):
    n_heads = q.shape[1] // HD_B
    nq = seq // WINDOW
    t = np.arange(WINDOW)[:, None]
    s = np.arange(2 * WINDOW)[None, :]
    bucket = jnp.asarray(_t5_bucket_table(t + WINDOW - s))
    return pl.pallas_call(
        functools.partial(_win_prompt_kernel, n_heads=n_heads),
        grid=(batch, nq),
        in_specs=[pl.BlockSpec(memory_space=pltpu.SMEM),
                  pl.BlockSpec(memory_space=pltpu.SMEM),
                  pl.BlockSpec((WINDOW, q.shape[1]), lambda b, i: (b * nq + i, 0)),
                  pl.BlockSpec((WINDOW, kv.shape[1]), lambda b, i: (jnp.maximum(b * nq + i - 1, 0), 0)),
                  pl.BlockSpec((WINDOW, kv.shape[1]), lambda b, i: (b * nq + i, 0)),
                  pl.BlockSpec((WINDOW, 2 * WINDOW), lambda b, i: (0, 0))],
        out_specs=pl.BlockSpec((WINDOW, q.shape[1]), lambda b, i: (b * nq + i, 0)),
        out_shape=jax.ShapeDtypeStruct(q.shape, BF16),
        scratch_shapes=[pltpu.VMEM((2, KV_B, (n_heads // KV_B) * WINDOW, 2 * WINDOW), F32),
                        pltpu.VMEM((KV_B, (n_heads // KV_B) * WINDOW, 1), F32)],
        compiler_params=_cparams(("arbitrary", "arbitrary")),
        name="win_prompt",
    )(rel_bias, sinks, q, kv, kv, bucket)


def _win_decode_kernel(rb_ref, sink_ref, q_ref, k_ref, v_ref, bkt_ref, o_ref, tbl_ref, sinkcol_ref, *, grp, t_s):
    rk, nk = bkt_ref.shape
    n_sub = q_ref.shape[0]

    @pl.when(pl.program_id(0) == 0)
    def _():
        t_idx = lax.broadcasted_iota(jnp.int32, (rk, nk), 0) % t_s
        j_idx = lax.broadcasted_iota(jnp.int32, (rk, nk), 1)
        dist = t_idx + WINDOW - j_idx
        valid = (dist >= 0) & (dist < WINDOW)
        bucket = bkt_ref[...]
        row_g = lax.broadcasted_iota(jnp.int32, (rk, nk), 0) // t_s
        row = lax.broadcasted_iota(jnp.int32, (rk, 1), 0)
        for k in range(KV_B):
            def body(g, tbl):
                acc = jnp.zeros((rk, nk), F32)
                for bb in range(N_BUCKETS):
                    acc = jnp.where(bucket == bb, rb_ref[bb, k * grp + g], acc)
                return jnp.where(row_g == g, acc, tbl)
            tbl = lax.fori_loop(0, grp, body, jnp.zeros((rk, nk), F32))
            tbl_ref[k] = jnp.where(valid, tbl, NEG)
            sink = jnp.zeros((rk, 1), F32)
            for g in range(grp):
                sink = jnp.where(row // t_s == g, sink_ref[k * grp + g], sink)
            sinkcol_ref[k] = sink

    scale = HD_B ** -0.5
    items = [(s, k) for s in range(n_sub) for k in range(KV_B)]
    ss = [lax.dot_general(q_ref[s, k].astype(BF16), k_ref[s, k].astype(BF16), (((1,), (1,)), ((), ())),
                          preferred_element_type=F32) * scale + tbl_ref[k] for s, k in items]
    ms = [jnp.maximum(jnp.max(ss[n], axis=-1, keepdims=True), sinkcol_ref[k]) for n, (s, k) in enumerate(items)]
    ps = [jnp.exp(ss[n] - ms[n]) for n in range(len(items))]
    dens = [jnp.sum(ps[n], axis=-1, keepdims=True) + jnp.exp(sinkcol_ref[k] - ms[n])
            for n, (s, k) in enumerate(items)]
    for n, (s, k) in enumerate(items):
        o_ref[s, k] = jnp.dot(ps[n].astype(BF16), v_ref[s, k].astype(BF16), preferred_element_type=F32) / dens[n]


def win_decode(q, k, v, sinks, rel_bias, t_s):
    n_seq, _, rk, _ = q.shape
    nk = k.shape[2]
    grp = rk // t_s
    n_sub = WIN_DECODE_SEQS
    t = (np.arange(rk) % t_s)[:, None]
    j = np.arange(nk)[None, :]
    bucket = jnp.asarray(_t5_bucket_table(t + WINDOW - j))
    blk_q = pl.BlockSpec((n_sub, KV_B, rk, HD_B), lambda b: (b, 0, 0, 0))
    blk_k = pl.BlockSpec((n_sub, KV_B, nk, HD_B), lambda b: (b, 0, 0, 0))
    return pl.pallas_call(
        functools.partial(_win_decode_kernel, grp=grp, t_s=t_s),
        grid=(n_seq // n_sub,),
        in_specs=[pl.BlockSpec(memory_space=pltpu.SMEM),
                  pl.BlockSpec(memory_space=pltpu.SMEM),
                  blk_q, blk_k, blk_k,
                  pl.BlockSpec((rk, nk), lambda b: (0, 0))],
        out_specs=blk_q,
        out_shape=jax.ShapeDtypeStruct(q.shape, F32),
        scratch_shapes=[pltpu.VMEM((KV_B, rk, nk), F32),
                        pltpu.VMEM((KV_B, rk, 1), F32)],
        compiler_params=_cparams(("arbitrary",)),
        name="win_decode",
    )(rel_bias, sinks, q, k, v, bucket)


def _heads_to_rows(x, n_seq, t_s, n_kv, hd):
    grp = x.shape[1] // (n_kv * hd)
    x = x.reshape(n_seq, t_s, n_kv, grp, hd).transpose(0, 2, 3, 1, 4)
    return x.reshape(n_seq, n_kv, grp * t_s, hd)


def _rows_to_heads(x, t_s):
    n_seq, n_kv, rk, hd = x.shape
    grp = rk // t_s
    x = x.reshape(n_seq, n_kv, grp, t_s, hd).transpose(0, 3, 1, 2, 4)
    return x.reshape(n_seq * t_s, n_kv * grp * hd)


def kernel(x_prompt, x_sample, cache_k_a, cache_v_a, page_table, cache_k_b, cache_v_b, state_conv,
           w_qkv_a, w_o_a, sb_bias, g_kv, w_kv_b, w_q_b, w_o_b, sinks_b, rel_bias,
           g_pre_mix, g_post_mix, g_pre_ffn, g_post_ffn, w_gate, w_up, w_down, conv_w, conv_b):
    batch, seq, d = x_prompt.shape
    n_seq, t_s, _ = x_sample.shape
    n_a = w_qkv_a.shape[0]
    depth = w_gate.shape[0]
    d_ff = w_gate.shape[2]
    page = cache_k_a.shape[2]
    h_a = w_o_a.shape[1] // HD_A
    mp, ms = batch * seq, n_seq * t_s
    qw = h_a * HD_A
    kvw_a = KV_A * HD_A
    kvw_b = KV_B * HD_B
    assert t_s >= CONV_W - 1 and seq % TM_PROMPT == 0

    cache_k4 = cache_k_a.reshape(cache_k_a.shape[0], cache_k_a.shape[1], page * KV_A, HD_A)
    cache_v4 = cache_v_a.reshape(cache_v_a.shape[0], cache_v_a.shape[1], page * KV_A, HD_A)
    w_kv_b3 = w_kv_b.reshape(1, d, 2 * kvw_b)

    hp = x_prompt.reshape(mp, d)
    hs = x_sample.reshape(ms, d)
    ka_s, va_s, conv_p, conv_s = [], [], [], []
    kv_all = kv_p = kb_s = vb_s = kq_s = vq_s = None
    for l in range(depth):
        if l < n_a:
            q_p, k_all, v_all = norm_qkv(hp, g_pre_mix[l], w_qkv_a, l, TM_PROMPT, (HD_A ** -0.5) * LOG2E, kv_all)
            kv_all = (k_all, v_all)
            qkv_s = norm_matmul(hs, g_pre_mix[l], w_qkv_a, l, ms, TN_PROJ)
            op = sb_prompt(q_p, k_all, v_all, l, sb_bias[l], batch, seq)
            q_s = _heads_to_rows(qkv_s[:, :qw], n_seq, t_s, KV_A, HD_A)
            k_new = qkv_s[:, qw:qw + kvw_a]
            v_new = qkv_s[:, qw + kvw_a:]
            pad = ((0, 0), (0, (page - t_s) * KV_A), (0, 0))
            o_s = sb_decode(q_s, jnp.pad(k_new.reshape(n_seq, t_s * KV_A, HD_A), pad),
                            jnp.pad(v_new.reshape(n_seq, t_s * KV_A, HD_A), pad),
                            cache_k4, cache_v4, l, page_table, sb_bias[l], pages=SB_PAGES_PER_STEP)
            o_s = _rows_to_heads(o_s, t_s)
            hp = proj_res(op, w_o_a, l, g_post_mix[l], hp, TM_PROJ_RES)
            hs = proj_res(o_s, w_o_a, l, g_post_mix[l], hs, ms)
            ka_s.append(k_new.reshape(n_seq, t_s, KV_A, HD_A))
            va_s.append(v_new.reshape(n_seq, t_s, KV_A, HD_A))
        else:
            j = l - n_a
            if j == 0:
                kv_p = norm_matmul(hp, g_kv, w_kv_b3, 0, TM_PROMPT, 2 * kvw_b)
                kv_s = norm_matmul(hs, g_kv, w_kv_b3, 0, ms, 2 * kvw_b)
                kb_s = jnp.concatenate([cache_k_b, kv_s[:, :kvw_b].reshape(n_seq, t_s, KV_B, HD_B)], axis=1)
                vb_s = jnp.concatenate([cache_v_b, kv_s[:, kvw_b:].reshape(n_seq, t_s, KV_B, HD_B)], axis=1)
                padk = ((0, 0), (0, 0), (0, 2 * WINDOW - (WINDOW + t_s)), (0, 0))
                kq_s = jnp.pad(kb_s.transpose(0, 2, 1, 3), padk)
                vq_s = jnp.pad(vb_s.transpose(0, 2, 1, 3), padk)
            q_p = norm_matmul(hp, g_pre_mix[l], w_q_b, j, TM_PROMPT, TN_PROJ, BF16, HD_B ** -0.5)
            q_s = norm_matmul(hs, g_pre_mix[l], w_q_b, j, ms, TN_PROJ)
            op = win_prompt(q_p, kv_p, sinks_b[j], rel_bias, batch, seq)
            o_s = win_decode(_heads_to_rows(q_s, n_seq, t_s, KV_B, HD_B), kq_s, vq_s, sinks_b[j], rel_bias, t_s)
            o_s = _rows_to_heads(o_s, t_s)
            hp = proj_res(op, w_o_b, j, g_post_mix[l], hp, TM_PROJ_RES)
            hs = proj_res(o_s, w_o_b, j, g_post_mix[l], hs, ms)
        st = state_conv[l]
        zeros = jnp.zeros((n_seq, t_s - 1, d_ff), F32)
        s1 = jnp.concatenate([st[:, 1:2], zeros], axis=1).reshape(ms, d_ff)
        s2 = jnp.concatenate([st, zeros[:, 1:]], axis=1).reshape(ms, d_ff)
        hs, gt_s, wg16, wu16, wd16 = conv_ffn(hs, g_pre_ffn, g_post_ffn, w_gate, w_up, w_down, conv_w, conv_b, l,
                                              tm=ms, tf=TF_SAMPLE, seq_len=t_s, n_split=1, state_rows=(s1, s2))
        hp, gt_p = conv_ffn(hp, g_pre_ffn, g_post_ffn, wg16, wu16, wd16, conv_w, conv_b, l,
                            tm=TM_PROMPT, tf=TF_PROMPT, seq_len=seq, n_split=FFN_ROW_GROUPS)
        tiles_per_seq = seq // TM_PROMPT
        conv_p.append(gt_p.reshape(batch, tiles_per_seq, 8, d_ff)[:, -1, 8 - (CONV_W - 1):, :])
        conv_s.append(gt_s.reshape(n_seq, t_s, d_ff)[:, t_s - (CONV_W - 1):, :])

    kvp4 = kv_p.reshape(batch, seq, 2 * kvw_b)[:, seq - WINDOW:].reshape(batch, WINDOW, 2, KV_B, HD_B)
    return (hp.reshape(batch, seq, d), hs.reshape(n_seq, t_s, d),
            kv_all[0].reshape(n_a, batch, seq, KV_A, HD_A), kv_all[1].reshape(n_a, batch, seq, KV_A, HD_A),
            jnp.stack(ka_s), jnp.stack(va_s),
            kvp4[:, :, 0], kvp4[:, :, 1],
            kb_s[:, -WINDOW:], vb_s[:, -WINDOW:],
            jnp.stack(conv_p), jnp.stack(conv_s))
```

```python
import functools
import math

import numpy as np
import jax
import jax.numpy as jnp
from jax import lax
from jax.experimental import pallas as pl
from jax.experimental.pallas import tpu as pltpu

F32 = jnp.float32
BF16 = jnp.bfloat16

EPS = 1e-6
HD_A = 128
KV_A = 4
HD_B = 64
KV_B = 4
WINDOW = 128
Q_BLOCK = 128
N_BUCKETS = 32
MAX_EXACT = N_BUCKETS // 2
MAX_DISTANCE = WINDOW
CONV_W = 3
NEG = -1e30
LOG2E = math.log2(math.e)
SB_KEY_BLOCK = 256
SB_PAGES_PER_STEP = 16
VMEM_LIMIT = 62 * 1024 * 1024
TM_PROMPT = 1024
TF_PROMPT = 512
FFN_ROW_GROUPS = 8
TF_SAMPLE = 512
TN_PROJ = 512
TM_PROJ_RES = 512
NORM_GROUP_ROWS = 256
WIN_DECODE_SEQS = 8


def _cparams(sem):
    return pltpu.CompilerParams(dimension_semantics=sem, vmem_limit_bytes=VMEM_LIMIT)


def _rmsnorm_rows(x, g):
    ms = jnp.mean(x * x, axis=-1, keepdims=True)
    return x * lax.rsqrt(ms + EPS) * g


def _normed_tile_times_w(h_ref, g_ref, w_ref, hn_ref, wb_ref, emit):
    i = pl.program_id(0)
    j = pl.program_id(1)
    tm = h_ref.shape[0]
    n_split = max(1, tm // NORM_GROUP_ROWS)
    th = tm // n_split

    @pl.when(i == 0)
    def _():
        wb_ref[j] = w_ref[...].astype(BF16)

    @pl.when(j == 0)
    def _():
        ys = []
        for r in range(n_split):
            hn = _rmsnorm_rows(h_ref[r * th:(r + 1) * th, :], g_ref[...]).astype(BF16)
            hn_ref[r * th:(r + 1) * th, :] = hn
            ys.append(jnp.dot(hn, wb_ref[j], preferred_element_type=F32))
        emit(ys[0] if n_split == 1 else jnp.concatenate(ys, axis=0))

    @pl.when(j > 0)
    def _():
        emit(jnp.dot(hn_ref[...], wb_ref[j], preferred_element_type=F32))


def _resident_w_spec(layer, d, tn, n_tiles):
    return pl.BlockSpec((None, d, tn), lambda i, j: (layer, 0, jnp.where(i == 0, j, n_tiles - 1)))


def _norm_matmul_kernel(h_ref, g_ref, w_ref, o_ref, hn_ref, wb_ref, *, scale):
    def emit(y):
        o_ref[...] = (y if scale is None else y * scale).astype(o_ref.dtype)

    _normed_tile_times_w(h_ref, g_ref, w_ref, hn_ref, wb_ref, emit)


def norm_matmul(h, g, w, layer, tm, tn, out_dtype=F32, scale=None):
    m, d = h.shape
    n = w.shape[2]
    return pl.pallas_call(
        functools.partial(_norm_matmul_kernel, scale=scale),
        grid=(m // tm, n // tn),
        in_specs=[pl.BlockSpec((tm, d), lambda i, j: (i, 0)),
                  pl.BlockSpec((1, d), lambda i, j: (0, 0)),
                  _resident_w_spec(layer, d, tn, n // tn)],
        out_specs=pl.BlockSpec((tm, tn), lambda i, j: (i, j)),
        out_shape=jax.ShapeDtypeStruct((m, n), out_dtype),
        scratch_shapes=[pltpu.VMEM((tm, d), BF16),
                        pltpu.VMEM((n // tn, d, tn), BF16)],
        compiler_params=_cparams(("arbitrary", "arbitrary")),
        name="norm_matmul",
    )(h, g.reshape(1, d), w)


def _norm_qkv_kernel(*refs, n_q_tiles, scale, aliased):
    if aliased:
        h_ref, g_ref, w_ref, _, _, q_ref, k_ref, v_ref, hn_ref, wb_ref = refs
    else:
        h_ref, g_ref, w_ref, q_ref, k_ref, v_ref, hn_ref, wb_ref = refs
    j = pl.program_id(1)
    tm = h_ref.shape[0]
    def emit(y):
        @pl.when(j < n_q_tiles)
        def _():
            q_ref[...] = (y * scale).astype(q_ref.dtype)

        def rows_out(ref):
            for kh in range(KV_A):
                ref[pl.ds(kh, tm, stride=KV_A), :] = y[:, kh * HD_A:(kh + 1) * HD_A]

        @pl.when(j == n_q_tiles)
        def _():
            rows_out(k_ref)

        @pl.when(j == n_q_tiles + 1)
        def _():
            rows_out(v_ref)

    _normed_tile_times_w(h_ref, g_ref, w_ref, hn_ref, wb_ref, emit)


def norm_qkv(h, g, w, layer, tm, q_scale, kv_prev=None):
    m, d = h.shape
    nl = w.shape[0]
    tn = KV_A * HD_A
    qw = w.shape[2] - 2 * tn
    nqt = qw // tn
    aliased = kv_prev is not None
    kv_shape = jax.ShapeDtypeStruct((nl, m * KV_A, HD_A), F32)
    kv_spec = pl.BlockSpec((None, tm * KV_A, HD_A), lambda i, j: (layer, i, 0))
    in_specs = [pl.BlockSpec((tm, d), lambda i, j: (i, 0)),
                pl.BlockSpec((1, d), lambda i, j: (0, 0)),
                _resident_w_spec(layer, d, tn, nqt + 2)]
    args = [h, g.reshape(1, d), w]
    if aliased:
        in_specs += [pl.BlockSpec(memory_space=pl.ANY)] * 2
        args += list(kv_prev)
    return pl.pallas_call(
        functools.partial(_norm_qkv_kernel, n_q_tiles=nqt, scale=q_scale, aliased=aliased),
        grid=(m // tm, nqt + 2),
        in_specs=in_specs,
        out_specs=[pl.BlockSpec((tm, tn), lambda i, j: (i, jnp.minimum(j, nqt - 1))), kv_spec, kv_spec],
        out_shape=[jax.ShapeDtypeStruct((m, qw), BF16), kv_shape, kv_shape],
        scratch_shapes=[pltpu.VMEM((tm, d), BF16),
                        pltpu.VMEM((nqt + 2, d, tn), BF16)],
        input_output_aliases={3: 1, 4: 2} if aliased else {},
        compiler_params=_cparams(("arbitrary", "arbitrary")),
        name="norm_qkv",
    )(*args)


def _proj_res_kernel(o_ref, w_ref, g_ref, h_ref, out_ref, wb_ref):
    @pl.when(pl.program_id(0) == 0)
    def _():
        wb_ref[...] = w_ref[...].astype(BF16)

    y = jnp.dot(o_ref[...].astype(BF16), wb_ref[...], preferred_element_type=F32)
    out_ref[...] = h_ref[...] + _rmsnorm_rows(y, g_ref[...])


def proj_res(o, w, layer, g, h, tm):
    m, k = o.shape
    d = w.shape[2]
    return pl.pallas_call(
        _proj_res_kernel,
        grid=(m // tm,),
        in_specs=[pl.BlockSpec((tm, k), lambda i: (i, 0)),
                  pl.BlockSpec((None, k, d), lambda i: (layer, 0, 0), pipeline_mode=pl.Buffered(1)),
                  pl.BlockSpec((1, d), lambda i: (0, 0)),
                  pl.BlockSpec((tm, d), lambda i: (i, 0))],
        out_specs=pl.BlockSpec((tm, d), lambda i: (i, 0)),
        out_shape=jax.ShapeDtypeStruct((m, d), F32),
        scratch_shapes=[pltpu.VMEM((k, d), BF16)],
        compiler_params=_cparams(("arbitrary",)),
        name="proj_res",
    )(o, w, g.reshape(1, d), h)


def _gelu_tanh(x):
    c = math.sqrt(2.0 / math.pi)
    return 0.5 * x * (1.0 + jnp.tanh(c * (x + 0.044715 * (x * x * x))))


def _conv_ffn_kernel(*refs, tm, seq_len, has_state, emit_w, tail, n_split):
    h_ref, gpre_ref, gpost_ref, wg_ref, wu_ref, wd_ref, cw_ref, cb_ref = refs[:8]
    refs = refs[8:]
    if has_state:
        s1_ref, s2_ref = refs[:2]
        refs = refs[2:]
    out_ref, gt_ref = refs[:2]
    refs = refs[2:]
    if emit_w:
        wg16_ref, wu16_ref, wd16_ref = refs[:3]
        refs = refs[3:]
    hn_ref, gp_ref, carry_ref = refs
    i = pl.program_id(0)
    j = pl.program_id(1)

    @pl.when(i == 0)
    def _():
        carry_ref[j] = jnp.zeros(carry_ref.shape[1:], F32)

    th = tm // n_split

    def step(first, last):
        wg = wg_ref[...].astype(BF16)
        wu = wu_ref[...].astype(BF16)
        wd = wd_ref[...].astype(BF16)
        if emit_w:
            wg16_ref[...] = wg
            wu16_ref[...] = wu
            wd16_ref[...] = wd
        cw = cw_ref[...]
        cb = cb_ref[...]
        gp_ref[0:8, :] = carry_ref[j]
        gs, us = [], []
        for r in range(n_split):
            r0 = r * th
            if first:
                hn = _rmsnorm_rows(h_ref[r0:r0 + th, :], gpre_ref[...]).astype(BF16)
                hn_ref[r0:r0 + th, :] = hn
            else:
                hn = hn_ref[r0:r0 + th, :]
            g = jnp.dot(hn, wg, preferred_element_type=F32)
            us.append(jnp.dot(hn, wu, preferred_element_type=F32))
            gs.append(g)
            gp_ref[8 + r0:8 + r0 + th, :] = g
        carry_ref[j] = gs[-1][th - 8:, :]
        if tail == tm:
            for r in range(n_split):
                gt_ref[r * th:(r + 1) * th, :] = gs[r]
        else:
            gt_ref[...] = gs[-1][th - tail:, :]
        for r in range(n_split):
            r0 = r * th
            g, u = gs[r], us[r]
            g1 = gp_ref[7 + r0:7 + r0 + th, :]
            g2 = gp_ref[6 + r0:6 + r0 + th, :]
            pos = (i * tm + r0 + lax.broadcasted_iota(jnp.int32, (th, 1), 0)) % seq_len
            if has_state:
                g1 = jnp.where(pos >= 1, g1, s1_ref[r0:r0 + th, :])
                g2 = jnp.where(pos >= 2, g2, s2_ref[r0:r0 + th, :])
            else:
                g1 = jnp.where(pos >= 1, g1, 0.0)
                g2 = jnp.where(pos >= 2, g2, 0.0)
            gc = cb + g2 * cw[0:1, :] + g1 * cw[1:2, :] + g * cw[2:3, :]
            y = (_gelu_tanh(gc) * u).astype(BF16)
            acc = jnp.dot(y, wd, preferred_element_type=F32)
            if not first:
                acc = out_ref[r0:r0 + th, :] + acc
            if last:
                acc = h_ref[r0:r0 + th, :] + _rmsnorm_rows(acc, gpost_ref[...])
            out_ref[r0:r0 + th, :] = acc

    nj = pl.num_programs(1)
    pl.when(j == 0)(lambda: step(True, False))
    pl.when((j > 0) & (j < nj - 1))(lambda: step(False, False))
    pl.when(j == nj - 1)(lambda: step(False, True))


def conv_ffn(h, g_pre, g_post, wg, wu, wd, cw, cb, layer, *, tm, tf, seq_len, n_split, state_rows=None):
    m, d = h.shape
    f = wg.shape[2]
    has_state = state_rows is not None
    emit_w = wg.dtype == F32
    tail = tm if has_state else 8
    ni, nj = m // tm, f // tf
    assert (not emit_w or ni == 1) and nj >= 2
    in_specs = [pl.BlockSpec((tm, d), lambda i, j: (i, 0)),
                pl.BlockSpec((None, 1, d), lambda i, j: (layer, 0, 0)),
                pl.BlockSpec((None, 1, d), lambda i, j: (layer, 0, 0)),
                pl.BlockSpec((None, d, tf), lambda i, j: (wl, 0, j)),
                pl.BlockSpec((None, d, tf), lambda i, j: (wl, 0, j)),
                pl.BlockSpec((None, tf, d), lambda i, j: (wl, j, 0)),
                pl.BlockSpec((None, CONV_W, tf), lambda i, j: (layer, 0, j)),
                pl.BlockSpec((None, 1, tf), lambda i, j: (layer, 0, j))]
    nl = g_pre.shape[0]
    wl = layer if wg.shape[0] == nl else 0
    args =[h, g_pre.reshape(nl, 1, d), g_post.reshape(nl, 1, d), wg, wu, wd, cw, cb.reshape(nl, 1, f)]
    if has_state:
        in_specs += [pl.BlockSpec((tm, tf), lambda i, j: (i, j))] * 2
        args += list(state_rows)
    out_specs = [pl.BlockSpec((tm, d), lambda i, j: (i, 0)),
                 pl.BlockSpec((None, tail, tf), lambda i, j: (i, 0, j))]
    out_shape = [jax.ShapeDtypeStruct((m, d), F32),
                 jax.ShapeDtypeStruct((ni, tail, f), F32)]
    if emit_w:
        out_specs += [pl.BlockSpec((None, d, tf), lambda i, j: (0, 0, j)),
                      pl.BlockSpec((None, d, tf), lambda i, j: (0, 0, j)),
                      pl.BlockSpec((None, tf, d), lambda i, j: (0, j, 0))]
        out_shape += [jax.ShapeDtypeStruct((1, d, f), BF16), jax.ShapeDtypeStruct((1, d, f), BF16),
                      jax.ShapeDtypeStruct((1, f, d), BF16)]
    return pl.pallas_call(
        functools.partial(_conv_ffn_kernel, tm=tm, seq_len=seq_len, has_state=has_state, emit_w=emit_w,
                          tail=tail, n_split=n_split),
        grid=(ni, nj),
        in_specs=in_specs,
        out_specs=out_specs,
        out_shape=out_shape,
        scratch_shapes=[pltpu.VMEM((tm, d), BF16),
                        pltpu.VMEM((tm + 8, tf), F32),
                        pltpu.VMEM((nj, 8, tf), F32)],
        compiler_params=_cparams(("arbitrary", "arbitrary")),
        name="conv_ffn",
    )(*args)


def _sb_logs(z, valid):
    nz = -z
    sp = jnp.log2(1.0 + jnp.exp2(jnp.minimum(z, nz)))
    log_stay = jnp.minimum(nz, 0.0) - sp
    log_beta = z + log_stay
    if valid is not None:
        log_stay = jnp.where(valid, log_stay, 0.0)
    return log_stay, log_beta


def _suffix_matrix(n, with_total):
    u = (np.arange(n)[:, None] > np.arange(n)[None, :]).astype(np.float32)
    if with_total:
        u = np.concatenate([u, np.ones((n, n), np.float32)], axis=1)
    return jnp.asarray(u, dtype=BF16)


def _sb_prompt_kernel(bias_ref, q_ref, k_ref, v_ref, u2_ref, o_ref, kb_ref, vb_ref, qs_ref, acc_ref, *, grp):
    qi = pl.program_id(1)
    tq = q_ref.shape[0]
    seq = kb_ref.shape[0]
    tk = SB_KEY_BLOCK
    rows = grp * tq
    heads = range(KV_A)

    @pl.when(qi == 0)
    def _():
        for hh in heads:
            kb_ref[:, hh * HD_A:(hh + 1) * HD_A] = k_ref[pl.ds(hh, seq, stride=KV_A), :].astype(BF16)
            vb_ref[:, hh * HD_A:(hh + 1) * HD_A] = v_ref[pl.ds(hh, seq, stride=KV_A), :].astype(BF16)

    row = lax.broadcasted_iota(jnp.int32, (rows, 1), 0)
    biases = []
    for hh in heads:
        for g in range(grp):
            c0 = (hh * grp + g) * HD_A
            qs_ref[hh, g * tq:(g + 1) * tq, :] = q_ref[:, c0:c0 + HD_A]
        bias = jnp.zeros((rows, 1), F32)
        for g in range(grp):
            bias = jnp.where(row // tq == g, bias_ref[hh * grp + g] * LOG2E, bias)
        biases.append(bias)
    u2 = u2_ref[...]
    acc_ref[...] = jnp.zeros_like(acc_ref)

    def step(kb, n_blk, carries, masked):
        start = pl.multiple_of(kb * tk, tk)
        width = n_blk * tk
        valid = None
        if masked:
            q_pos = qi * tq + row % tq
            k_pos = kb * tk + lax.broadcasted_iota(jnp.int32, (rows, width), 1)
            valid = k_pos < q_pos
        zs = [lax.dot_general(qs_ref[hh], kb_ref[pl.ds(start, width), hh * HD_A:(hh + 1) * HD_A],
                              (((1,), (1,)), ((), ())), preferred_element_type=F32) + biases[hh] for hh in heads]
        logs = [_sb_logs(zs[hh], valid) for hh in heads]
        laters, out = [], []
        for hh in heads:
            stay16 = logs[hh][0].astype(BF16)
            carry = carries[hh]
            parts = [None] * n_blk
            for blk in reversed(range(n_blk)):
                sl = slice(blk * tk, (blk + 1) * tk)
                parts[blk] = jnp.dot(stay16[:, sl], u2, preferred_element_type=F32) + carry
                carry = carry + jnp.sum(logs[hh][0][:, sl], axis=1, keepdims=True)
            laters.append(parts[0] if n_blk == 1 else jnp.concatenate(parts, axis=1))
            out.append(carry)
        for hh in heads:
            a = jnp.exp2(logs[hh][1] + laters[hh])
            if masked:
                a = jnp.where(valid, a, 0.0)
            vblk = vb_ref[pl.ds(start, width), hh * HD_A:(hh + 1) * HD_A]
            acc_ref[hh] += jnp.dot(a.astype(BF16), vblk, preferred_element_type=F32)
        return tuple(out)

    kb_diag = (qi * tq) // tk
    odd = kb_diag % 2
    carries = step(kb_diag, 1, tuple(jnp.zeros((rows, 1), F32) for _ in heads), True)
    carries = lax.fori_loop(0, odd, lambda n, c: step(kb_diag - 1, 1, c, False), carries)
    lax.fori_loop(0, kb_diag // 2, lambda n, c: step(kb_diag - odd - 2 - 2 * n, 2, c, False), carries)
    for hh in heads:
        for g in range(grp):
            c0 = (hh * grp + g) * HD_A
            o_ref[:, c0:c0 + HD_A] = acc_ref[hh, g * tq:(g + 1) * tq, :].astype(o_ref.dtype)


def sb_prompt(q, k_all, v_all, layer, sb_bias, batch, seq):
    qw = q.shape[1]
    grp = qw // (KV_A * HD_A)
    nq = seq // Q_BLOCK
    kv_spec = pl.BlockSpec((None, seq * KV_A, HD_A), lambda b, i: (layer, b, 0))
    return pl.pallas_call(
        functools.partial(_sb_prompt_kernel, grp=grp),
        grid=(batch, nq),
        in_specs=[pl.BlockSpec(memory_space=pltpu.SMEM),
                  pl.BlockSpec((Q_BLOCK, qw), lambda b, i: (b * nq + i, 0)),
                  kv_spec, kv_spec,
                  pl.BlockSpec((SB_KEY_BLOCK, SB_KEY_BLOCK), lambda b, i: (0, 0))],
        out_specs=pl.BlockSpec((Q_BLOCK, qw), lambda b, i: (b * nq + i, 0)),
        out_shape=jax.ShapeDtypeStruct((batch * seq, qw), BF16),
        scratch_shapes=[pltpu.VMEM((seq, KV_A * HD_A), BF16),
                        pltpu.VMEM((seq, KV_A * HD_A), BF16),
                        pltpu.VMEM((KV_A, grp * Q_BLOCK, HD_A), BF16),
                        pltpu.VMEM((KV_A, grp * Q_BLOCK, HD_A), F32)],
        compiler_params=_cparams(("arbitrary", "arbitrary")),
        name="sb_prompt",
    )(sb_bias, q, k_all, v_all, _suffix_matrix(SB_KEY_BLOCK, False))


def _sb_decode_kernel(pt_ref, bias_ref, q_ref, kn_ref, vn_ref, *rest, pages, grp, t_s):
    k_refs = rest[:pages]
    v_refs = rest[pages:2 * pages]
    u2_ref, o_ref, qs_ref, acc_ref, carry_ref = rest[2 * pages:]
    c = pl.program_id(1)
    rk = grp * t_s
    rows = KV_A * rk
    page = u2_ref.shape[0]
    row = lax.broadcasted_iota(jnp.int32, (rows, 1), 0)
    bias = jnp.zeros((rows, 1), F32)
    for hh in range(KV_A * grp):
        bias = jnp.where(row // t_s == hh, bias_ref[hh] * LOG2E, bias)

    def head_rows(ref, k):
        return ref[pl.ds(k, page, stride=KV_A), :].astype(BF16)

    def process(krefs, vrefs, masked):
        n = len(krefs)
        zs = []
        for kr in krefs:
            zk = [lax.dot_general(qs_ref[k], head_rows(kr, k), (((1,), (1,)), ((), ())),
                                  preferred_element_type=F32) for k in range(KV_A)]
            zs.append(jnp.concatenate(zk, axis=0))
        z = jnp.concatenate(zs, axis=1) + bias
        valid = None
        if masked:
            valid = lax.broadcasted_iota(jnp.int32, (rows, n * page), 1) < row % t_s
        log_stay, log_beta = _sb_logs(z, valid)
        stay16 = log_stay.astype(BF16)
        u2 = u2_ref[...]
        carry = carry_ref[...]
        a_pages = []
        for j in range(n):
            sl = slice(j * page, (j + 1) * page)
            r = jnp.dot(stay16[:, sl], u2, preferred_element_type=F32)
            a = jnp.exp2(log_beta[:, sl] + r[:, :page] + carry)
            if masked:
                a = jnp.where(valid[:, sl], a, 0.0)
            a_pages.append(a.astype(BF16))
            carry = carry + r[:, page:]
        carry_ref[...] = carry
        for k in range(KV_A):
            acc = acc_ref[k]
            for j in range(n):
                acc = acc + jnp.dot(a_pages[j][k * rk:(k + 1) * rk, :], head_rows(vrefs[j], k),
                                    preferred_element_type=F32)
            acc_ref[k] = acc

    @pl.when(c == 0)
    def _():
        qs_ref[...] = (q_ref[...] * ((HD_A ** -0.5) * LOG2E)).astype(BF16)
        acc_ref[...] = jnp.zeros_like(acc_ref)
        carry_ref[...] = jnp.zeros_like(carry_ref)
        process([kn_ref], [vn_ref], True)

    process(k_refs, v_refs, False)

    @pl.when(c == pl.num_programs(1) - 1)
    def _():
        o_ref[...] = acc_ref[...]


def sb_decode(q, k_new, v_new, cache_k, cache_v, layer, page_table, sb_bias, *, pages):
    n_seq, _, rk, _ = q.shape
    n_pages = page_table.shape[1]
    prow = cache_k.shape[2]
    page = prow // KV_A
    grp = sb_bias.shape[0] // KV_A
    t_s = rk // grp
    n_chunks = n_pages // pages

    def page_spec(j):
        return pl.BlockSpec((None, None, prow, HD_A),
                            lambda b, c, pt: (layer, pt[b, n_pages - 1 - (c * pages + j)], 0, 0))

    blk4 = pl.BlockSpec((None, KV_A, rk, HD_A), lambda b, c, pt: (b, 0, 0, 0))
    new_spec = pl.BlockSpec((None, prow, HD_A), lambda b, c, pt: (b, 0, 0))
    grid_spec = pltpu.PrefetchScalarGridSpec(
        num_scalar_prefetch=1,
        grid=(n_seq, n_chunks),
        in_specs=([pl.BlockSpec(memory_space=pltpu.SMEM), blk4, new_spec, new_spec]
                  + [page_spec(j) for j in range(pages)] * 2
                  + [pl.BlockSpec((page, 2 * page), lambda b, c, pt: (0, 0))]),
        out_specs=blk4,
        scratch_shapes=[pltpu.VMEM((KV_A, rk, HD_A), BF16),
                        pltpu.VMEM((KV_A, rk, HD_A), F32),
                        pltpu.VMEM((KV_A * rk, page), F32)],
    )
    return pl.pallas_call(
        functools.partial(_sb_decode_kernel, pages=pages, grp=grp, t_s=t_s),
        grid_spec=grid_spec,
        out_shape=jax.ShapeDtypeStruct(q.shape, F32),
        compiler_params=_cparams(("arbitrary", "arbitrary")),
        name="sb_decode",
    )(page_table, sb_bias, q, k_new, v_new, *([cache_k] * pages), *([cache_v] * pages),
      _suffix_matrix(page, True))


def _t5_bucket_table(dist):
    d = np.maximum(dist, 0)
    large = MAX_EXACT + (np.log(np.maximum(d, 1).astype(np.float32) / np.float32(MAX_EXACT))
                         / np.float32(math.log(MAX_DISTANCE / MAX_EXACT))
                         * np.float32(N_BUCKETS - MAX_EXACT)).astype(np.int32)
    large = np.minimum(large, N_BUCKETS - 1)
    return np.where(d < MAX_EXACT, d, large).astype(np.int32)


def _win_prompt_kernel(rb_ref, sink_ref, q_ref, kvp_ref, kvc_ref, bkt_ref, o_ref, tbl_ref, sinkcol_ref, *, n_heads):
    b = pl.program_id(0)
    qi = pl.program_id(1)
    tq = q_ref.shape[0]
    grp = n_heads // KV_B
    kvw = KV_B * HD_B

    @pl.when((b == 0) & (qi == 0))
    def _():
        t_idx = lax.broadcasted_iota(jnp.int32, (tq, 2 * WINDOW), 0)
        s_idx = lax.broadcasted_iota(jnp.int32, (tq, 2 * WINDOW), 1)
        dist = t_idx + WINDOW - s_idx
        valid = (dist >= 0) & (dist < WINDOW)
        bucket = bkt_ref[...]

        def body(h, _):
            acc = jnp.zeros(bucket.shape, F32)
            for bb in range(N_BUCKETS):
                acc = jnp.where(bucket == bb, rb_ref[bb, h], acc)
            r0 = pl.multiple_of((h % grp) * tq, tq)
            tbl_ref[0, h // grp, pl.ds(r0, tq), :] = jnp.where(valid, acc, NEG)
            tbl_ref[1, h // grp, pl.ds(r0, tq), :] = jnp.where(valid & (s_idx >= WINDOW), acc, NEG)
            sinkcol_ref[h // grp, pl.ds(r0, tq), :] = jnp.full((tq, 1), sink_ref[h], F32)
            return 0
        lax.fori_loop(0, n_heads, body, 0)

    variant = (qi == 0).astype(jnp.int32)
    kv = jnp.concatenate([kvp_ref[...], kvc_ref[...]], axis=0).astype(BF16)
    kvs = range(KV_B)
    qgs = [jnp.concatenate([q_ref[:, (k * grp + g) * HD_B:(k * grp + g + 1) * HD_B] for g in range(grp)], axis=0)
           for k in kvs]
    ss = [lax.dot_general(qgs[k], kv[:, k * HD_B:(k + 1) * HD_B], (((1,), (1,)), ((), ())),
                          preferred_element_type=F32) + tbl_ref[variant, k] for k in kvs]
    ms = [jnp.maximum(jnp.max(ss[k], axis=-1, keepdims=True), sinkcol_ref[k]) for k in kvs]
    ps = [jnp.exp(ss[k] - ms[k]) for k in kvs]
    dens = [jnp.sum(ps[k], axis=-1, keepdims=True) + jnp.exp(sinkcol_ref[k] - ms[k]) for k in kvs]
    os_ = [jnp.dot(ps[k].astype(BF16), kv[:, kvw + k * HD_B:kvw + (k + 1) * HD_B], preferred_element_type=F32)
           / dens[k] for k in kvs]
    for k in kvs:
        for g in range(grp):
            h = k * grp + g
            o_ref[:, h * HD_B:(h + 1) * HD_B] = os_[k][g * tq:(g + 1) * tq, :].astype(o_ref.dtype)


def win_prompt(q, kv, sinks, rel_bias, batch, seq):
    n_heads = q.shape[1] // HD_B
    nq = seq // WINDOW
    t = np.arange(WINDOW)[:, None]
    s = np.arange(2 * WINDOW)[None, :]
    bucket = jnp.asarray(_t5_bucket_table(t + WINDOW - s))
    return pl.pallas_call(
        functools.partial(_win_prompt_kernel, n_heads=n_heads),
        grid=(batch, nq),
        in_specs=[pl.BlockSpec(memory_space=pltpu.SMEM),
                  pl.BlockSpec(memory_space=pltpu.SMEM),
                  pl.BlockSpec((WINDOW, q.shape[1]), lambda b, i: (b * nq + i, 0)),
                  pl.BlockSpec((WINDOW, kv.shape[1]), lambda b, i: (jnp.maximum(b * nq + i - 1, 0), 0)),
                  pl.BlockSpec((WINDOW, kv.shape[1]), lambda b, i: (b * nq + i, 0)),
                  pl.BlockSpec((WINDOW, 2 * WINDOW), lambda b, i: (0, 0))],
        out_specs=pl.BlockSpec((WINDOW, q.shape[1]), lambda b, i: (b * nq + i, 0)),
        out_shape=jax.ShapeDtypeStruct(q.shape, BF16),
        scratch_shapes=[pltpu.VMEM((2, KV_B, (n_heads // KV_B) * WINDOW, 2 * WINDOW), F32),
                        pltpu.VMEM((KV_B, (n_heads // KV_B) * WINDOW, 1), F32)],
        compiler_params=_cparams(("arbitrary", "arbitrary")),
        name="win_prompt",
    )(rel_bias, sinks, q, kv, kv, bucket)


def _win_decode_kernel(rb_ref, sink_ref, q_ref, k_ref, v_ref, bkt_ref, o_ref, tbl_ref, sinkcol_ref, *, grp, t_s):
    rk, nk = bkt_ref.shape
    n_sub = q_ref.shape[0]

    @pl.when(pl.program_id(0) == 0)
    def _():
        t_idx = lax.broadcasted_iota(jnp.int32, (rk, nk), 0) % t_s
        j_idx = lax.broadcasted_iota(jnp.int32, (rk, nk), 1)
        dist = t_idx + WINDOW - j_idx
        valid = (dist >= 0) & (dist < WINDOW)
        bucket = bkt_ref[...]
        row_g = lax.broadcasted_iota(jnp.int32, (rk, nk), 0) // t_s
        row = lax.broadcasted_iota(jnp.int32, (rk, 1), 0)
        for k in range(KV_B):
            def body(g, tbl):
                acc = jnp.zeros((rk, nk), F32)
                for bb in range(N_BUCKETS):
                    acc = jnp.where(bucket == bb, rb_ref[bb, k * grp + g], acc)
                return jnp.where(row_g == g, acc, tbl)
            tbl = lax.fori_loop(0, grp, body, jnp.zeros((rk, nk), F32))
            tbl_ref[k] = jnp.where(valid, tbl, NEG)
            sink = jnp.zeros((rk, 1), F32)
            for g in range(grp):
                sink = jnp.where(row // t_s == g, sink_ref[k * grp + g], sink)
            sinkcol_ref[k] = sink

    scale = HD_B ** -0.5
    items = [(s, k) for s in range(n_sub) for k in range(KV_B)]
    ss = [lax.dot_general(q_ref[s, k].astype(BF16), k_ref[s, k].astype(BF16), (((1,), (1,)), ((), ())),
                          preferred_element_type=F32) * scale + tbl_ref[k] for s, k in items]
    ms = [jnp.maximum(jnp.max(ss[n], axis=-1, keepdims=True), sinkcol_ref[k]) for n, (s, k) in enumerate(items)]
    ps = [jnp.exp(ss[n] - ms[n]) for n in range(len(items))]
    dens = [jnp.sum(ps[n], axis=-1, keepdims=True) + jnp.exp(sinkcol_ref[k] - ms[n])
            for n, (s, k) in enumerate(items)]
    for n, (s, k) in enumerate(items):
        o_ref[s, k] = jnp.dot(ps[n].astype(BF16), v_ref[s, k].astype(BF16), preferred_element_type=F32) / dens[n]


def win_decode(q, k, v, sinks, rel_bias, t_s):
    n_seq, _, rk, _ = q.shape
    nk = k.shape[2]
    grp = rk // t_s
    n_sub = WIN_DECODE_SEQS
    t = (np.arange(rk) % t_s)[:, None]
    j = np.arange(nk)[None, :]
    bucket = jnp.asarray(_t5_bucket_table(t + WINDOW - j))
    blk_q = pl.BlockSpec((n_sub, KV_B, rk, HD_B), lambda b: (b, 0, 0, 0))
    blk_k = pl.BlockSpec((n_sub, KV_B, nk, HD_B), lambda b: (b, 0, 0, 0))
    return pl.pallas_call(
        functools.partial(_win_decode_kernel, grp=grp, t_s=t_s),
        grid=(n_seq // n_sub,),
        in_specs=[pl.BlockSpec(memory_space=pltpu.SMEM),
                  pl.BlockSpec(memory_space=pltpu.SMEM),
                  blk_q, blk_k, blk_k,
                  pl.BlockSpec((rk, nk), lambda b: (0, 0))],
        out_specs=blk_q,
        out_shape=jax.ShapeDtypeStruct(q.shape, F32),
        scratch_shapes=[pltpu.VMEM((KV_B, rk, nk), F32),
                        pltpu.VMEM((KV_B, rk, 1), F32)],
        compiler_params=_cparams(("arbitrary",)),
        name="win_decode",
    )(rel_bias, sinks, q, k, v, bucket)


def _heads_to_rows(x, n_seq, t_s, n_kv, hd):
    grp = x.shape[1] // (n_kv * hd)
    x = x.reshape(n_seq, t_s, n_kv, grp, hd).transpose(0, 2, 3, 1, 4)
    return x.reshape(n_seq, n_kv, grp * t_s, hd)


def _rows_to_heads(x, t_s):
    n_seq, n_kv, rk, hd = x.shape
    grp = rk // t_s
    x = x.reshape(n_seq, n_kv, grp, t_s, hd).transpose(0, 3, 1, 2, 4)
    return x.reshape(n_seq * t_s, n_kv * grp * hd)


def kernel(x_prompt, x_sample, cache_k_a, cache_v_a, page_table, cache_k_b, cache_v_b, state_conv,
           w_qkv_a, w_o_a, sb_bias, g_kv, w_kv_b, w_q_b, w_o_b, sinks_b, rel_bias,
           g_pre_mix, g_post_mix, g_pre_ffn, g_post_ffn, w_gate, w_up, w_down, conv_w, conv_b):
    batch, seq, d = x_prompt.shape
    n_seq, t_s, _ = x_sample.shape
    n_a = w_qkv_a.shape[0]
    depth = w_gate.shape[0]
    d_ff = w_gate.shape[2]
    page = cache_k_a.shape[2]
    h_a = w_o_a.shape[1] // HD_A
    mp, ms = batch * seq, n_seq * t_s
    qw = h_a * HD_A
    kvw_a = KV_A * HD_A
    kvw_b = KV_B * HD_B
    assert t_s >= CONV_W - 1 and seq % TM_PROMPT == 0

    cache_k4 = cache_k_a.reshape(cache_k_a.shape[0], cache_k_a.shape[1], page * KV_A, HD_A)
    cache_v4 = cache_v_a.reshape(cache_v_a.shape[0], cache_v_a.shape[1], page * KV_A, HD_A)
    w_kv_b3 = w_kv_b.reshape(1, d, 2 * kvw_b)

    hp = x_prompt.reshape(mp, d)
    hs = x_sample.reshape(ms, d)
    ka_s, va_s, conv_p, conv_s = [], [], [], []
    kv_all = kv_p = kb_s = vb_s = kq_s = vq_s = None
    for l in range(depth):
        if l < n_a:
            q_p, k_all, v_all = norm_qkv(hp, g_pre_mix[l], w_qkv_a, l, TM_PROMPT, (HD_A ** -0.5) * LOG2E, kv_all)
            kv_all = (k_all, v_all)
            qkv_s = norm_matmul(hs, g_pre_mix[l], w_qkv_a, l, ms, TN_PROJ)
            op = sb_prompt(q_p, k_all, v_all, l, sb_bias[l], batch, seq)
            q_s = _heads_to_rows(qkv_s[:, :qw], n_seq, t_s, KV_A, HD_A)
            k_new = qkv_s[:, qw:qw + kvw_a]
            v_new = qkv_s[:, qw + kvw_a:]
            pad = ((0, 0), (0, (page - t_s) * KV_A), (0, 0))
            o_s = sb_decode(q_s, jnp.pad(k_new.reshape(n_seq, t_s * KV_A, HD_A), pad),
                            jnp.pad(v_new.reshape(n_seq, t_s * KV_A, HD_A), pad),
                            cache_k4, cache_v4, l, page_table, sb_bias[l], pages=SB_PAGES_PER_STEP)
            o_s = _rows_to_heads(o_s, t_s)
            hp = proj_res(op, w_o_a, l, g_post_mix[l], hp, TM_PROJ_RES)
            hs = proj_res(o_s, w_o_a, l, g_post_mix[l], hs, ms)
            ka_s.append(k_new.reshape(n_seq, t_s, KV_A, HD_A))
            va_s.append(v_new.reshape(n_seq, t_s, KV_A, HD_A))
        else:
            j = l - n_a
            if j == 0:
                kv_p = norm_matmul(hp, g_kv, w_kv_b3, 0, TM_PROMPT, 2 * kvw_b)
                kv_s = norm_matmul(hs, g_kv, w_kv_b3, 0, ms, 2 * kvw_b)
                kb_s = jnp.concatenate([cache_k_b, kv_s[:, :kvw_b].reshape(n_seq, t_s, KV_B, HD_B)], axis=1)
                vb_s = jnp.concatenate([cache_v_b, kv_s[:, kvw_b:].reshape(n_seq, t_s, KV_B, HD_B)], axis=1)
                padk = ((0, 0), (0, 0), (0, 2 * WINDOW - (WINDOW + t_s)), (0, 0))
                kq_s = jnp.pad(kb_s.transpose(0, 2, 1, 3), padk)
                vq_s = jnp.pad(vb_s.transpose(0, 2, 1, 3), padk)
            q_p = norm_matmul(hp, g_pre_mix[l], w_q_b, j, TM_PROMPT, TN_PROJ, BF16, HD_B ** -0.5)
            q_s = norm_matmul(hs, g_pre_mix[l], w_q_b, j, ms, TN_PROJ)
            op = win_prompt(q_p, kv_p, sinks_b[j], rel_bias, batch, seq)
            o_s = win_decode(_heads_to_rows(q_s, n_seq, t_s, KV_B, HD_B), kq_s, vq_s, sinks_b[j], rel_bias, t_s)
            o_s = _rows_to_heads(o_s, t_s)
            hp = proj_res(op, w_o_b, j, g_post_mix[l], hp, TM_PROJ_RES)
            hs = proj_res(o_s, w_o_b, j, g_post_mix[l], hs, ms)
        st = state_conv[l]
        zeros = jnp.zeros((n_seq, t_s - 1, d_ff), F32)
        s1 = jnp.concatenate([st[:, 1:2], zeros], axis=1).reshape(ms, d_ff)
        s2 = jnp.concatenate([st, zeros[:, 1:]], axis=1).reshape(ms, d_ff)
        hs, gt_s, wg16, wu16, wd16 = conv_ffn(hs, g_pre_ffn, g_post_ffn, w_gate, w_up, w_down, conv_w, conv_b, l,
                                              tm=ms, tf=TF_SAMPLE, seq_len=t_s, n_split=1, state_rows=(s1, s2))
        hp, gt_p = conv_ffn(hp, g_pre_ffn, g_post_ffn, wg16, wu16, wd16, conv_w, conv_b, l,
                            tm=TM_PROMPT, tf=TF_PROMPT, seq_len=seq, n_split=FFN_ROW_GROUPS)
        tiles_per_seq = seq // TM_PROMPT
        conv_p.append(gt_p.reshape(batch, tiles_per_seq, 8, d_ff)[:, -1, 8 - (CONV_W - 1):, :])
        conv_s.append(gt_s.reshape(n_seq, t_s, d_ff)[:, t_s - (CONV_W - 1):, :])

    kvp4 = kv_p.reshape(batch, seq, 2 * kvw_b)[:, seq - WINDOW:].reshape(batch, WINDOW, 2, KV_B, HD_B)
    return (hp.reshape(batch, seq, d), hs.reshape(n_seq, t_s, d),
            kv_all[0].reshape(n_a, batch, seq, KV_A, HD_A), kv_all[1].reshape(n_a, batch, seq, KV_A, HD_A),
            jnp.stack(ka_s), jnp.stack(va_s),
            kvp4[:, :, 0], kvp4[:, :, 1],
            kb_s[:, -WINDOW:], vb_s[:, -WINDOW:],
            jnp.stack(conv_p), jnp.stack(conv_s))
```

```python
import functools
import math

import numpy as np
import jax
import jax.numpy as jnp
from jax import lax
from jax.experimental import pallas as pl
from jax.experimental.pallas import tpu as pltpu

F32 = jnp.float32
BF16 = jnp.bfloat16

EPS = 1e-6
HD_A = 128
KV_A = 4
HD_B = 64
KV_B = 4
WINDOW = 128
Q_BLOCK = 128
N_BUCKETS = 32
MAX_EXACT = N_BUCKETS // 2
MAX_DISTANCE = WINDOW
CONV_W = 3
NEG = -1e30
LOG2E = math.log2(math.e)
SB_KEY_BLOCK = 256
SB_PAGES_PER_STEP = 16
VMEM_LIMIT = 62 * 1024 * 1024
TM_PROMPT = 1024
TF_PROMPT = 512
FFN_ROW_GROUPS = 8
TF_SAMPLE = 512
TN_PROJ = 512
TM_PROJ_RES = 512
NORM_GROUP_ROWS = 256
WIN_DECODE_SEQS = 8


def _cparams(sem):
    return pltpu.CompilerParams(dimension_semantics=sem, vmem_limit_bytes=VMEM_LIMIT)


def _rmsnorm_rows(x, g):
    ms = jnp.mean(x * x, axis=-1, keepdims=True)
    return x * lax.rsqrt(ms + EPS) * g


def _normed_tile_times_w(h_ref, g_ref, w_ref, hn_ref, wb_ref, emit):
    i = pl.program_id(0)
    j = pl.program_id(1)
    tm = h_ref.shape[0]
    n_split = max(1, tm // NORM_GROUP_ROWS)
    th = tm // n_split

    @pl.when(i == 0)
    def _():
        wb_ref[j] = w_ref[...].astype(BF16)

    @pl.when(j == 0)
    def _():
        ys = []
        for r in range(n_split):
            hn = _rmsnorm_rows(h_ref[r * th:(r + 1) * th, :], g_ref[...]).astype(BF16)
            hn_ref[r * th:(r + 1) * th, :] = hn
            ys.append(jnp.dot(hn, wb_ref[j], preferred_element_type=F32))
        emit(ys[0] if n_split == 1 else jnp.concatenate(ys, axis=0))

    @pl.when(j > 0)
    def _():
        emit(jnp.dot(hn_ref[...], wb_ref[j], preferred_element_type=F32))


def _resident_w_spec(layer, d, tn, n_tiles):
    return pl.BlockSpec((None, d, tn), lambda i, j: (layer, 0, jnp.where(i == 0, j, n_tiles - 1)))


def _norm_matmul_kernel(h_ref, g_ref, w_ref, o_ref, hn_ref, wb_ref, *, scale):
    def emit(y):
        o_ref[...] = (y if scale is None else y * scale).astype(o_ref.dtype)

    _normed_tile_times_w(h_ref, g_ref, w_ref, hn_ref, wb_ref, emit)


def norm_matmul(h, g, w, layer, tm, tn, out_dtype=F32, scale=None):
    m, d = h.shape
    n = w.shape[2]
    return pl.pallas_call(
        functools.partial(_norm_matmul_kernel, scale=scale),
        grid=(m // tm, n // tn),
        in_specs=[pl.BlockSpec((tm, d), lambda i, j: (i, 0)),
                  pl.BlockSpec((1, d), lambda i, j: (0, 0)),
                  _resident_w_spec(layer, d, tn, n // tn)],
        out_specs=pl.BlockSpec((tm, tn), lambda i, j: (i, j)),
        out_shape=jax.ShapeDtypeStruct((m, n), out_dtype),
        scratch_shapes=[pltpu.VMEM((tm, d), BF16),
                        pltpu.VMEM((n // tn, d, tn), BF16)],
        compiler_params=_cparams(("arbitrary", "arbitrary")),
        name="norm_matmul",
    )(h, g.reshape(1, d), w)


def _norm_qkv_kernel(*refs, n_q_tiles, scale, aliased):
    if aliased:
        h_ref, g_ref, w_ref, _, _, q_ref, k_ref, v_ref, hn_ref, wb_ref = refs
    else:
        h_ref, g_ref, w_ref, q_ref, k_ref, v_ref, hn_ref, wb_ref = refs
    j = pl.program_id(1)
    tm = h_ref.shape[0]
    def emit(y):
        @pl.when(j < n_q_tiles)
        def _():
            q_ref[...] = (y * scale).astype(q_ref.dtype)

        def rows_out(ref):
            for kh in range(KV_A):
                ref[pl.ds(kh, tm, stride=KV_A), :] = y[:, kh * HD_A:(kh + 1) * HD_A]

        @pl.when(j == n_q_tiles)
        def _():
            rows_out(k_ref)

        @pl.when(j == n_q_tiles + 1)
        def _():
            rows_out(v_ref)

    _normed_tile_times_w(h_ref, g_ref, w_ref, hn_ref, wb_ref, emit)


def norm_qkv(h, g, w, layer, tm, q_scale, kv_prev=None):
    m, d = h.shape
    nl = w.shape[0]
    tn = KV_A * HD_A
    qw = w.shape[2] - 2 * tn
    nqt = qw // tn
    aliased = kv_prev is not None
    kv_shape = jax.ShapeDtypeStruct((nl, m * KV_A, HD_A), F32)
    kv_spec = pl.BlockSpec((None, tm * KV_A, HD_A), lambda i, j: (layer, i, 0))
    in_specs = [pl.BlockSpec((tm, d), lambda i, j: (i, 0)),
                pl.BlockSpec((1, d), lambda i, j: (0, 0)),
                _resident_w_spec(layer, d, tn, nqt + 2)]
    args = [h, g.reshape(1, d), w]
    if aliased:
        in_specs += [pl.BlockSpec(memory_space=pl.ANY)] * 2
        args += list(kv_prev)
    return pl.pallas_call(
        functools.partial(_norm_qkv_kernel, n_q_tiles=nqt, scale=q_scale, aliased=aliased),
        grid=(m // tm, nqt + 2),
        in_specs=in_specs,
        out_specs=[pl.BlockSpec((tm, tn), lambda i, j: (i, jnp.minimum(j, nqt - 1))), kv_spec, kv_spec],
        out_shape=[jax.ShapeDtypeStruct((m, qw), BF16), kv_shape, kv_shape],
        scratch_shapes=[pltpu.VMEM((tm, d), BF16),
                        pltpu.VMEM((nqt + 2, d, tn), BF16)],
        input_output_aliases={3: 1, 4: 2} if aliased else {},
        compiler_params=_cparams(("arbitrary", "arbitrary")),
        name="norm_qkv",
    )(*args)


def _proj_res_kernel(o_ref, w_ref, g_ref, h_ref, out_ref, wb_ref):
    @pl.when(pl.program_id(0) == 0)
    def _():
        wb_ref[...] = w_ref[...].astype(BF16)

    y = jnp.dot(o_ref[...].astype(BF16), wb_ref[...], preferred_element_type=F32)
    out_ref[...] = h_ref[...] + _rmsnorm_rows(y, g_ref[...])


def proj_res(o, w, layer, g, h, tm):
    m, k = o.shape
    d = w.shape[2]
    return pl.pallas_call(
        _proj_res_kernel,
        grid=(m // tm,),
        in_specs=[pl.BlockSpec((tm, k), lambda i: (i, 0)),
                  pl.BlockSpec((None, k, d), lambda i: (layer, 0, 0), pipeline_mode=pl.Buffered(1)),
                  pl.BlockSpec((1, d), lambda i: (0, 0)),
                  pl.BlockSpec((tm, d), lambda i: (i, 0))],
        out_specs=pl.BlockSpec((tm, d), lambda i: (i, 0)),
        out_shape=jax.ShapeDtypeStruct((m, d), F32),
        scratch_shapes=[pltpu.VMEM((k, d), BF16)],
        compiler_params=_cparams(("arbitrary",)),
        name="proj_res",
    )(o, w, g.reshape(1, d), h)


def _gelu_tanh(x):
    c = math.sqrt(2.0 / math.pi)
    return 0.5 * x * (1.0 + jnp.tanh(c * (x + 0.044715 * (x * x * x))))


def _conv_ffn_kernel(*refs, tm, seq_len, has_state, emit_w, tail, n_split):
    h_ref, gpre_ref, gpost_ref, wg_ref, wu_ref, wd_ref, cw_ref, cb_ref = refs[:8]
    refs = refs[8:]
    if has_state:
        s1_ref, s2_ref = refs[:2]
        refs = refs[2:]
    out_ref, gt_ref = refs[:2]
    refs = refs[2:]
    if emit_w:
        wg16_ref, wu16_ref, wd16_ref = refs[:3]
        refs = refs[3:]
    hn_ref, gp_ref, carry_ref = refs
    i = pl.program_id(0)
    j = pl.program_id(1)

    @pl.when(i == 0)
    def _():
        carry_ref[j] = jnp.zeros(carry_ref.shape[1:], F32)

    th = tm // n_split

    def step(first, last):
        wg = wg_ref[...].astype(BF16)
        wu = wu_ref[...].astype(BF16)
        wd = wd_ref[...].astype(BF16)
        if emit_w:
            wg16_ref[...] = wg
            wu16_ref[...] = wu
            wd16_ref[...] = wd
        cw = cw_ref[...]
        cb = cb_ref[...]
        gp_ref[0:8, :] = carry_ref[j]
        gs, us = [], []
        for r in range(n_split):
            r0 = r * th
            if first:
                hn = _rmsnorm_rows(h_ref[r0:r0 + th, :], gpre_ref[...]).astype(BF16)
                hn_ref[r0:r0 + th, :] = hn
            else:
                hn = hn_ref[r0:r0 + th, :]
            g = jnp.dot(hn, wg, preferred_element_type=F32)
            us.append(jnp.dot(hn, wu, preferred_element_type=F32))
            gs.append(g)
            gp_ref[8 + r0:8 + r0 + th, :] = g
        carry_ref[j] = gs[-1][th - 8:, :]
        if tail == tm:
            for r in range(n_split):
                gt_ref[r * th:(r + 1) * th, :] = gs[r]
        else:
            gt_ref[...] = gs[-1][th - tail:, :]
        for r in range(n_split):
            r0 = r * th
            g, u = gs[r], us[r]
            g1 = gp_ref[7 + r0:7 + r0 + th, :]
            g2 = gp_ref[6 + r0:6 + r0 + th, :]
            pos = (i * tm + r0 + lax.broadcasted_iota(jnp.int32, (th, 1), 0)) % seq_len
            if has_state:
                g1 = jnp.where(pos >= 1, g1, s1_ref[r0:r0 + th, :])
                g2 = jnp.where(pos >= 2, g2, s2_ref[r0:r0 + th, :])
            else:
                g1 = jnp.where(pos >= 1, g1, 0.0)
                g2 = jnp.where(pos >= 2, g2, 0.0)
            gc = cb + g2 * cw[0:1, :] + g1 * cw[1:2, :] + g * cw[2:3, :]
            y = (_gelu_tanh(gc) * u).astype(BF16)
            acc = jnp.dot(y, wd, preferred_element_type=F32)
            if not first:
                acc = out_ref[r0:r0 + th, :] + acc
            if last:
                acc = h_ref[r0:r0 + th, :] + _rmsnorm_rows(acc, gpost_ref[...])
            out_ref[r0:r0 + th, :] = acc

    nj = pl.num_programs(1)
    pl.when(j == 0)(lambda: step(True, False))
    pl.when((j > 0) & (j < nj - 1))(lambda: step(False, False))
    pl.when(j == nj - 1)(lambda: step(False, True))


def conv_ffn(h, g_pre, g_post, wg, wu, wd, cw, cb, layer, *, tm, tf, seq_len, n_split, state_rows=None):
    m, d = h.shape
    f = wg.shape[2]
    has_state = state_rows is not None
    emit_w = wg.dtype == F32
    tail = tm if has_state else 8
    ni, nj = m // tm, f // tf
    assert (not emit_w or ni == 1) and nj >= 2
    in_specs = [pl.BlockSpec((tm, d), lambda i, j: (i, 0)),
                pl.BlockSpec((None, 1, d), lambda i, j: (layer, 0, 0)),
                pl.BlockSpec((None, 1, d), lambda i, j: (layer, 0, 0)),
                pl.BlockSpec((None, d, tf), lambda i, j: (wl, 0, j)),
                pl.BlockSpec((None, d, tf), lambda i, j: (wl, 0, j)),
                pl.BlockSpec((None, tf, d), lambda i, j: (wl, j, 0)),
                pl.BlockSpec((None, CONV_W, tf), lambda i, j: (layer, 0, j)),
                pl.BlockSpec((None, 1, tf), lambda i, j: (layer, 0, j))]
    nl = g_pre.shape[0]
    wl = layer if wg.shape[0] == nl else 0
    args =[h, g_pre.reshape(nl, 1, d), g_post.reshape(nl, 1, d), wg, wu, wd, cw, cb.reshape(nl, 1, f)]
    if has_state:
        in_specs += [pl.BlockSpec((tm, tf), lambda i, j: (i, j))] * 2
        args += list(state_rows)
    out_specs = [pl.BlockSpec((tm, d), lambda i, j: (i, 0)),
                 pl.BlockSpec((None, tail, tf), lambda i, j: (i, 0, j))]
    out_shape = [jax.ShapeDtypeStruct((m, d), F32),
                 jax.ShapeDtypeStruct((ni, tail, f), F32)]
    if emit_w:
        out_specs += [pl.BlockSpec((None, d, tf), lambda i, j: (0, 0, j)),
                      pl.BlockSpec((None, d, tf), lambda i, j: (0, 0, j)),
                      pl.BlockSpec((None, tf, d), lambda i, j: (0, j, 0))]
        out_shape += [jax.ShapeDtypeStruct((1, d, f), BF16), jax.ShapeDtypeStruct((1, d, f), BF16),
                      jax.ShapeDtypeStruct((1, f, d), BF16)]
    return pl.pallas_call(
        functools.partial(_conv_ffn_kernel, tm=tm, seq_len=seq_len, has_state=has_state, emit_w=emit_w,
                          tail=tail, n_split=n_split),
        grid=(ni, nj),
        in_specs=in_specs,
        out_specs=out_specs,
        out_shape=out_shape,
        scratch_shapes=[pltpu.VMEM((tm, d), BF16),
                        pltpu.VMEM((tm + 8, tf), F32),
                        pltpu.VMEM((nj, 8, tf), F32)],
        compiler_params=_cparams(("arbitrary", "arbitrary")),
        name="conv_ffn",
    )(*args)


def _sb_logs(z, valid):
    nz = -z
    sp = jnp.log2(1.0 + jnp.exp2(jnp.minimum(z, nz)))
    log_stay = jnp.minimum(nz, 0.0) - sp
    log_beta = z + log_stay
    if valid is not None:
        log_stay = jnp.where(valid, log_stay, 0.0)
    return log_stay, log_beta


def _suffix_matrix(n, with_total):
    u = (np.arange(n)[:, None] > np.arange(n)[None, :]).astype(np.float32)
    if with_total:
        u = np.concatenate([u, np.ones((n, n), np.float32)], axis=1)
    return jnp.asarray(u, dtype=BF16)


def _sb_prompt_kernel(bias_ref, q_ref, k_ref, v_ref, u2_ref, o_ref, kb_ref, vb_ref, qs_ref, acc_ref, *, grp):
    qi = pl.program_id(1)
    tq = q_ref.shape[0]
    seq = kb_ref.shape[0]
    tk = SB_KEY_BLOCK
    rows = grp * tq
    heads = range(KV_A)

    kw = 2 * HD_A
    lane_k = lax.broadcasted_iota(jnp.int32, (seq, HD_A), 1)
    lane_q = lax.broadcasted_iota(jnp.int32, (rows, HD_A), 1)

    @pl.when(qi == 0)
    def _():
        ones_cols = jnp.where(lane_k < 2, 1.0, 0.0).astype(BF16)
        for hh in heads:
            kb_ref[:, hh * kw:hh * kw + HD_A] = k_ref[pl.ds(hh, seq, stride=KV_A), :].astype(BF16)
            kb_ref[:, hh * kw + HD_A:(hh + 1) * kw] = ones_cols
            vb_ref[:, hh * HD_A:(hh + 1) * HD_A] = v_ref[pl.ds(hh, seq, stride=KV_A), :].astype(BF16)

    row = lax.broadcasted_iota(jnp.int32, (rows, 1), 0)
    for hh in heads:
        for g in range(grp):
            c0 = (hh * grp + g) * HD_A
            qs_ref[hh, g * tq:(g + 1) * tq, 0:HD_A] = q_ref[:, c0:c0 + HD_A]
        bias = jnp.zeros((rows, 1), F32)
        for g in range(grp):
            bias = jnp.where(row // tq == g, bias_ref[hh * grp + g] * LOG2E, bias)
        hi = bias.astype(BF16).astype(F32)
        lo = (bias - hi).astype(BF16).astype(F32)
        qs_ref[hh, :, HD_A:kw] = jnp.where(lane_q == 0, hi, jnp.where(lane_q == 1, lo, 0.0)).astype(BF16)
    u2 = u2_ref[...]
    acc_ref[...] = jnp.zeros_like(acc_ref)

    def step(kb, n_blk, carries, masked):
        start = pl.multiple_of(kb * tk, tk)
        width = n_blk * tk
        valid = None
        if masked:
            q_pos = qi * tq + row % tq
            k_pos = kb * tk + lax.broadcasted_iota(jnp.int32, (rows, width), 1)
            valid = k_pos < q_pos
        zs = [lax.dot_general(qs_ref[hh], kb_ref[pl.ds(start, width), hh * kw:(hh + 1) * kw],
                              (((1,), (1,)), ((), ())), preferred_element_type=F32) for hh in heads]
        logs = [_sb_logs(zs[hh], valid) for hh in heads]
        laters, out = [], []
        for hh in heads:
            stay16 = logs[hh][0].astype(BF16)
            carry = carries[hh]
            parts = [None] * n_blk
            for blk in reversed(range(n_blk)):
                sl = slice(blk * tk, (blk + 1) * tk)
                parts[blk] = jnp.dot(stay16[:, sl], u2, preferred_element_type=F32) + carry
                carry = carry + jnp.sum(logs[hh][0][:, sl], axis=1, keepdims=True)
            laters.append(parts[0] if n_blk == 1 else jnp.concatenate(parts, axis=1))
            out.append(carry)
        for hh in heads:
            a = jnp.exp2(logs[hh][1] + laters[hh])
            if masked:
                a = jnp.where(valid, a, 0.0)
            vblk = vb_ref[pl.ds(start, width), hh * HD_A:(hh + 1) * HD_A]
            acc_ref[hh] += jnp.dot(a.astype(BF16), vblk, preferred_element_type=F32)
        return tuple(out)

    kb_diag = (qi * tq) // tk
    odd = kb_diag % 2
    carries = step(kb_diag, 1, tuple(jnp.zeros((rows, 1), F32) for _ in heads), True)
    carries = lax.fori_loop(0, odd, lambda n, c: step(kb_diag - 1, 1, c, False), carries)
    lax.fori_loop(0, kb_diag // 2, lambda n, c: step(kb_diag - odd - 2 - 2 * n, 2, c, False), carries)
    for hh in heads:
        for g in range(grp):
            c0 = (hh * grp + g) * HD_A
            o_ref[:, c0:c0 + HD_A] = acc_ref[hh, g * tq:(g + 1) * tq, :].astype(o_ref.dtype)


def sb_prompt(q, k_all, v_all, layer, sb_bias, batch, seq):
    qw = q.shape[1]
    grp = qw // (KV_A * HD_A)
    nq = seq // Q_BLOCK
    kv_spec = pl.BlockSpec((None, seq * KV_A, HD_A), lambda b, i: (layer, b, 0))
    return pl.pallas_call(
        functools.partial(_sb_prompt_kernel, grp=grp),
        grid=(batch, nq),
        in_specs=[pl.BlockSpec(memory_space=pltpu.SMEM),
                  pl.BlockSpec((Q_BLOCK, qw), lambda b, i: (b * nq + i, 0)),
                  kv_spec, kv_spec,
                  pl.BlockSpec((SB_KEY_BLOCK, SB_KEY_BLOCK), lambda b, i: (0, 0))],
        out_specs=pl.BlockSpec((Q_BLOCK, qw), lambda b, i: (b * nq + i, 0)),
        out_shape=jax.ShapeDtypeStruct((batch * seq, qw), BF16),
        scratch_shapes=[pltpu.VMEM((seq, KV_A * 2 * HD_A), BF16),
                        pltpu.VMEM((seq, KV_A * HD_A), BF16),
                        pltpu.VMEM((KV_A, grp * Q_BLOCK, 2 * HD_A), BF16),
                        pltpu.VMEM((KV_A, grp * Q_BLOCK, HD_A), F32)],
        compiler_params=_cparams(("arbitrary", "arbitrary")),
        name="sb_prompt",
    )(sb_bias, q, k_all, v_all, _suffix_matrix(SB_KEY_BLOCK, False))


def _sb_decode_kernel(pt_ref, bias_ref, q_ref, kn_ref, vn_ref, *rest, pages, grp, t_s):
    k_refs = rest[:pages]
    v_refs = rest[pages:2 * pages]
    u2_ref, o_ref, qs_ref, acc_ref, carry_ref = rest[2 * pages:]
    c = pl.program_id(1)
    rk = grp * t_s
    rows = KV_A * rk
    page = u2_ref.shape[0]
    row = lax.broadcasted_iota(jnp.int32, (rows, 1), 0)
    bias = jnp.zeros((rows, 1), F32)
    for hh in range(KV_A * grp):
        bias = jnp.where(row // t_s == hh, bias_ref[hh] * LOG2E, bias)

    def head_rows(ref, k):
        return ref[pl.ds(k, page, stride=KV_A), :].astype(BF16)

    def process(krefs, vrefs, masked):
        n = len(krefs)
        zs = []
        for kr in krefs:
            zk = [lax.dot_general(qs_ref[k], head_rows(kr, k), (((1,), (1,)), ((), ())),
                                  preferred_element_type=F32) for k in range(KV_A)]
            zs.append(jnp.concatenate(zk, axis=0))
        z = jnp.concatenate(zs, axis=1) + bias
        valid = None
        if masked:
            valid = lax.broadcasted_iota(jnp.int32, (rows, n * page), 1) < row % t_s
        log_stay, log_beta = _sb_logs(z, valid)
        stay16 = log_stay.astype(BF16)
        u2 = u2_ref[...]
        carry = carry_ref[...]
        a_pages = []
        for j in range(n):
            sl = slice(j * page, (j + 1) * page)
            r = jnp.dot(stay16[:, sl], u2, preferred_element_type=F32)
            a = jnp.exp2(log_beta[:, sl] + r[:, :page] + carry)
            if masked:
                a = jnp.where(valid[:, sl], a, 0.0)
            a_pages.append(a.astype(BF16))
            carry = carry + r[:, page:]
        carry_ref[...] = carry
        for k in range(KV_A):
            acc = acc_ref[k]
            for j in range(n):
                acc = acc + jnp.dot(a_pages[j][k * rk:(k + 1) * rk, :], head_rows(vrefs[j], k),
                                    preferred_element_type=F32)
            acc_ref[k] = acc

    @pl.when(c == 0)
    def _():
        qs_ref[...] = (q_ref[...] * ((HD_A ** -0.5) * LOG2E)).astype(BF16)
        acc_ref[...] = jnp.zeros_like(acc_ref)
        carry_ref[...] = jnp.zeros_like(carry_ref)
        process([kn_ref], [vn_ref], True)

    process(k_refs, v_refs, False)

    @pl.when(c == pl.num_programs(1) - 1)
    def _():
        o_ref[...] = acc_ref[...]


def sb_decode(q, k_new, v_new, cache_k, cache_v, layer, page_table, sb_bias, *, pages):
    n_seq, _, rk, _ = q.shape
    n_pages = page_table.shape[1]
    prow = cache_k.shape[2]
    page = prow // KV_A
    grp = sb_bias.shape[0] // KV_A
    t_s = rk // grp
    n_chunks = n_pages // pages

    def page_spec(j):
        return pl.BlockSpec((None, None, prow, HD_A),
                            lambda b, c, pt: (layer, pt[b, n_pages - 1 - (c * pages + j)], 0, 0))

    blk4 = pl.BlockSpec((None, KV_A, rk, HD_A), lambda b, c, pt: (b, 0, 0, 0))
    new_spec = pl.BlockSpec((None, prow, HD_A), lambda b, c, pt: (b, 0, 0))
    grid_spec = pltpu.PrefetchScalarGridSpec(
        num_scalar_prefetch=1,
        grid=(n_seq, n_chunks),
        in_specs=([pl.BlockSpec(memory_space=pltpu.SMEM), blk4, new_spec, new_spec]
                  + [page_spec(j) for j in range(pages)] * 2
                  + [pl.BlockSpec((page, 2 * page), lambda b, c, pt: (0, 0))]),
        out_specs=blk4,
        scratch_shapes=[pltpu.VMEM((KV_A, rk, HD_A), BF16),
                        pltpu.VMEM((KV_A, rk, HD_A), F32),
                        pltpu.VMEM((KV_A * rk, page), F32)],
    )
    return pl.pallas_call(
        functools.partial(_sb_decode_kernel, pages=pages, grp=grp, t_s=t_s),
        grid_spec=grid_spec,
        out_shape=jax.ShapeDtypeStruct(q.shape, F32),
        compiler_params=_cparams(("arbitrary", "arbitrary")),
        name="sb_decode",
    )(page_table, sb_bias, q, k_new, v_new, *([cache_k] * pages), *([cache_v] * pages),
      _suffix_matrix(page, True))


def _t5_bucket_table(dist):
    d = np.maximum(dist, 0)
    large = MAX_EXACT + (np.log(np.maximum(d, 1).astype(np.float32) / np.float32(MAX_EXACT))
                         / np.float32(math.log(MAX_DISTANCE / MAX_EXACT))
                         * np.float32(N_BUCKETS - MAX_EXACT)).astype(np.int32)
    large = np.minimum(large, N_BUCKETS - 1)
    return np.where(d < MAX_EXACT, d, large).astype(np.int32)


def _win_prompt_kernel(rb_ref, sink_ref, q_ref, kvp_ref, kvc_ref, bkt_ref, o_ref, tbl_ref, sinkcol_ref, *, n_heads):
    b = pl.program_id(0)
    qi = pl.program_id(1)
    tq = q_ref.shape[0]
    grp = n_heads // KV_B
    kvw = KV_B * HD_B

    @pl.when((b == 0) & (qi == 0))
    def _():
        t_idx = lax.broadcasted_iota(jnp.int32, (tq, 2 * WINDOW), 0)
        s_idx = lax.broadcasted_iota(jnp.int32, (tq, 2 * WINDOW), 1)
        dist = t_idx + WINDOW - s_idx
        valid = (dist >= 0) & (dist < WINDOW)
        bucket = bkt_ref[...]

        def body(h, _):
            acc = jnp.zeros(bucket.shape, F32)
            for bb in range(N_BUCKETS):
                acc = jnp.where(bucket == bb, rb_ref[bb, h], acc)
            r0 = pl.multiple_of((h % grp) * tq, tq)
            tbl_ref[0, h // grp, pl.ds(r0, tq), :] = jnp.where(valid, acc, NEG)
            tbl_ref[1, h // grp, pl.ds(r0, tq), :] = jnp.where(valid & (s_idx >= WINDOW), acc, NEG)
            sinkcol_ref[h // grp, pl.ds(r0, tq), :] = jnp.full((tq, 1), sink_ref[h], F32)
            return 0
        lax.fori_loop(0, n_heads, body, 0)

    variant = (qi == 0).astype(jnp.int32)
    kv = jnp.concatenate([kvp_ref[...], kvc_ref[...]], axis=0).astype(BF16)
    kvs = range(KV_B)
    qgs = [jnp.concatenate([q_ref[:, (k * grp + g) * HD_B:(k * grp + g + 1) * HD_B] for g in range(grp)], axis=0)
           for k in kvs]
    ss = [lax.dot_general(qgs[k], kv[:, k * HD_B:(k + 1) * HD_B], (((1,), (1,)), ((), ())),
                          preferred_element_type=F32) + tbl_ref[variant, k] for k in kvs]
    ms = [jnp.maximum(jnp.max(ss[k], axis=-1, keepdims=True), sinkcol_ref[k]) for k in kvs]
    ps = [jnp.exp(ss[k] - ms[k]) for k in kvs]
    dens = [jnp.sum(ps[k], axis=-1, keepdims=True) + jnp.exp(sinkcol_ref[k] - ms[k]) for k in kvs]
    os_ = [jnp.dot(ps[k].astype(BF16), kv[:, kvw + k * HD_B:kvw + (k + 1) * HD_B], preferred_element_type=F32)
           / dens[k] for k in kvs]
    for k in kvs:
        for g in range(grp):
            h = k * grp + g
            o_ref[:, h * HD_B:(h + 1) * HD_B] = os_[k][g * tq:(g + 1) * tq, :].astype(o_ref.dtype)


def win_prompt(q, kv, sinks, rel_bias, batch, seq):
    n_heads = q.shape[1] // HD_B
    nq = seq // WINDOW
    t = np.arange(WINDOW)[:, None]
    s = np.arange(2 * WINDOW)[None, :]
    bucket = jnp.asarray(_t5_bucket_table(t + WINDOW - s))
    return pl.pallas_call(
        functools.partial(_win_prompt_kernel, n_heads=n_heads),
        grid=(batch, nq),
        in_specs=[pl.BlockSpec(memory_space=pltpu.SMEM),
                  pl.BlockSpec(memory_space=pltpu.SMEM),
                  pl.BlockSpec((WINDOW, q.shape[1]), lambda b, i: (b * nq + i, 0)),
                  pl.BlockSpec((WINDOW, kv.shape[1]), lambda b, i: (jnp.maximum(b * nq + i - 1, 0), 0)),
                  pl.BlockSpec((WINDOW, kv.shape[1]), lambda b, i: (b * nq + i, 0)),
                  pl.BlockSpec((WINDOW, 2 * WINDOW), lambda b, i: (0, 0))],
        out_specs=pl.BlockSpec((WINDOW, q.shape[1]), lambda b, i: (b * nq + i, 0)),
        out_shape=jax.ShapeDtypeStruct(q.shape, BF16),
        scratch_shapes=[pltpu.VMEM((2, KV_B, (n_heads // KV_B) * WINDOW, 2 * WINDOW), F32),
                        pltpu.VMEM((KV_B, (n_heads // KV_B) * WINDOW, 1), F32)],
        compiler_params=_cparams(("arbitrary", "arbitrary")),
        name="win_prompt",
    )(rel_bias, sinks, q, kv, kv, bucket)


def _win_decode_kernel(rb_ref, sink_ref, q_ref, k_ref, v_ref, bkt_ref, o_ref, tbl_ref, sinkcol_ref, *, grp, t_s):
    rk, nk = bkt_ref.shape
    n_sub = q_ref.shape[0]

    @pl.when(pl.program_id(0) == 0)
    def _():
        t_idx = lax.broadcasted_iota(jnp.int32, (rk, nk), 0) % t_s
        j_idx = lax.broadcasted_iota(jnp.int32, (rk, nk), 1)
        dist = t_idx + WINDOW - j_idx
        valid = (dist >= 0) & (dist < WINDOW)
        bucket = bkt_ref[...]
        row_g = lax.broadcasted_iota(jnp.int32, (rk, nk), 0) // t_s
        row = lax.broadcasted_iota(jnp.int32, (rk, 1), 0)
        for k in range(KV_B):
            def body(g, tbl):
                acc = jnp.zeros((rk, nk), F32)
                for bb in range(N_BUCKETS):
                    acc = jnp.where(bucket == bb, rb_ref[bb, k * grp + g], acc)
                return jnp.where(row_g == g, acc, tbl)
            tbl = lax.fori_loop(0, grp, body, jnp.zeros((rk, nk), F32))
            tbl_ref[k] = jnp.where(valid, tbl, NEG)
            sink = jnp.zeros((rk, 1), F32)
            for g in range(grp):
                sink = jnp.where(row // t_s == g, sink_ref[k * grp + g], sink)
            sinkcol_ref[k] = sink

    scale = HD_B ** -0.5
    items = [(s, k) for s in range(n_sub) for k in range(KV_B)]
    ss = [lax.dot_general(q_ref[s, k].astype(BF16), k_ref[s, k].astype(BF16), (((1,), (1,)), ((), ())),
                          preferred_element_type=F32) * scale + tbl_ref[k] for s, k in items]
    ms = [jnp.maximum(jnp.max(ss[n], axis=-1, keepdims=True), sinkcol_ref[k]) for n, (s, k) in enumerate(items)]
    ps = [jnp.exp(ss[n] - ms[n]) for n in range(len(items))]
    dens = [jnp.sum(ps[n], axis=-1, keepdims=True) + jnp.exp(sinkcol_ref[k] - ms[n])
            for n, (s, k) in enumerate(items)]
    for n, (s, k) in enumerate(items):
        o_ref[s, k] = jnp.dot(ps[n].astype(BF16), v_ref[s, k].astype(BF16), preferred_element_type=F32) / dens[n]


def win_decode(q, k, v, sinks, rel_bias, t_s):
    n_seq, _, rk, _ = q.shape
    nk = k.shape[2]
    grp = rk // t_s
    n_sub = WIN_DECODE_SEQS
    t = (np.arange(rk) % t_s)[:, None]
    j = np.arange(nk)[None, :]
    bucket = jnp.asarray(_t5_bucket_table(t + WINDOW - j))
    blk_q = pl.BlockSpec((n_sub, KV_B, rk, HD_B), lambda b: (b, 0, 0, 0))
    blk_k = pl.BlockSpec((n_sub, KV_B, nk, HD_B), lambda b: (b, 0, 0, 0))
    return pl.pallas_call(
        functools.partial(_win_decode_kernel, grp=grp, t_s=t_s),
        grid=(n_seq // n_sub,),
        in_specs=[pl.BlockSpec(memory_space=pltpu.SMEM),
                  pl.BlockSpec(memory_space=pltpu.SMEM),
                  blk_q, blk_k, blk_k,
                  pl.BlockSpec((rk, nk), lambda b: (0, 0))],
        out_specs=blk_q,
        out_shape=jax.ShapeDtypeStruct(q.shape, F32),
        scratch_shapes=[pltpu.VMEM((KV_B, rk, nk), F32),
                        pltpu.VMEM((KV_B, rk, 1), F32)],
        compiler_params=_cparams(("arbitrary",)),
        name="win_decode",
    )(rel_bias, sinks, q, k, v, bucket)


def _heads_to_rows(x, n_seq, t_s, n_kv, hd):
    grp = x.shape[1] // (n_kv * hd)
    x = x.reshape(n_seq, t_s, n_kv, grp, hd).transpose(0, 2, 3, 1, 4)
    return x.reshape(n_seq, n_kv, grp * t_s, hd)


def _rows_to_heads(x, t_s):
    n_seq, n_kv, rk, hd = x.shape
    grp = rk // t_s
    x = x.reshape(n_seq, n_kv, grp, t_s, hd).transpose(0, 3, 1, 2, 4)
    return x.reshape(n_seq * t_s, n_kv * grp * hd)


def kernel(x_prompt, x_sample, cache_k_a, cache_v_a, page_table, cache_k_b, cache_v_b, state_conv,
           w_qkv_a, w_o_a, sb_bias, g_kv, w_kv_b, w_q_b, w_o_b, sinks_b, rel_bias,
           g_pre_mix, g_post_mix, g_pre_ffn, g_post_ffn, w_gate, w_up, w_down, conv_w, conv_b):
    batch, seq, d = x_prompt.shape
    n_seq, t_s, _ = x_sample.shape
    n_a = w_qkv_a.shape[0]
    depth = w_gate.shape[0]
    d_ff = w_gate.shape[2]
    page = cache_k_a.shape[2]
    h_a = w_o_a.shape[1] // HD_A
    mp, ms = batch * seq, n_seq * t_s
    qw = h_a * HD_A
    kvw_a = KV_A * HD_A
    kvw_b = KV_B * HD_B
    assert t_s >= CONV_W - 1 and seq % TM_PROMPT == 0

    cache_k4 = cache_k_a.reshape(cache_k_a.shape[0], cache_k_a.shape[1], page * KV_A, HD_A)
    cache_v4 = cache_v_a.reshape(cache_v_a.shape[0], cache_v_a.shape[1], page * KV_A, HD_A)
    w_kv_b3 = w_kv_b.reshape(1, d, 2 * kvw_b)

    hp = x_prompt.reshape(mp, d)
    hs = x_sample.reshape(ms, d)
    ka_s, va_s, conv_p, conv_s = [], [], [], []
    kv_all = kv_p = kb_s = vb_s = kq_s = vq_s = None
    for l in range(depth):
        if l < n_a:
            q_p, k_all, v_all = norm_qkv(hp, g_pre_mix[l], w_qkv_a, l, TM_PROMPT, (HD_A ** -0.5) * LOG2E, kv_all)
            kv_all = (k_all, v_all)
            qkv_s = norm_matmul(hs, g_pre_mix[l], w_qkv_a, l, ms, TN_PROJ)
            op = sb_prompt(q_p, k_all, v_all, l, sb_bias[l], batch, seq)
            q_s = _heads_to_rows(qkv_s[:, :qw], n_seq, t_s, KV_A, HD_A)
            k_new = qkv_s[:, qw:qw + kvw_a]
            v_new = qkv_s[:, qw + kvw_a:]
            pad = ((0, 0), (0, (page - t_s) * KV_A), (0, 0))
            o_s = sb_decode(q_s, jnp.pad(k_new.reshape(n_seq, t_s * KV_A, HD_A), pad),
                            jnp.pad(v_new.reshape(n_seq, t_s * KV_A, HD_A), pad),
                            cache_k4, cache_v4, l, page_table, sb_bias[l], pages=SB_PAGES_PER_STEP)
            o_s = _rows_to_heads(o_s, t_s)
            hp = proj_res(op, w_o_a, l, g_post_mix[l], hp, TM_PROJ_RES)
            hs = proj_res(o_s, w_o_a, l, g_post_mix[l], hs, ms)
            ka_s.append(k_new.reshape(n_seq, t_s, KV_A, HD_A))
            va_s.append(v_new.reshape(n_seq, t_s, KV_A, HD_A))
        else:
            j = l - n_a
            if j == 0:
                kv_p = norm_matmul(hp, g_kv, w_kv_b3, 0, TM_PROMPT, 2 * kvw_b)
                kv_s = norm_matmul(hs, g_kv, w_kv_b3, 0, ms, 2 * kvw_b)
                kb_s = jnp.concatenate([cache_k_b, kv_s[:, :kvw_b].reshape(n_seq, t_s, KV_B, HD_B)], axis=1)
                vb_s = jnp.concatenate([cache_v_b, kv_s[:, kvw_b:].reshape(n_seq, t_s, KV_B, HD_B)], axis=1)
                padk = ((0, 0), (0, 0), (0, 2 * WINDOW - (WINDOW + t_s)), (0, 0))
                kq_s = jnp.pad(kb_s.transpose(0, 2, 1, 3), padk)
                vq_s = jnp.pad(vb_s.transpose(0, 2, 1, 3), padk)
            q_p = norm_matmul(hp, g_pre_mix[l], w_q_b, j, TM_PROMPT, TN_PROJ, BF16, HD_B ** -0.5)
            q_s = norm_matmul(hs, g_pre_mix[l], w_q_b, j, ms, TN_PROJ)
            op = win_prompt(q_p, kv_p, sinks_b[j], rel_bias, batch, seq)
            o_s = win_decode(_heads_to_rows(q_s, n_seq, t_s, KV_B, HD_B), kq_s, vq_s, sinks_b[j], rel_bias, t_s)
            o_s = _rows_to_heads(o_s, t_s)
            hp = proj_res(op, w_o_b, j, g_post_mix[l], hp, TM_PROJ_RES)
            hs = proj_res(o_s, w_o_b, j, g_post_mix[l], hs, ms)
        st = state_conv[l]
        zeros = jnp.zeros((n_seq, t_s - 1, d_ff), F32)
        s1 = jnp.concatenate([st[:, 1:2], zeros], axis=1).reshape(ms, d_ff)
        s2 = jnp.concatenate([st, zeros[:, 1:]], axis=1).reshape(ms, d_ff)
        hs, gt_s, wg16, wu16, wd16 = conv_ffn(hs, g_pre_ffn, g_post_ffn, w_gate, w_up, w_down, conv_w, conv_b, l,
                                              tm=ms, tf=TF_SAMPLE, seq_len=t_s, n_split=1, state_rows=(s1, s2))
        hp, gt_p = conv_ffn(hp, g_pre_ffn, g_post_ffn, wg16, wu16, wd16, conv_w, conv_b, l,
                            tm=TM_PROMPT, tf=TF_PROMPT, seq_len=seq, n_split=FFN_ROW_GROUPS)
        tiles_per_seq = seq // TM_PROMPT
        conv_p.append(gt_p.reshape(batch, tiles_per_seq, 8, d_ff)[:, -1, 8 - (CONV_W - 1):, :])
        conv_s.append(gt_s.reshape(n_seq, t_s, d_ff)[:, t_s - (CONV_W - 1):, :])

    kvp4 = kv_p.reshape(batch, seq, 2 * kvw_b)[:, seq - WINDOW:].reshape(batch, WINDOW, 2, KV_B, HD_B)
    return (hp.reshape(batch, seq, d), hs.reshape(n_seq, t_s, d),
            kv_all[0].reshape(n_a, batch, seq, KV_A, HD_A), kv_all[1].reshape(n_a, batch, seq, KV_A, HD_A),
            jnp.stack(ka_s), jnp.stack(va_s),
            kvp4[:, :, 0], kvp4[:, :, 1],
            kb_s[:, -WINDOW:], vb_s[:, -WINDOW:],
            jnp.stack(conv_p), jnp.stack(conv_s))
```
